```python
import math
import jax, jax.numpy as jnp
from jax import lax
import numpy as np

D_MODEL = 1024
BATCH = 8
SEQ = 2048
DEPTH = 2
DEC_BATCH = 16
DEC_SEQ = 16
PAST_LEN = 4096

CHUNK = 64
N_EVEN = (DEPTH + 1) // 2
N_ODD = DEPTH // 2
H_A = 4
DK_A = 64
DV_A = 128
R_A = 16
GATE_NORM_A = 16.0
A_QK = H_A * DK_A
A_V = H_A * DV_A
H_B = 8
D_B = 64
B_W = H_B * D_B
BAND_CHUNKS = 8
WIN_B = BAND_CHUNKS * CHUNK
MAX_REL = 128
H_C = 16
D_C = 64
C_W = H_C * D_C
Q_BLOCK = 128
FOX_BIAS_INIT = 2.0
D_FF = 2816
EPS = 1e-6
E_IN = 2 * A_QK + 2 * A_V + R_A + 3 * B_W
O_IN = 3 * C_W + H_C

kernel_name = "hybrid_streaming_gla_band_fox_macaron"


def rmsnorm(x, g):
    xf = x.astype(jnp.float32)
    y = xf * lax.rsqrt(jnp.mean(xf * xf, axis=-1, keepdims=True) + EPS)
    return (y * g.astype(jnp.float32)).astype(x.dtype)


def swiglu(x, w_gu, w_down):
    g, u = jnp.split(x @ w_gu, 2, axis=-1)
    return (jax.nn.silu(g) * u) @ w_down


def split_cols(h, sizes):
    idx = np.cumsum(sizes)[:-1].tolist()
    return jnp.split(h, idx, axis=-1)


def gla_blocked(q, k, v, log_a, s0):
    B, T, H, _ = q.shape
    L = min(CHUNK, T)
    n = T // L
    f32 = jnp.float32

    def blk(t):
        return t.astype(f32).reshape(B, n, L, H, -1).transpose(1, 0, 3, 2, 4)

    qc, kc, vc, ac = blk(q), blk(k), blk(v), blk(log_a)
    bcum = jnp.cumsum(ac, axis=-2)
    b_last = bcum[..., -1:, :]
    q_in = qc * jnp.exp(bcum)
    k_in = kc * jnp.exp(-bcum)
    causal = np.tril(np.ones((L, L), dtype=bool))
    att = jnp.where(causal, jnp.einsum('nbhik,nbhjk->nbhij', q_in, k_in), 0.0)
    o_intra = jnp.einsum('nbhij,nbhjv->nbhiv', att, vc)
    k_state = kc * jnp.exp(b_last - bcum)
    a_total = jnp.exp(b_last[..., 0, :])

    def step(s, inp):
        qi, ks, vi, at = inp
        o = jnp.einsum('bhik,bhkv->bhiv', qi, s)
        s = at[..., None] * s + jnp.einsum('bhjk,bhjv->bhkv', ks, vi)
        return s, o

    s_fin, o_inter = lax.scan(step, s0.astype(f32), (q_in, k_state, vc, a_total))
    o = (o_intra + o_inter).transpose(1, 0, 3, 2, 4).reshape(B, T, H, -1)
    return o, s_fin


def band_attention_prompt(q, k, v, rel_bias):
    B, T, H, D = q.shape
    n = T // CHUNK
    P = BAND_CHUNKS
    pad = ((0, 0), (P * CHUNK, 0), (0, 0), (0, 0))
    kp = jnp.pad(k, pad).reshape(B, n + P, CHUNK, H, D)
    vp = jnp.pad(v, pad).reshape(B, n + P, CHUNK, H, D)
    kb = jnp.concatenate([kp[:, o:o + n] for o in range(P + 1)], axis=2)
    vb = jnp.concatenate([vp[:, o:o + n] for o in range(P + 1)], axis=2)
    qc = q.reshape(B, n, CHUNK, H, D)
    s = jnp.einsum('bnihd,bnjhd->bnhij', qc, kb).astype(jnp.float32) * (D ** -0.5)
    i = np.arange(CHUNK)[:, None]
    j = np.arange((P + 1) * CHUNK)[None, :]
    dist = i - (j - P * CHUNK)
    bias = rel_bias[:, np.clip(dist, -MAX_REL, MAX_REL) + MAX_REL].astype(jnp.float32)
    kpos = (np.arange(n)[:, None] - P) * CHUNK + j
    valid = (kpos >= 0)[None, :, None, None, :]
    s = jnp.where(valid, s + bias[None, None], -jnp.inf)
    p = jax.nn.softmax(s, axis=-1).astype(v.dtype)
    return jnp.einsum('bnhij,bnjhd->bnihd', p, vb).reshape(B, T, H * D)


def band_attention_sample(q, k, v, ck, cv, rel_bias):
    B, S, H, D = q.shape
    Lc = ck.shape[1]
    kk = jnp.concatenate([ck, k], axis=1)
    vv = jnp.concatenate([cv, v], axis=1)
    s = jnp.einsum('bihd,bjhd->bhij', q, kk).astype(jnp.float32) * (D ** -0.5)
    qpos = PAST_LEN + np.arange(S)
    kpos = np.concatenate([PAST_LEN - Lc + np.arange(Lc), PAST_LEN + np.arange(S)])
    dist = qpos[:, None] - kpos[None, :]
    bias = rel_bias[:, np.clip(dist, -MAX_REL, MAX_REL) + MAX_REL].astype(jnp.float32)
    p = jax.nn.softmax(s + bias[None], axis=-1).astype(vv.dtype)
    return jnp.einsum('bhij,bjhd->bihd', p, vv).reshape(B, S, H * D)


def fox_prompt(q, k, v, logf):
    B, T, H, D = q.shape
    nq = T // Q_BLOCK
    F = jnp.cumsum(logf, axis=1)
    Fk = F.transpose(0, 2, 1)[:, :, None, :]
    qb = q.reshape(B, nq, Q_BLOCK, H, D).transpose(1, 0, 2, 3, 4)
    Fb = F.reshape(B, nq, Q_BLOCK, H).transpose(1, 0, 3, 2)
    kpos = jnp.arange(T)

    def one(args):
        qi, Fi, bi = args
        s = jnp.einsum('bihd,bjhd->bhij', qi, k).astype(jnp.float32) * (D ** -0.5)
        s = s + Fi[..., :, None] - Fk
        qpos = bi * Q_BLOCK + jnp.arange(Q_BLOCK)
        s = jnp.where(qpos[:, None] >= kpos[None, :], s, -jnp.inf)
        p = jax.nn.softmax(s, axis=-1).astype(v.dtype)
        return jnp.einsum('bhij,bjhd->bihd', p, v)

    o = lax.map(one, (qb, Fb, jnp.arange(nq)))
    return o.transpose(1, 0, 2, 3, 4).reshape(B, T, H * D)


def fox_sample(q, k, v, logf, ck, cv, clogf):
    B, S, H, D = q.shape
    Lc = ck.shape[1]
    clogf = clogf.astype(jnp.float32)
    fq = jnp.cumsum(logf, axis=1)
    c_after = lax.cumsum(clogf, axis=1, reverse=True) - clogf
    fk = jnp.concatenate([-c_after, fq], axis=1)
    kk = jnp.concatenate([ck, k], axis=1)
    vv = jnp.concatenate([cv, v], axis=1)
    s = jnp.einsum('bihd,bjhd->bhij', q, kk).astype(jnp.float32) * (D ** -0.5)
    s = s + fq.transpose(0, 2, 1)[..., :, None] - fk.transpose(0, 2, 1)[..., None, :]
    valid = np.concatenate([np.ones((S, Lc), dtype=bool), np.tril(np.ones((S, S), dtype=bool))], axis=1)
    s = jnp.where(valid, s, -jnp.inf)
    p = jax.nn.softmax(s, axis=-1).astype(vv.dtype)
    return jnp.einsum('bhij,bjhd->bihd', p, vv).reshape(B, S, H * D)


def even_mixer(xn, w_in, w_alpha_up, b_alpha, gla_g, rel_bias, w_out, state):
    B, T, _ = xn.shape
    aq, ak, av, ag, ar, bq, bk, bv = split_cols(xn @ w_in, [A_QK, A_QK, A_V, A_V, R_A, B_W, B_W, B_W])
    q = aq.reshape(B, T, H_A, DK_A) * (DK_A ** -0.5)
    k = ak.reshape(B, T, H_A, DK_A)
    v = av.reshape(B, T, H_A, DV_A)
    log_a = (jax.nn.log_sigmoid((ar @ w_alpha_up + b_alpha).astype(jnp.float32)) / GATE_NORM_A).reshape(B, T, H_A, DK_A)
    s0 = jnp.zeros((B, H_A, DK_A, DV_A), jnp.float32) if state is None else state[0]
    o_a, s_fin = gla_blocked(q, k, v, log_a, s0)
    o_a = rmsnorm(o_a, gla_g).reshape(B, T, A_V).astype(xn.dtype) * jax.nn.silu(ag)
    qb = bq.reshape(B, T, H_B, D_B)
    kb = bk.reshape(B, T, H_B, D_B)
    vb = bv.reshape(B, T, H_B, D_B)
    if state is None:
        o_b = band_attention_prompt(qb, kb, vb, rel_bias)
        keep = min(WIN_B, T)
        nk, nv = kb[:, T - keep:], vb[:, T - keep:]
    else:
        o_b = band_attention_sample(qb, kb, vb, state[1], state[2], rel_bias)
        nk, nv = kb, vb
    y = jnp.concatenate([o_a, o_b], axis=-1) @ w_out
    return y, (s_fin, nk, nv)


def odd_mixer(xn, w_in, b_f, w_out, state):
    B, T, _ = xn.shape
    cq, ck, cv, cf = split_cols(xn @ w_in, [C_W, C_W, C_W, H_C])
    q = cq.reshape(B, T, H_C, D_C)
    k = ck.reshape(B, T, H_C, D_C)
    v = cv.reshape(B, T, H_C, D_C)
    logf = jax.nn.log_sigmoid((cf + b_f).astype(jnp.float32))
    if state is None:
        o = fox_prompt(q, k, v, logf)
    else:
        o = fox_sample(q, k, v, logf, state[0], state[1], state[2])
    return o @ w_out, (k, v, logf)


def trunk(x, layer_states, norm_g, ffn_w_gu, ffn_w_down, even_w_in, gla_w_alpha_up, gla_b_alpha,
          gla_norm_g, band_rel_bias, even_w_out, odd_w_in, fox_b_f, odd_w_out, final_norm_g):
    new_states = []
    for l in range(DEPTH):
        x = x + 0.5 * swiglu(rmsnorm(x, norm_g[l, 0]), ffn_w_gu[l, 0], ffn_w_down[l, 0])
        xn = rmsnorm(x, norm_g[l, 1])
        st = None if layer_states is None else layer_states[l]
        i = l // 2
        if l % 2 == 0:
            y, ns = even_mixer(xn, even_w_in[i], gla_w_alpha_up[i], gla_b_alpha[i], gla_norm_g[i],
                               band_rel_bias[i], even_w_out[i], st)
        else:
            y, ns = odd_mixer(xn, odd_w_in[i], fox_b_f[i], odd_w_out[i], st)
        x = x + y
        x = x + 0.5 * swiglu(rmsnorm(x, norm_g[l, 2]), ffn_w_gu[l, 1], ffn_w_down[l, 1])
        new_states.append(ns)
    return rmsnorm(x, final_norm_g), new_states


def setup_inputs(seed: int = 0) -> dict:
    key = jax.random.key(seed)
    ks = iter(jax.random.split(key, 32))

    def nrm(shape, scale):
        return jax.random.normal(next(ks), shape, jnp.float32) * scale

    lb = min(WIN_B, PAST_LEN)
    return {
        "x_prompt": nrm((BATCH, SEQ, D_MODEL), 1.0),
        "x_sample": nrm((DEC_BATCH, DEC_SEQ, D_MODEL), 1.0),
        "state_gla": nrm((N_EVEN, DEC_BATCH, H_A, DK_A, DV_A), 0.5),
        "cache_band_k": nrm((N_EVEN, DEC_BATCH, lb, H_B, D_B), 1.0),
        "cache_band_v": nrm((N_EVEN, DEC_BATCH, lb, H_B, D_B), 1.0),
        "cache_fox_k": nrm((N_ODD, DEC_BATCH, PAST_LEN, H_C, D_C), 1.0),
        "cache_fox_v": nrm((N_ODD, DEC_BATCH, PAST_LEN, H_C, D_C), 1.0),
        "cache_fox_logf": jax.nn.log_sigmoid(FOX_BIAS_INIT + nrm((N_ODD, DEC_BATCH, PAST_LEN, H_C), 1.0)),
        "norm_g": 1.0 + nrm((DEPTH, 3, D_MODEL), 0.05),
        "ffn_w_gu": nrm((DEPTH, 2, D_MODEL, 2 * D_FF), D_MODEL ** -0.5),
        "ffn_w_down": nrm((DEPTH, 2, D_FF, D_MODEL), D_FF ** -0.5),
        "even_w_in": nrm((N_EVEN, D_MODEL, E_IN), D_MODEL ** -0.5),
        "gla_w_alpha_up": nrm((N_EVEN, R_A, A_QK), R_A ** -0.5),
        "gla_b_alpha": nrm((N_EVEN, A_QK), 0.1),
        "gla_norm_g": 1.0 + nrm((N_EVEN, DV_A), 0.05),
        "band_rel_bias": nrm((N_EVEN, H_B, 2 * MAX_REL + 1), 0.5),
        "even_w_out": nrm((N_EVEN, A_V + B_W, D_MODEL), (A_V + B_W) ** -0.5),
        "odd_w_in": nrm((N_ODD, D_MODEL, O_IN), D_MODEL ** -0.5),
        "fox_b_f": FOX_BIAS_INIT + nrm((N_ODD, H_C), 0.1),
        "odd_w_out": nrm((N_ODD, C_W, D_MODEL), C_W ** -0.5),
        "final_norm_g": 1.0 + nrm((D_MODEL,), 0.05),
    }


def reference(x_prompt, x_sample, state_gla, cache_band_k, cache_band_v, cache_fox_k, cache_fox_v,
              cache_fox_logf, norm_g, ffn_w_gu, ffn_w_down, even_w_in, gla_w_alpha_up, gla_b_alpha,
              gla_norm_g, band_rel_bias, even_w_out, odd_w_in, fox_b_f, odd_w_out, final_norm_g):
    weights = (norm_g, ffn_w_gu, ffn_w_down, even_w_in, gla_w_alpha_up, gla_b_alpha, gla_norm_g,
               band_rel_bias, even_w_out, odd_w_in, fox_b_f, odd_w_out, final_norm_g)
    sample_states = []
    for l in range(DEPTH):
        i = l // 2
        if l % 2 == 0:
            sample_states.append((state_gla[i], cache_band_k[i], cache_band_v[i]))
        else:
            sample_states.append((cache_fox_k[i], cache_fox_v[i], cache_fox_logf[i]))
    y_prompt, ns_p = trunk(x_prompt, None, *weights)
    y_sample, ns_s = trunk(x_sample, sample_states, *weights)
    ev = range(0, DEPTH, 2)
    od = range(1, DEPTH, 2)
    p_state_gla = jnp.stack([ns_p[l][0] for l in ev])
    p_band_k = jnp.stack([ns_p[l][1] for l in ev])
    p_band_v = jnp.stack([ns_p[l][2] for l in ev])
    p_fox_k = jnp.stack([ns_p[l][0] for l in od])
    p_fox_v = jnp.stack([ns_p[l][1] for l in od])
    p_fox_logf = jnp.stack([ns_p[l][2] for l in od])
    s_state_gla = jnp.stack([ns_s[l][0] for l in ev])
    s_band_k = jnp.stack([ns_s[l][1] for l in ev])
    s_band_v = jnp.stack([ns_s[l][2] for l in ev])
    s_fox_k = jnp.stack([ns_s[l][0] for l in od])
    s_fox_v = jnp.stack([ns_s[l][1] for l in od])
    s_fox_logf = jnp.stack([ns_s[l][2] for l in od])
    return (y_prompt, y_sample, p_state_gla, p_band_k, p_band_v, p_fox_k, p_fox_v, p_fox_logf,
            s_state_gla, s_band_k, s_band_v, s_fox_k, s_fox_v, s_fox_logf)
```

```python
import functools

import jax
import jax.numpy as jnp
from jax import lax
from jax.experimental import pallas as pl
from jax.experimental.pallas import tpu as pltpu

F32 = jnp.float32
BF16 = jnp.bfloat16
HIGHEST = lax.Precision.HIGHEST
NT = (((1,), (1,)), ((), ()))
TN = (((0,), (0,)), ((), ()))

EPS = 1e-6
CHUNK = 64
BAND_CHUNKS = 8
MAX_REL = 128
GATE_NORM_A = 16.0
VMEM_LIMIT_BYTES = 56 * 1024 * 1024


def _params(*sem):
    return pltpu.CompilerParams(dimension_semantics=sem, vmem_limit_bytes=VMEM_LIMIT_BYTES)


def _resident(shape):
    nd = len(shape)
    return pl.BlockSpec(shape, lambda *_: (0,) * nd, pipeline_mode=pl.Buffered(1))


def _rows(tm, width):
    return pl.BlockSpec((tm, width), lambda i: (i, 0))


def _rms(x, g):
    ms = jnp.mean(x * x, axis=-1, keepdims=True)
    return x * lax.rsqrt(ms + EPS) * g


def _log_sigmoid(x):
    return jnp.minimum(x, 0.0) - jnp.log1p(jnp.exp(-jnp.abs(x)))


def _dot(a, b):
    return jnp.dot(a, b, preferred_element_type=F32)


def _ffn_kernel(*refs, n_proj, has_final, d_ff):
    it = iter(refs)
    x_ref = next(it)
    proj = [(next(it), next(it)) for _ in range(n_proj)]
    g_ref, wgu_ref, wd_ref = next(it), next(it), next(it)
    gf_ref = next(it) if has_final else None
    out_ref = next(it)

    x = x_ref[...]
    for o_ref, w_ref in proj:
        x = x + _dot(o_ref[...], w_ref[...])
    xn = _rms(x, g_ref[...]).astype(BF16)
    h = _dot(xn, wgu_ref[...])
    a = (jax.nn.silu(h[:, :d_ff]) * h[:, d_ff:]).astype(BF16)
    y = x + 0.5 * _dot(a, wd_ref[...])
    if has_final:
        y = _rms(y, gf_ref[...])
    out_ref[...] = y


def _ffn(x, g, wgu, wd, proj=(), final_g=None, *, tm):
    n, d = x.shape
    d_ff = wd.shape[0]
    args, specs = [x], [_rows(tm, d)]
    for o, w in proj:
        args += [o, w]
        specs += [_rows(tm, o.shape[1]), _resident(w.shape)]
    args += [g, wgu, wd]
    specs += [_resident(g.shape), _resident(wgu.shape), _resident(wd.shape)]
    if final_g is not None:
        args.append(final_g)
        specs.append(_resident(final_g.shape))
    return pl.pallas_call(
        functools.partial(_ffn_kernel, n_proj=len(proj), has_final=final_g is not None, d_ff=d_ff),
        grid=(n // tm,),
        in_specs=specs,
        out_specs=_rows(tm, d),
        out_shape=jax.ShapeDtypeStruct((n, d), F32),
        compiler_params=_params("parallel"),
        name="ffn",
    )(*args)


def _even_in_kernel(x_ref, g_ref, w_ref, wr_ref, wup_ref, ba_ref,
                    qa_ref, ka_ref, va_ref, gate_ref, la_ref, bq_ref, bk_ref, bv_ref, bk32_ref, bv32_ref,
                    *, a_qk, a_v, b_w, dk_a, d_b):
    xn = _rms(x_ref[...], g_ref[...]).astype(BF16)
    h = _dot(xn, w_ref[...])
    o = 0
    qa_ref[...] = h[:, o:o + a_qk] * (dk_a ** -0.5)
    o += a_qk
    ka_ref[...] = h[:, o:o + a_qk]
    o += a_qk
    va_ref[...] = h[:, o:o + a_v].astype(BF16)
    o += a_v
    gate_ref[...] = jax.nn.silu(h[:, o:o + a_v])
    o += a_v
    bq_ref[...] = (h[:, o:o + b_w] * (d_b ** -0.5)).astype(BF16)
    o += b_w
    bk = h[:, o:o + b_w]
    o += b_w
    bv = h[:, o:o + b_w]
    bk32_ref[...] = bk
    bv32_ref[...] = bv
    bk_ref[...] = bk.astype(BF16)
    bv_ref[...] = bv.astype(BF16)
    r = _dot(xn, wr_ref[...]).astype(BF16)
    z = _dot(r, wup_ref[...]) + ba_ref[...]
    la_ref[...] = _log_sigmoid(z) / GATE_NORM_A


def _even_in(x, g, w_main, w_r, w_up, b_alpha, *, tm, a_qk, a_v, b_w, dk_a, d_b):
    n, d = x.shape
    widths = [(a_qk, F32), (a_qk, F32), (a_v, BF16), (a_v, F32), (a_qk, F32),
              (b_w, BF16), (b_w, BF16), (b_w, BF16), (b_w, F32), (b_w, F32)]
    return pl.pallas_call(
        functools.partial(_even_in_kernel, a_qk=a_qk, a_v=a_v, b_w=b_w, dk_a=dk_a, d_b=d_b),
        grid=(n // tm,),
        in_specs=[_rows(tm, d), _resident(g.shape), _resident(w_main.shape), _resident(w_r.shape),
                  _resident(w_up.shape), _resident(b_alpha.shape)],
        out_specs=[_rows(tm, w) for w, _ in widths],
        out_shape=[jax.ShapeDtypeStruct((n, w), dt) for w, dt in widths],
        compiler_params=_params("parallel"),
        name="even_in",
    )(x, g, w_main, w_r, w_up, b_alpha)


def _odd_in_kernel(x_ref, g_ref, w_ref, wf_ref, bf_ref, q_ref, k_ref, v_ref, k32_ref, v32_ref, lf_ref, *, c_w, d_c):
    xn = _rms(x_ref[...], g_ref[...]).astype(BF16)
    h = _dot(xn, w_ref[...])
    q_ref[...] = (h[:, :c_w] * (d_c ** -0.5)).astype(BF16)
    k = h[:, c_w:2 * c_w]
    v = h[:, 2 * c_w:]
    k32_ref[...] = k
    v32_ref[...] = v
    k_ref[...] = k.astype(BF16)
    v_ref[...] = v.astype(BF16)
    lf_ref[...] = _log_sigmoid(_dot(xn, wf_ref[...]) + bf_ref[...])


def _odd_in(x, g, w_main, w_f, b_f, *, tm, c_w, d_c, h_c):
    n, d = x.shape
    outs = [(c_w, BF16), (c_w, BF16), (c_w, BF16), (c_w, F32), (c_w, F32), (h_c, F32)]
    return pl.pallas_call(
        functools.partial(_odd_in_kernel, c_w=c_w, d_c=d_c),
        grid=(n // tm,),
        in_specs=[_rows(tm, d), _resident(g.shape), _resident(w_main.shape), _resident(w_f.shape),
                  _resident(b_f.shape)],
        out_specs=[_rows(tm, w) for w, _ in outs],
        out_shape=[jax.ShapeDtypeStruct((n, w), dt) for w, dt in outs],
        compiler_params=_params("parallel"),
        name="odd_in",
    )(x, g, w_main, w_f, b_f)


def _gla_kernel(q_ref, k_ref, v_ref, gate_ref, la_ref, s0_ref, g_ref, o_ref, sfin_ref, s_scr, *, bb, L, nh, dk, dv):
    c = pl.program_id(1)

    @pl.when(c == 0)
    def _():
        s_scr[...] = s0_ref[...]

    row = lax.broadcasted_iota(jnp.int32, (L, L), 0)
    col = lax.broadcasted_iota(jnp.int32, (L, L), 1)
    causal = row >= col
    tri = causal.astype(F32)
    eye = (lax.broadcasted_iota(jnp.int32, (dk, dk), 0) == lax.broadcasted_iota(jnp.int32, (dk, dk), 1))
    g = g_ref[...]
    for b in range(bb):
        la = la_ref[b]
        bcum = jnp.dot(tri, la, precision=HIGHEST, preferred_element_type=F32)
        b_last = bcum[L - 1:L, :]
        q = q_ref[b]
        k = k_ref[b]
        q_in = (q * jnp.exp(bcum)).astype(BF16)
        k_in = (k * jnp.exp(-bcum)).astype(BF16)
        k_st = (k * jnp.exp(b_last - bcum)).astype(BF16)
        a_tot = jnp.exp(b_last)
        v = v_ref[b]
        outs = []
        for h in range(nh):
            sk = slice(h * dk, (h + 1) * dk)
            sv = slice(h * dv, (h + 1) * dv)
            qh, kh, ksh, vh = q_in[:, sk], k_in[:, sk], k_st[:, sk], v[:, sv]
            att = lax.dot_general(qh, kh, NT, preferred_element_type=F32)
            att = jnp.where(causal, att, 0.0).astype(BF16)
            s_old = s_scr[b, h]
            o = _dot(att, vh) + _dot(qh, s_old.astype(BF16))
            a_col = jnp.sum(jnp.where(eye, jnp.broadcast_to(a_tot[:, sk], (dk, dk)), 0.0), axis=1, keepdims=True)
            s_scr[b, h] = a_col * s_old + lax.dot_general(ksh, vh, TN, preferred_element_type=F32)
            ms = jnp.mean(o * o, axis=-1, keepdims=True)
            outs.append(o * lax.rsqrt(ms + EPS) * g)
        o_ref[b] = (jnp.concatenate(outs, axis=-1) * gate_ref[b]).astype(BF16)

    @pl.when(c == pl.num_programs(1) - 1)
    def _():
        sfin_ref[...] = s_scr[...]


def _gla(q, k, v, gate, la, s0, g, *, bb, L):
    B, T, a_qk = q.shape
    a_v = v.shape[-1]
    _, nh, dk, dv = s0.shape
    n = T // L

    def tok(w):
        return pl.BlockSpec((bb, L, w), lambda i, c: (i, c, 0))

    st = pl.BlockSpec((bb, nh, dk, dv), lambda i, c: (i, 0, 0, 0))
    return pl.pallas_call(
        functools.partial(_gla_kernel, bb=bb, L=L, nh=nh, dk=dk, dv=dv),
        grid=(B // bb, n),
        in_specs=[tok(a_qk), tok(a_qk), tok(a_v), tok(a_v), tok(a_qk), st, pl.BlockSpec(g.shape, lambda i, c: (0, 0))],
        out_specs=[tok(a_v), st],
        out_shape=[jax.ShapeDtypeStruct((B, T, a_v), BF16), jax.ShapeDtypeStruct(s0.shape, F32)],
        scratch_shapes=[pltpu.VMEM((bb, nh, dk, dv), F32)],
        compiler_params=_params("parallel", "arbitrary"),
        name="gla",
    )(q, k, v, gate, la, s0, g)


def _bias_table_kernel(rel_ref, out_ref, *, chunk, win, n_rel):
    rel = rel_ref[...]
    m_idx = lax.broadcasted_iota(jnp.int32, (n_rel, win), 0)
    j = lax.broadcasted_iota(jnp.int32, (n_rel, win), 1)
    for i in range(chunk):
        want = jnp.clip((win - chunk) + i - j, -MAX_REL, MAX_REL) + MAX_REL
        onehot = (m_idx == want).astype(F32)
        out_ref[:, i, :] = jnp.dot(rel, onehot, precision=HIGHEST, preferred_element_type=F32)


def _bias_table(rel_padded, *, chunk, win):
    nh, n_rel = rel_padded.shape
    return pl.pallas_call(
        functools.partial(_bias_table_kernel, chunk=chunk, win=win, n_rel=n_rel),
        out_shape=jax.ShapeDtypeStruct((nh, chunk, win), F32),
        compiler_params=_params(),
        name="band_bias_table",
    )(rel_padded)


def _band_prompt_kernel(q_ref, k_ref, v_ref, bias_ref, o_ref, *, nh, d, chunk, win):
    c = pl.program_id(1)
    start = pl.multiple_of(c * chunk, chunk)
    first_valid = jnp.maximum(BAND_CHUNKS - c, 0) * chunk
    valid = lax.broadcasted_iota(jnp.int32, (chunk, win), 1) >= first_valid
    q = q_ref[0]
    kw = k_ref[0, pl.ds(start, win), :]
    vw = v_ref[0, pl.ds(start, win), :]
    outs = []
    for h in range(nh):
        sl = slice(h * d, (h + 1) * d)
        s = lax.dot_general(q[:, sl], kw[:, sl], NT, preferred_element_type=F32)
        s = jnp.where(valid, s + bias_ref[h], -jnp.inf)
        m = jnp.max(s, axis=-1, keepdims=True)
        e = jnp.exp(s - m)
        l = jnp.sum(e, axis=-1, keepdims=True)
        outs.append(_dot(e.astype(BF16), vw[:, sl]) / l)
    o_ref[0] = jnp.concatenate(outs, axis=-1).astype(BF16)


def _band_prompt(q, kpad, vpad, bias, *, nh, d):
    B, T, w = q.shape
    tp = kpad.shape[1]
    win = (BAND_CHUNKS + 1) * CHUNK
    kv = pl.BlockSpec((1, tp, w), lambda b, c: (b, 0, 0))
    return pl.pallas_call(
        functools.partial(_band_prompt_kernel, nh=nh, d=d, chunk=CHUNK, win=win),
        grid=(B, T // CHUNK),
        in_specs=[pl.BlockSpec((1, CHUNK, w), lambda b, c: (b, c, 0)), kv, kv,
                  pl.BlockSpec(bias.shape, lambda b, c: (0, 0, 0))],
        out_specs=pl.BlockSpec((1, CHUNK, w), lambda b, c: (b, c, 0)),
        out_shape=jax.ShapeDtypeStruct((B, T, w), BF16),
        compiler_params=_params("parallel", "arbitrary"),
        name="band_prompt",
    )(q, kpad, vpad, bias)


def _band_sample_kernel(q_ref, ck_ref, cv_ref, kn_ref, vn_ref, bias_ref, o_ref, *, nh, d, lc, s_new):
    q = q_ref[0]
    ck = ck_ref[0].astype(BF16)
    cv = cv_ref[0].astype(BF16)
    kn = kn_ref[0]
    vn = vn_ref[0]
    outs = []
    for h in range(nh):
        sl = slice(h * d, (h + 1) * d)
        bias = bias_ref[h]
        sc = lax.dot_general(q[:, sl], ck[:, sl], NT, preferred_element_type=F32) + bias[:s_new, :lc]
        sn = lax.dot_general(q[:, sl], kn[:, sl], NT, preferred_element_type=F32) + bias[:s_new, lc:lc + s_new]
        m = jnp.maximum(jnp.max(sc, axis=-1, keepdims=True), jnp.max(sn, axis=-1, keepdims=True))
        ec = jnp.exp(sc - m)
        en = jnp.exp(sn - m)
        l = jnp.sum(ec, axis=-1, keepdims=True) + jnp.sum(en, axis=-1, keepdims=True)
        outs.append((_dot(ec.astype(BF16), cv[:, sl]) + _dot(en.astype(BF16), vn[:, sl])) / l)
    o_ref[0] = jnp.concatenate(outs, axis=-1).astype(BF16)


def _band_sample(q, ck, cv, kn, vn, bias, *, nh, d):
    B, s_new, w = q.shape
    lc = ck.shape[1]
    new = pl.BlockSpec((1, s_new, w), lambda b: (b, 0, 0))
    old = pl.BlockSpec((1, lc, w), lambda b: (b, 0, 0))
    return pl.pallas_call(
        functools.partial(_band_sample_kernel, nh=nh, d=d, lc=lc, s_new=s_new),
        grid=(B,),
        in_specs=[new, old, old, new, new, pl.BlockSpec(bias.shape, lambda b: (0, 0, 0))],
        out_specs=new,
        out_shape=jax.ShapeDtypeStruct((B, s_new, w), BF16),
        compiler_params=_params("parallel"),
        name="band_sample",
    )(q, ck, cv, kn, vn, bias)


def _cumsum_t_kernel(x_ref, out_ref, *, blk, suffix):
    T, nh = x_ref.shape[1], x_ref.shape[2]
    tri = (lax.broadcasted_iota(jnp.int32, (blk, blk), 0) >= lax.broadcasted_iota(jnp.int32, (blk, blk), 1)).astype(F32)
    eye = (lax.broadcasted_iota(jnp.int32, (nh, nh), 0) == lax.broadcasted_iota(jnp.int32, (nh, nh), 1)).astype(F32)
    carry = jnp.zeros((1, nh), F32)
    pieces = []
    for i in range(T // blk):
        c = jnp.dot(tri, x_ref[0, i * blk:(i + 1) * blk, :], precision=HIGHEST, preferred_element_type=F32) + carry
        carry = c[blk - 1:blk, :]
        pieces.append(c)
    for i, c in enumerate(pieces):
        if suffix:
            c = carry - c
        out_ref[0, :, i * blk:(i + 1) * blk] = lax.dot_general(eye, c, NT, precision=HIGHEST,
                                                              preferred_element_type=F32)


def _cumsum_t(x, *, suffix):
    nb, T, nh = x.shape
    return pl.pallas_call(
        functools.partial(_cumsum_t_kernel, blk=256 if T % 256 == 0 else T, suffix=suffix),
        grid=(nb,),
        in_specs=[pl.BlockSpec((1, T, nh), lambda b: (b, 0, 0))],
        out_specs=pl.BlockSpec((1, nh, T), lambda b: (b, 0, 0)),
        out_shape=jax.ShapeDtypeStruct((nb, nh, T), F32),
        compiler_params=_params("parallel"),
        name="cumsum_t",
    )(x)


def _fox_prompt_kernel(q_ref, k_ref, v_ref, ft_ref, o_ref, *, tq, d):
    qi = pl.program_id(2)
    q = q_ref[0]
    causal = lax.broadcasted_iota(jnp.int32, (tq, tq), 0) >= lax.broadcasted_iota(jnp.int32, (tq, tq), 1)
    outs = []
    for hh in range(2):
        sl = slice(hh * d, (hh + 1) * d)
        qh = q[:, sl]

        def block(kv, carry, masked, hh=hh, sl=sl, qh=qh):
            m, l, acc = carry
            start = pl.multiple_of(kv * tq, tq)
            ks = k_ref[0, pl.ds(start, tq), :][:, sl]
            vs = v_ref[0, pl.ds(start, tq), :][:, sl]
            s = lax.dot_general(qh, ks, NT, preferred_element_type=F32) - ft_ref[0, 0, hh, pl.ds(kv, 1), :]
            if masked:
                s = jnp.where(causal, s, -jnp.inf)
            m_new = jnp.maximum(m, jnp.max(s, axis=-1, keepdims=True))
            alpha = jnp.exp(m - m_new)
            e = jnp.exp(s - m_new)
            l = alpha * l + jnp.sum(e, axis=-1, keepdims=True)
            acc = alpha * acc + _dot(e.astype(BF16), vs)
            return m_new, l, acc

        init = (jnp.full((tq, 1), -jnp.inf, F32), jnp.zeros((tq, 1), F32), jnp.zeros((tq, d), F32))
        carry = lax.fori_loop(0, qi, lambda kv, cr: block(kv, cr, False), init)
        _, l, acc = block(qi, carry, True)
        outs.append(acc / l)
    o_ref[0] = jnp.concatenate(outs, axis=-1).astype(BF16)


def _fox_prompt(q, k, v, ft, *, tq, d):
    B, T, w = q.shape
    pair = 2 * d
    kv = pl.BlockSpec((1, T, pair), lambda b, p, i: (b, 0, p))
    qo = pl.BlockSpec((1, tq, pair), lambda b, p, i: (b, i, p))
    return pl.pallas_call(
        functools.partial(_fox_prompt_kernel, tq=tq, d=d),
        grid=(B, w // pair, T // tq),
        in_specs=[qo, kv, kv, pl.BlockSpec((1, 1, 2, T // tq, tq), lambda b, p, i: (b, p, 0, 0, 0))],
        out_specs=qo,
        out_shape=jax.ShapeDtypeStruct((B, T, w), BF16),
        compiler_params=_params("parallel", "parallel", "arbitrary"),
        name="fox_prompt",
    )(q, k, v, ft)


def _fox_sample_kernel(q_ref, kc_ref, vc_ref, ct_ref, kn_ref, vn_ref, lf_ref, o_ref, qbd, m_s, l_s, acc_s,
                       *, nh, d, s_new):
    kv = pl.program_id(1)
    R, W = nh * s_new, nh * d
    assert s_new & (s_new - 1) == 0 and d & (d - 1) == 0, "head index uses shifts"
    same_head = ((lax.broadcasted_iota(jnp.int32, (R, W), 0) >> (s_new.bit_length() - 1))
                 == (lax.broadcasted_iota(jnp.int32, (R, W), 1) >> (d.bit_length() - 1)))

    @pl.when(kv == 0)
    def _():
        q = q_ref[0].astype(F32)
        qt = jnp.broadcast_to(q[None], (nh, s_new, W)).reshape(R, W)
        qbd[...] = jnp.where(same_head, qt, 0.0).astype(BF16)
        m_s[...] = jnp.full(m_s.shape, -jnp.inf, F32)
        l_s[...] = jnp.zeros(l_s.shape, F32)
        acc_s[...] = jnp.zeros(acc_s.shape, F32)

    def update(s, v):
        m = m_s[...]
        m_new = jnp.maximum(m, jnp.max(s, axis=-1, keepdims=True))
        alpha = jnp.exp(m - m_new)
        e = jnp.exp(s - m_new)
        l_s[...] = alpha * l_s[...] + jnp.sum(e, axis=-1, keepdims=True)
        acc_s[...] = alpha * acc_s[...] + _dot(e.astype(BF16), v)
        m_s[...] = m_new

    tk = kc_ref.shape[1]
    s = lax.dot_general(qbd[...], kc_ref[0].astype(BF16), NT, preferred_element_type=F32)
    bias = jnp.broadcast_to(ct_ref[0][:, None, :], (nh, s_new, tk)).reshape(R, tk)
    update(s + bias, vc_ref[0].astype(BF16))

    @pl.when(kv == pl.num_programs(1) - 1)
    def _():
        tri = (lax.broadcasted_iota(jnp.int32, (s_new, s_new), 0)
               >= lax.broadcasted_iota(jnp.int32, (s_new, s_new), 1)).astype(F32)
        eye = (lax.broadcasted_iota(jnp.int32, (nh, nh), 0)
               == lax.broadcasted_iota(jnp.int32, (nh, nh), 1)).astype(F32)
        fq = jnp.dot(tri, lf_ref[0], precision=HIGHEST, preferred_element_type=F32)
        fqt = lax.dot_general(eye, fq, NT, precision=HIGHEST, preferred_element_type=F32)
        bias_n = jnp.broadcast_to(fqt[:, None, :], (nh, s_new, s_new)).reshape(R, s_new)
        sn = lax.dot_general(qbd[...], kn_ref[0], NT, preferred_element_type=F32) - bias_n
        qidx = lax.broadcasted_iota(jnp.int32, (R, s_new), 0) & (s_new - 1)
        jidx = lax.broadcasted_iota(jnp.int32, (R, s_new), 1)
        update(jnp.where(qidx >= jidx, sn, -jnp.inf), vn_ref[0])
        o = acc_s[...] / l_s[...]
        o = jnp.where(same_head, o, 0.0).reshape(nh, s_new, W)
        o_ref[0] = jnp.sum(o, axis=0).astype(BF16)


def _fox_sample(q, kc, vc, ct, kn, vn, lf, *, nh, d, tk):
    B, s_new, w = q.shape
    lc = kc.shape[1]
    R = nh * s_new
    new = pl.BlockSpec((1, s_new, w), lambda b, j: (b, 0, 0))
    old = pl.BlockSpec((1, tk, w), lambda b, j: (b, j, 0))
    return pl.pallas_call(
        functools.partial(_fox_sample_kernel, nh=nh, d=d, s_new=s_new),
        grid=(B, lc // tk),
        in_specs=[new, old, old, pl.BlockSpec((1, nh, tk), lambda b, j: (b, 0, j)), new, new,
                  pl.BlockSpec((1, s_new, nh), lambda b, j: (b, 0, 0))],
        out_specs=new,
        out_shape=jax.ShapeDtypeStruct((B, s_new, w), BF16),
        scratch_shapes=[pltpu.VMEM((R, w), BF16), pltpu.VMEM((R, 1), F32), pltpu.VMEM((R, 1), F32),
                        pltpu.VMEM((R, w), F32)],
        compiler_params=_params("parallel", "arbitrary"),
        name="fox_sample",
    )(q, kc, vc, ct, kn, vn, lf)


def _trunk(x, state, wts, dims, *, tm, gla_bb):
    B, T, D = x.shape
    n = B * T
    depth = wts["norm_g"].shape[0]
    h_a, dk_a, dv_a, h_b, d_b, h_c, d_c = dims
    a_qk, a_v, b_w, c_w = h_a * dk_a, h_a * dv_a, h_b * d_b, h_c * d_c
    xs = x.reshape(n, D)
    new_states = []
    pending = ()
    for l in range(depth):
        i = l // 2
        g = wts["norm_g"][l]
        xs = _ffn(xs, g[0:1], wts["w_gu"][l][0], wts["w_down"][l][0], pending, tm=tm)
        if l % 2 == 0:
            qa, ka, va, gate, la, bq, bk, bv, bk32, bv32 = _even_in(
                xs, g[1:2], wts["even_main"][i], wts["even_r"][i], wts["alpha_up"][i], wts["b_alpha"][i],
                tm=tm, a_qk=a_qk, a_v=a_v, b_w=b_w, dk_a=dk_a, d_b=d_b)
            r3 = lambda t: t.reshape(B, T, t.shape[-1])
            s0 = jnp.zeros((B, h_a, dk_a, dv_a), F32) if state is None else state[l][0]
            o_a, s_fin = _gla(r3(qa), r3(ka), r3(va), r3(gate), r3(la), s0, wts["gla_g"][i],
                              bb=gla_bb, L=min(CHUNK, T))
            if state is None:
                pad = ((0, 0), (BAND_CHUNKS * CHUNK, 0), (0, 0))
                o_b = _band_prompt(r3(bq), jnp.pad(r3(bk), pad), jnp.pad(r3(bv), pad), wts["band_bias"][i],
                                   nh=h_b, d=d_b)
                keep = min(BAND_CHUNKS * CHUNK, T)
                nk, nv = r3(bk32)[:, T - keep:], r3(bv32)[:, T - keep:]
            else:
                ck, cv = state[l][1], state[l][2]
                o_b = _band_sample(r3(bq), ck.reshape(B, ck.shape[1], b_w), cv.reshape(B, cv.shape[1], b_w),
                                   r3(bk), r3(bv), wts["band_bias"][i], nh=h_b, d=d_b)
                nk, nv = r3(bk32), r3(bv32)
            new_states.append((s_fin, nk.reshape(B, -1, h_b, d_b), nv.reshape(B, -1, h_b, d_b)))
            pending = ((o_a.reshape(n, a_v), wts["even_out_a"][i]), (o_b.reshape(n, b_w), wts["even_out_b"][i]))
        else:
            q, k, v, k32, v32, lf = _odd_in(xs, g[1:2], wts["odd_main"][i], wts["odd_f"][i], wts["b_f"][i],
                                             tm=tm, c_w=c_w, d_c=d_c, h_c=h_c)
            r3 = lambda t: t.reshape(B, T, t.shape[-1])
            if state is None:
                tq = 256
                ft = _cumsum_t(r3(lf), suffix=False).reshape(B, h_c // 2, 2, T // tq, tq)
                o = _fox_prompt(r3(q), r3(k), r3(v), ft, tq=tq, d=d_c)
            else:
                ck, cv, clf = state[l]
                lc = ck.shape[1]
                ct = _cumsum_t(clf.astype(F32), suffix=True)
                o = _fox_sample(r3(q), ck.reshape(B, lc, c_w), cv.reshape(B, lc, c_w), ct, r3(k), r3(v), r3(lf),
                                nh=h_c, d=d_c, tk=1024)
            new_states.append((r3(k32).reshape(B, T, h_c, d_c), r3(v32).reshape(B, T, h_c, d_c), r3(lf)))
            pending = ((o.reshape(n, c_w), wts["odd_out"][i]),)
        last = l == depth - 1
        xs = _ffn(xs, g[2:3], wts["w_gu"][l][1], wts["w_down"][l][1], pending,
                  wts["final_g"] if last else None, tm=tm)
        pending = ()
    return xs.reshape(B, T, D), new_states


def kernel(x_prompt, x_sample, state_gla, cache_band_k, cache_band_v, cache_fox_k, cache_fox_v, cache_fox_logf,
           norm_g, ffn_w_gu, ffn_w_down, even_w_in, gla_w_alpha_up, gla_b_alpha, gla_norm_g, band_rel_bias,
           even_w_out, odd_w_in, fox_b_f, odd_w_out, final_norm_g):
    depth = norm_g.shape[0]
    _, _, h_a, dk_a, dv_a = state_gla.shape
    h_b, d_b = cache_band_k.shape[-2:]
    h_c, d_c = cache_fox_k.shape[-2:]
    r_a = gla_w_alpha_up.shape[1]
    a_qk, a_v, b_w, c_w = h_a * dk_a, h_a * dv_a, h_b * d_b, h_c * d_c
    assert cache_band_k.shape[2] == BAND_CHUNKS * CHUNK and band_rel_bias.shape[-1] == 2 * MAX_REL + 1

    r0 = 2 * a_qk + 2 * a_v
    n_rel = 384
    rel_padded = jnp.pad(band_rel_bias, ((0, 0), (0, 0), (0, n_rel - band_rel_bias.shape[-1])))
    win = (BAND_CHUNKS + 1) * CHUNK
    wts = {
        "norm_g": norm_g,
        "w_gu": ffn_w_gu.astype(BF16),
        "w_down": ffn_w_down.astype(BF16),
        "even_main": jnp.concatenate([even_w_in[:, :, :r0], even_w_in[:, :, r0 + r_a:]], axis=-1).astype(BF16),
        "even_r": even_w_in[:, :, r0:r0 + r_a].astype(BF16),
        "alpha_up": gla_w_alpha_up.astype(BF16),
        "b_alpha": gla_b_alpha[:, None, :],
        "gla_g": gla_norm_g[:, None, :],
        "band_bias": [_bias_table(rel_padded[i], chunk=CHUNK, win=win) for i in range(rel_padded.shape[0])],
        "even_out_a": even_w_out[:, :a_v].astype(BF16),
        "even_out_b": even_w_out[:, a_v:].astype(BF16),
        "odd_main": odd_w_in[:, :, :3 * c_w].astype(BF16),
        "odd_f": odd_w_in[:, :, 3 * c_w:].astype(BF16),
        "b_f": fox_b_f[:, None, :],
        "odd_out": odd_w_out.astype(BF16),
        "final_g": final_norm_g[None, :],
    }
    dims = (h_a, dk_a, dv_a, h_b, d_b, h_c, d_c)

    sample_states = []
    for l in range(depth):
        i = l // 2
        if l % 2 == 0:
            sample_states.append((state_gla[i], cache_band_k[i], cache_band_v[i]))
        else:
            sample_states.append((cache_fox_k[i], cache_fox_v[i], cache_fox_logf[i]))

    y_prompt, ns_p = _trunk(x_prompt, None, wts, dims, tm=512, gla_bb=8)
    y_sample, ns_s = _trunk(x_sample, sample_states, wts, dims, tm=256, gla_bb=8)

    ev = range(0, depth, 2)
    od = range(1, depth, 2)
    stack = lambda ns, layers, j: jnp.stack([ns[l][j] for l in layers])
    return (y_prompt, y_sample,
            stack(ns_p, ev, 0), stack(ns_p, ev, 1), stack(ns_p, ev, 2),
            stack(ns_p, od, 0), stack(ns_p, od, 1), stack(ns_p, od, 2),
            stack(ns_s, ev, 0), stack(ns_s, ev, 1), stack(ns_s, ev, 2),
            stack(ns_s, od, 0), stack(ns_s, od, 1), stack(ns_s, od, 2))
```

```python
import functools

import jax
import jax.numpy as jnp
from jax import lax
from jax.experimental import pallas as pl
from jax.experimental.pallas import tpu as pltpu

F32 = jnp.float32
BF16 = jnp.bfloat16
HIGHEST = lax.Precision.HIGHEST
NT = (((1,), (1,)), ((), ()))
TN = (((0,), (0,)), ((), ()))

EPS = 1e-6
CHUNK = 64
BAND_CHUNKS = 8
MAX_REL = 128
GATE_NORM_A = 16.0
VMEM_LIMIT_BYTES = 56 * 1024 * 1024


def _params(*sem):
    return pltpu.CompilerParams(dimension_semantics=sem, vmem_limit_bytes=VMEM_LIMIT_BYTES)


def _resident(shape):
    nd = len(shape)
    return pl.BlockSpec(shape, lambda *_: (0,) * nd, pipeline_mode=pl.Buffered(1))


def _rows(tm, width):
    return pl.BlockSpec((tm, width), lambda i: (i, 0))


def _rms(x, g):
    ms = jnp.mean(x * x, axis=-1, keepdims=True)
    return x * lax.rsqrt(ms + EPS) * g


def _log_sigmoid(x):
    return jnp.minimum(x, 0.0) - jnp.log1p(jnp.exp(-jnp.abs(x)))


def _dot(a, b):
    return jnp.dot(a, b, preferred_element_type=F32)


def _dot_nt(a, b):
    return lax.dot_general(a, b, NT, preferred_element_type=F32)


def _ffn_kernel(*refs, n_proj, has_final, d_ff):
    it = iter(refs)
    x_ref = next(it)
    proj = [(next(it), next(it)) for _ in range(n_proj)]
    g_ref, wgu_ref, wd_ref = next(it), next(it), next(it)
    gf_ref = next(it) if has_final else None
    out_ref = next(it)

    x = x_ref[...]
    for o_ref, w_ref in proj:
        x = x + _dot(o_ref[...], w_ref[...])
    xn = _rms(x, g_ref[...]).astype(BF16)
    h = _dot(xn, wgu_ref[...])
    a = (jax.nn.silu(h[:, :d_ff]) * h[:, d_ff:]).astype(BF16)
    y = x + 0.5 * _dot(a, wd_ref[...])
    if has_final:
        y = _rms(y, gf_ref[...])
    out_ref[...] = y


def _ffn(x, g, wgu, wd, proj=(), final_g=None, *, tm):
    n, d = x.shape
    d_ff = wd.shape[0]
    args, specs = [x], [_rows(tm, d)]
    for o, w in proj:
        args += [o, w]
        specs += [_rows(tm, o.shape[1]), _resident(w.shape)]
    args += [g, wgu, wd]
    specs += [_resident(g.shape), _resident(wgu.shape), _resident(wd.shape)]
    if final_g is not None:
        args.append(final_g)
        specs.append(_resident(final_g.shape))
    return pl.pallas_call(
        functools.partial(_ffn_kernel, n_proj=len(proj), has_final=final_g is not None, d_ff=d_ff),
        grid=(n // tm,),
        in_specs=specs,
        out_specs=_rows(tm, d),
        out_shape=jax.ShapeDtypeStruct((n, d), F32),
        compiler_params=_params("parallel"),
        name="ffn",
    )(*args)


def _even_in_kernel(x_ref, g_ref, w_ref, wr_ref, wup_ref, ba_ref,
                    qa_ref, ka_ref, va_ref, gate_ref, la_ref, bq_ref, bk_ref, bv_ref, bk32_ref, bv32_ref,
                    *, a_qk, a_v, b_w, dk_a, d_b):
    xn = _rms(x_ref[...], g_ref[...]).astype(BF16)
    h = _dot(xn, w_ref[...])
    o = 0
    qa_ref[...] = h[:, o:o + a_qk] * (dk_a ** -0.5)
    o += a_qk
    ka_ref[...] = h[:, o:o + a_qk]
    o += a_qk
    va_ref[...] = h[:, o:o + a_v].astype(BF16)
    o += a_v
    gate_ref[...] = jax.nn.silu(h[:, o:o + a_v])
    o += a_v
    bq_ref[...] = (h[:, o:o + b_w] * (d_b ** -0.5)).astype(BF16)
    o += b_w
    bk = h[:, o:o + b_w]
    o += b_w
    bv = h[:, o:o + b_w]
    bk32_ref[...] = bk
    bv32_ref[...] = bv
    bk_ref[...] = bk.astype(BF16)
    bv_ref[...] = bv.astype(BF16)
    r = _dot(xn, wr_ref[...]).astype(BF16)
    z = _dot(r, wup_ref[...]) + ba_ref[...]
    la_ref[...] = _log_sigmoid(z) / GATE_NORM_A


def _even_in(x, g, w_main, w_r, w_up, b_alpha, *, tm, a_qk, a_v, b_w, dk_a, d_b):
    n, d = x.shape
    widths = [(a_qk, F32), (a_qk, F32), (a_v, BF16), (a_v, F32), (a_qk, F32),
              (b_w, BF16), (b_w, BF16), (b_w, BF16), (b_w, F32), (b_w, F32)]
    return pl.pallas_call(
        functools.partial(_even_in_kernel, a_qk=a_qk, a_v=a_v, b_w=b_w, dk_a=dk_a, d_b=d_b),
        grid=(n // tm,),
        in_specs=[_rows(tm, d), _resident(g.shape), _resident(w_main.shape), _resident(w_r.shape),
                  _resident(w_up.shape), _resident(b_alpha.shape)],
        out_specs=[_rows(tm, w) for w, _ in widths],
        out_shape=[jax.ShapeDtypeStruct((n, w), dt) for w, dt in widths],
        compiler_params=_params("parallel"),
        name="even_in",
    )(x, g, w_main, w_r, w_up, b_alpha)


def _odd_in_kernel(x_ref, g_ref, w_ref, wf_ref, bf_ref, q_ref, k_ref, v_ref, k32_ref, v32_ref, lf_ref, *, c_w, d_c):
    xn = _rms(x_ref[...], g_ref[...]).astype(BF16)
    h = _dot(xn, w_ref[...])
    q_ref[...] = (h[:, :c_w] * (d_c ** -0.5)).astype(BF16)
    k = h[:, c_w:2 * c_w]
    v = h[:, 2 * c_w:]
    k32_ref[...] = k
    v32_ref[...] = v
    k_ref[...] = k.astype(BF16)
    v_ref[...] = v.astype(BF16)
    lf_ref[...] = _log_sigmoid(_dot(xn, wf_ref[...]) + bf_ref[...])


def _odd_in(x, g, w_main, w_f, b_f, *, tm, c_w, d_c, h_c):
    n, d = x.shape
    outs = [(c_w, BF16), (c_w, BF16), (c_w, BF16), (c_w, F32), (c_w, F32), (h_c, F32)]
    return pl.pallas_call(
        functools.partial(_odd_in_kernel, c_w=c_w, d_c=d_c),
        grid=(n // tm,),
        in_specs=[_rows(tm, d), _resident(g.shape), _resident(w_main.shape), _resident(w_f.shape),
                  _resident(b_f.shape)],
        out_specs=[_rows(tm, w) for w, _ in outs],
        out_shape=[jax.ShapeDtypeStruct((n, w), dt) for w, dt in outs],
        compiler_params=_params("parallel"),
        name="odd_in",
    )(x, g, w_main, w_f, b_f)


def _odd_in_t_kernel(x_ref, g_ref, wq_ref, wkvt_ref, wft_ref, bf_ref, q_ref, kt_ref, vt_ref, lft_ref, *, nh, d_c):
    xn = _rms(x_ref[...], g_ref[...]).astype(BF16)
    tm = xn.shape[0]
    c_w = nh * d_c
    q_ref[...] = (_dot(xn, wq_ref[...]) * (d_c ** -0.5)).astype(BF16)
    ht = _dot_nt(wkvt_ref[...], xn)
    kt_ref[0] = ht[:c_w].reshape(nh, d_c, tm)
    vt_ref[0] = ht[c_w:].reshape(nh, d_c, tm)
    lft_ref[0] = _log_sigmoid(_dot_nt(wft_ref[...], xn) + bf_ref[...])


def _odd_in_t(x, g, wq, wkvt, wft, b_f_col, *, B, T, tm, nh, d_c):
    n, d = x.shape
    nb = T // tm
    c_w = nh * d_c
    tok = lambda w: pl.BlockSpec((tm, w), lambda b, i: (b * nb + i, 0))
    kv = pl.BlockSpec((1, nh, d_c, tm), lambda b, i: (b, 0, 0, i))
    return pl.pallas_call(
        functools.partial(_odd_in_t_kernel, nh=nh, d_c=d_c),
        grid=(B, nb),
        in_specs=[tok(d), _resident(g.shape), _resident(wq.shape), _resident(wkvt.shape), _resident(wft.shape),
                  _resident(b_f_col.shape)],
        out_specs=[tok(c_w), kv, kv, pl.BlockSpec((1, nh, tm), lambda b, i: (b, 0, i))],
        out_shape=[jax.ShapeDtypeStruct((n, c_w), BF16), jax.ShapeDtypeStruct((B, nh, d_c, T), F32),
                   jax.ShapeDtypeStruct((B, nh, d_c, T), F32), jax.ShapeDtypeStruct((B, nh, T), F32)],
        compiler_params=_params("parallel", "parallel"),
        name="odd_in_t",
    )(x, g, wq, wkvt, wft, b_f_col)


def _gla_kernel(q_ref, k_ref, v_ref, gate_ref, la_ref, s0_ref, g_ref, o_ref, sfin_ref, s_scr, *, bb, L, nh, dk, dv):
    c = pl.program_id(1)

    @pl.when(c == 0)
    def _():
        s_scr[...] = s0_ref[...]

    row = lax.broadcasted_iota(jnp.int32, (L, L), 0)
    col = lax.broadcasted_iota(jnp.int32, (L, L), 1)
    causal = row >= col
    tri = causal.astype(F32)
    eye = (lax.broadcasted_iota(jnp.int32, (dk, dk), 0) == lax.broadcasted_iota(jnp.int32, (dk, dk), 1))
    g = g_ref[...]
    for b in range(bb):
        la = la_ref[b]
        bcum = jnp.dot(tri, la, precision=HIGHEST, preferred_element_type=F32)
        b_last = bcum[L - 1:L, :]
        q = q_ref[b]
        k = k_ref[b]
        q_in = (q * jnp.exp(bcum)).astype(BF16)
        k_in = (k * jnp.exp(-bcum)).astype(BF16)
        k_st = (k * jnp.exp(b_last - bcum)).astype(BF16)
        a_tot = jnp.exp(b_last)
        v = v_ref[b]
        outs = []
        for h in range(nh):
            sk = slice(h * dk, (h + 1) * dk)
            sv = slice(h * dv, (h + 1) * dv)
            qh, kh, ksh, vh = q_in[:, sk], k_in[:, sk], k_st[:, sk], v[:, sv]
            att = _dot_nt(qh, kh)
            att = jnp.where(causal, att, 0.0).astype(BF16)
            s_old = s_scr[b, h]
            o = _dot(att, vh) + _dot(qh, s_old.astype(BF16))
            a_col = jnp.sum(jnp.where(eye, jnp.broadcast_to(a_tot[:, sk], (dk, dk)), 0.0), axis=1, keepdims=True)
            s_scr[b, h] = a_col * s_old + lax.dot_general(ksh, vh, TN, preferred_element_type=F32)
            ms = jnp.mean(o * o, axis=-1, keepdims=True)
            outs.append(o * lax.rsqrt(ms + EPS) * g)
        o_ref[b] = (jnp.concatenate(outs, axis=-1) * gate_ref[b]).astype(BF16)

    @pl.when(c == pl.num_programs(1) - 1)
    def _():
        sfin_ref[...] = s_scr[...]


def _gla(q, k, v, gate, la, s0, g, *, bb, L):
    B, T, a_qk = q.shape
    a_v = v.shape[-1]
    _, nh, dk, dv = s0.shape
    n = T // L

    def tok(w):
        return pl.BlockSpec((bb, L, w), lambda i, c: (i, c, 0))

    st = pl.BlockSpec((bb, nh, dk, dv), lambda i, c: (i, 0, 0, 0))
    return pl.pallas_call(
        functools.partial(_gla_kernel, bb=bb, L=L, nh=nh, dk=dk, dv=dv),
        grid=(B // bb, n),
        in_specs=[tok(a_qk), tok(a_qk), tok(a_v), tok(a_v), tok(a_qk), st, pl.BlockSpec(g.shape, lambda i, c: (0, 0))],
        out_specs=[tok(a_v), st],
        out_shape=[jax.ShapeDtypeStruct((B, T, a_v), BF16), jax.ShapeDtypeStruct(s0.shape, F32)],
        scratch_shapes=[pltpu.VMEM((bb, nh, dk, dv), F32)],
        compiler_params=_params("parallel", "arbitrary"),
        name="gla",
    )(q, k, v, gate, la, s0, g)


def _bias_table_kernel(rel_ref, out_ref, *, chunk, win, n_rel):
    rel = rel_ref[...]
    m_idx = lax.broadcasted_iota(jnp.int32, (n_rel, win), 0)
    j = lax.broadcasted_iota(jnp.int32, (n_rel, win), 1)
    for i in range(chunk):
        want = jnp.clip((win - chunk) + i - j, -MAX_REL, MAX_REL) + MAX_REL
        onehot = (m_idx == want).astype(F32)
        out_ref[:, i, :] = jnp.dot(rel, onehot, precision=HIGHEST, preferred_element_type=F32)


def _bias_table(rel_padded, *, chunk, win):
    nh, n_rel = rel_padded.shape
    return pl.pallas_call(
        functools.partial(_bias_table_kernel, chunk=chunk, win=win, n_rel=n_rel),
        out_shape=jax.ShapeDtypeStruct((nh, chunk, win), F32),
        compiler_params=_params(),
        name="band_bias_table",
    )(rel_padded)


def _band_prompt_kernel(q_ref, k_ref, v_ref, bias_ref, o_ref, *, nh, d, chunk, win):
    c = pl.program_id(1)
    start = pl.multiple_of(c * chunk, chunk)
    first_valid = jnp.maximum(BAND_CHUNKS - c, 0) * chunk
    valid = lax.broadcasted_iota(jnp.int32, (chunk, win), 1) >= first_valid
    q = q_ref[0]
    kw = k_ref[0, pl.ds(start, win), :]
    vw = v_ref[0, pl.ds(start, win), :]
    outs = []
    for h in range(nh):
        sl = slice(h * d, (h + 1) * d)
        s = _dot_nt(q[:, sl], kw[:, sl])
        s = jnp.where(valid, s + bias_ref[h], -jnp.inf)
        m = jnp.max(s, axis=-1, keepdims=True)
        e = jnp.exp(s - m)
        l = jnp.sum(e, axis=-1, keepdims=True)
        outs.append(_dot(e.astype(BF16), vw[:, sl]) / l)
    o_ref[0] = jnp.concatenate(outs, axis=-1).astype(BF16)


def _band_prompt(q, kpad, vpad, bias, *, nh, d):
    B, T, w = q.shape
    tp = kpad.shape[1]
    win = (BAND_CHUNKS + 1) * CHUNK
    kv = pl.BlockSpec((1, tp, w), lambda b, c: (b, 0, 0))
    return pl.pallas_call(
        functools.partial(_band_prompt_kernel, nh=nh, d=d, chunk=CHUNK, win=win),
        grid=(B, T // CHUNK),
        in_specs=[pl.BlockSpec((1, CHUNK, w), lambda b, c: (b, c, 0)), kv, kv,
                  pl.BlockSpec(bias.shape, lambda b, c: (0, 0, 0))],
        out_specs=pl.BlockSpec((1, CHUNK, w), lambda b, c: (b, c, 0)),
        out_shape=jax.ShapeDtypeStruct((B, T, w), BF16),
        compiler_params=_params("parallel", "arbitrary"),
        name="band_prompt",
    )(q, kpad, vpad, bias)


def _band_sample_kernel(q_ref, ck_ref, cv_ref, kn_ref, vn_ref, bias_ref, o_ref, *, nh, d, lc, s_new):
    q = q_ref[0]
    ck = ck_ref[0].astype(BF16)
    cv = cv_ref[0].astype(BF16)
    kn = kn_ref[0]
    vn = vn_ref[0]
    outs = []
    for h in range(nh):
        sl = slice(h * d, (h + 1) * d)
        bias = bias_ref[h]
        sc = _dot_nt(q[:, sl], ck[:, sl]) + bias[:s_new, :lc]
        sn = _dot_nt(q[:, sl], kn[:, sl]) + bias[:s_new, lc:lc + s_new]
        m = jnp.maximum(jnp.max(sc, axis=-1, keepdims=True), jnp.max(sn, axis=-1, keepdims=True))
        ec = jnp.exp(sc - m)
        en = jnp.exp(sn - m)
        l = jnp.sum(ec, axis=-1, keepdims=True) + jnp.sum(en, axis=-1, keepdims=True)
        outs.append((_dot(ec.astype(BF16), cv[:, sl]) + _dot(en.astype(BF16), vn[:, sl])) / l)
    o_ref[0] = jnp.concatenate(outs, axis=-1).astype(BF16)


def _band_sample(q, ck, cv, kn, vn, bias, *, nh, d):
    B, s_new, w = q.shape
    lc = ck.shape[1]
    new = pl.BlockSpec((1, s_new, w), lambda b: (b, 0, 0))
    old = pl.BlockSpec((1, lc, w), lambda b: (b, 0, 0))
    return pl.pallas_call(
        functools.partial(_band_sample_kernel, nh=nh, d=d, lc=lc, s_new=s_new),
        grid=(B,),
        in_specs=[new, old, old, new, new, pl.BlockSpec(bias.shape, lambda b: (0, 0, 0))],
        out_specs=new,
        out_shape=jax.ShapeDtypeStruct((B, s_new, w), BF16),
        compiler_params=_params("parallel"),
        name="band_sample",
    )(q, ck, cv, kn, vn, bias)


def _cumsum_lanes_kernel(x_ref, out_ref, *, blk, suffix):
    R, T = x_ref.shape
    src = lax.broadcasted_iota(jnp.int32, (blk, blk), 0)
    dst = lax.broadcasted_iota(jnp.int32, (blk, blk), 1)
    tri = ((src > dst) if suffix else (src <= dst)).astype(F32)
    carry = jnp.zeros((R, 1), F32)
    n = T // blk
    for i in (range(n - 1, -1, -1) if suffix else range(n)):
        xb = x_ref[:, i * blk:(i + 1) * blk]
        out_ref[:, i * blk:(i + 1) * blk] = jnp.dot(xb, tri, precision=HIGHEST, preferred_element_type=F32) + carry
        carry = carry + jnp.sum(xb, axis=1, keepdims=True)


def _cumsum_lanes(x, *, suffix):
    return pl.pallas_call(
        functools.partial(_cumsum_lanes_kernel, blk=256, suffix=suffix),
        out_shape=jax.ShapeDtypeStruct(x.shape, F32),
        compiler_params=_params(),
        name="cumsum_lanes",
    )(x)


def _fox_prompt_kernel(q_ref, kt_ref, vt_ref, ft_ref, o_ref, *, tq, tk, d):
    p = pl.program_id(1)
    T = q_ref.shape[1]
    f_rows = [ft_ref[0, pl.ds(2 * p + hh, 1), :] for hh in range(2)]
    diag_mask = (lax.broadcasted_iota(jnp.int32, (tq, tk), 0) >= lax.broadcasted_iota(jnp.int32, (tq, tk), 1))
    for qi in range(T // tq):
        qs = [q_ref[0, qi * tq:(qi + 1) * tq, hh * d:(hh + 1) * d] for hh in range(2)]
        m = [jnp.full((tq, 1), -jnp.inf, F32) for _ in range(2)]
        l = [jnp.zeros((tq, 1), F32) for _ in range(2)]
        acc = [jnp.zeros((tq, d), F32) for _ in range(2)]
        for kv in range((qi + 1) * tq // tk):
            ks = slice(kv * tk, (kv + 1) * tk)
            for hh in range(2):
                kt = kt_ref[0, hh, :, ks].astype(BF16)
                vt = vt_ref[0, hh, :, ks].astype(BF16)
                s = _dot(qs[hh], kt) - f_rows[hh][:, ks]
                if (kv + 1) * tk > qi * tq:
                    assert tq == tk and kv == qi
                    s = jnp.where(diag_mask, s, -jnp.inf)
                m_new = jnp.maximum(m[hh], jnp.max(s, axis=-1, keepdims=True))
                alpha = jnp.exp(m[hh] - m_new)
                e = jnp.exp(s - m_new)
                l[hh] = alpha * l[hh] + jnp.sum(e, axis=-1, keepdims=True)
                acc[hh] = alpha * acc[hh] + _dot_nt(e.astype(BF16), vt)
                m[hh] = m_new
        o_ref[0, qi * tq:(qi + 1) * tq, :] = jnp.concatenate([acc[hh] / l[hh] for hh in range(2)],
                                                              axis=-1).astype(BF16)


def _fox_prompt(q, kt, vt, ft, *, tq, tk):
    B, T, w = q.shape
    _, nh, d, _ = kt.shape
    pair = 2 * d
    kv = pl.BlockSpec((1, 2, d, T), lambda b, p: (b, p, 0, 0))
    qo = pl.BlockSpec((1, T, pair), lambda b, p: (b, 0, p))
    return pl.pallas_call(
        functools.partial(_fox_prompt_kernel, tq=tq, tk=tk, d=d),
        grid=(B, nh // 2),
        in_specs=[qo, kv, kv, pl.BlockSpec((1, nh, T), lambda b, p: (b, 0, 0))],
        out_specs=qo,
        out_shape=jax.ShapeDtypeStruct((B, T, w), BF16),
        compiler_params=_params("parallel", "arbitrary"),
        name="fox_prompt",
    )(q, kt, vt, ft)


def _fox_sample_kernel(q_ref, kt_ref, vt_ref, ct_ref, kn_ref, vn_ref, lf_ref, o_ref, fqt_scr, *, hg, d, s_new):
    g = pl.program_id(1)
    nh = lf_ref.shape[2]
    tri = (lax.broadcasted_iota(jnp.int32, (s_new, s_new), 0)
           >= lax.broadcasted_iota(jnp.int32, (s_new, s_new), 1))
    eye = (lax.broadcasted_iota(jnp.int32, (nh, nh), 0) == lax.broadcasted_iota(jnp.int32, (nh, nh), 1)).astype(F32)
    fq = jnp.dot(tri.astype(F32), lf_ref[0], precision=HIGHEST, preferred_element_type=F32)
    fqt_scr[...] = lax.dot_general(eye, fq, NT, precision=HIGHEST, preferred_element_type=F32)
    outs = []
    for hh in range(hg):
        sl = slice(hh * d, (hh + 1) * d)
        head = g * hg + hh
        qh = q_ref[0, 0, :, sl]
        sc = _dot(qh, kt_ref[0, hh].astype(BF16)) + ct_ref[0, pl.ds(head, 1), :]
        sn = _dot_nt(qh, kn_ref[0, 0, :, sl]) - fqt_scr[pl.ds(head, 1), :]
        sn = jnp.where(tri, sn, -jnp.inf)
        m = jnp.maximum(jnp.max(sc, axis=-1, keepdims=True), jnp.max(sn, axis=-1, keepdims=True))
        ec = jnp.exp(sc - m)
        en = jnp.exp(sn - m)
        l = jnp.sum(ec, axis=-1, keepdims=True) + jnp.sum(en, axis=-1, keepdims=True)
        o = _dot_nt(ec.astype(BF16), vt_ref[0, hh].astype(BF16)) + _dot(en.astype(BF16), vn_ref[0, 0, :, sl])
        outs.append(o / l)
    o_ref[0, 0] = jnp.concatenate(outs, axis=-1).astype(BF16)


def _fox_sample(q, kt, vt, ct, kn, vn, lf, *, hg):
    B, ng, s_new, gw = q.shape
    _, nh, d, lc = kt.shape
    new = pl.BlockSpec((1, 1, s_new, gw), lambda b, g: (b, g, 0, 0))
    old = pl.BlockSpec((1, hg, d, lc), lambda b, g: (b, g, 0, 0))
    return pl.pallas_call(
        functools.partial(_fox_sample_kernel, hg=hg, d=d, s_new=s_new),
        grid=(B, ng),
        in_specs=[new, old, old, pl.BlockSpec((1, nh, lc), lambda b, g: (b, 0, 0)), new, new,
                  pl.BlockSpec((1, s_new, nh), lambda b, g: (b, 0, 0))],
        out_specs=new,
        out_shape=jax.ShapeDtypeStruct(q.shape, BF16),
        scratch_shapes=[pltpu.VMEM((nh, s_new), F32)],
        compiler_params=_params("parallel", "arbitrary"),
        name="fox_sample",
    )(q, kt, vt, ct, kn, vn, lf)


def _trunk(x, state, wts, dims, *, tm, gla_bb):
    B, T, D = x.shape
    n = B * T
    depth = wts["norm_g"].shape[0]
    h_a, dk_a, dv_a, h_b, d_b, h_c, d_c = dims
    a_qk, a_v, b_w, c_w = h_a * dk_a, h_a * dv_a, h_b * d_b, h_c * d_c
    xs = x.reshape(n, D)
    new_states = []
    pending = ()
    r3 = lambda t: t.reshape(B, T, t.shape[-1])
    for l in range(depth):
        i = l // 2
        g = wts["norm_g"][l]
        xs = _ffn(xs, g[0:1], wts["w_gu"][l][0], wts["w_down"][l][0], pending, tm=tm)
        if l % 2 == 0:
            qa, ka, va, gate, la, bq, bk, bv, bk32, bv32 = _even_in(
                xs, g[1:2], wts["even_main"][i], wts["even_r"][i], wts["alpha_up"][i], wts["b_alpha"][i],
                tm=tm, a_qk=a_qk, a_v=a_v, b_w=b_w, dk_a=dk_a, d_b=d_b)
            s0 = jnp.zeros((B, h_a, dk_a, dv_a), F32) if state is None else state[l][0]
            o_a, s_fin = _gla(r3(qa), r3(ka), r3(va), r3(gate), r3(la), s0, wts["gla_g"][i],
                              bb=gla_bb, L=min(CHUNK, T))
            if state is None:
                pad = ((0, 0), (BAND_CHUNKS * CHUNK, 0), (0, 0))
                o_b = _band_prompt(r3(bq), jnp.pad(r3(bk), pad), jnp.pad(r3(bv), pad), wts["band_bias"][i],
                                   nh=h_b, d=d_b)
                keep = min(BAND_CHUNKS * CHUNK, T)
                nk, nv = r3(bk32)[:, T - keep:], r3(bv32)[:, T - keep:]
            else:
                ck, cv = state[l][1], state[l][2]
                o_b = _band_sample(r3(bq), ck.reshape(B, ck.shape[1], b_w), cv.reshape(B, cv.shape[1], b_w),
                                   r3(bk), r3(bv), wts["band_bias"][i], nh=h_b, d=d_b)
                nk, nv = r3(bk32), r3(bv32)
            new_states.append((s_fin, nk.reshape(B, -1, h_b, d_b), nv.reshape(B, -1, h_b, d_b)))
            pending = ((o_a.reshape(n, a_v), wts["even_out_a"][i]), (o_b.reshape(n, b_w), wts["even_out_b"][i]))
        elif state is None:
            q, kt, vt, lft = _odd_in_t(xs, g[1:2], wts["odd_q"][i], wts["odd_kvt"][i], wts["odd_ft"][i],
                                       wts["b_f_col"][i], B=B, T=T, tm=tm, nh=h_c, d_c=d_c)
            ft = _cumsum_lanes(lft.reshape(B * h_c, T), suffix=False).reshape(B, h_c, T)
            o = _fox_prompt(r3(q), kt, vt, ft, tq=512, tk=512)
            new_states.append((kt.transpose(0, 3, 1, 2), vt.transpose(0, 3, 1, 2), lft.transpose(0, 2, 1)))
            pending = ((o.reshape(n, c_w), wts["odd_out"][i]),)
        else:
            q, k, v, k32, v32, lf = _odd_in(xs, g[1:2], wts["odd_main"][i], wts["odd_f"][i], wts["b_f"][i],
                                             tm=tm, c_w=c_w, d_c=d_c, h_c=h_c)
            ck, cv, clf = state[l]
            lc = ck.shape[1]
            hg = 4
            ct = _cumsum_lanes(clf.astype(F32).transpose(0, 2, 1).reshape(B * h_c, lc), suffix=True)
            grp = lambda t: t.reshape(B, T, h_c // hg, hg * d_c).transpose(0, 2, 1, 3)
            o = _fox_sample(grp(q), ck.transpose(0, 2, 3, 1), cv.transpose(0, 2, 3, 1), ct.reshape(B, h_c, lc),
                            grp(k), grp(v), r3(lf), hg=hg)
            o = o.transpose(0, 2, 1, 3).reshape(n, c_w)
            new_states.append((r3(k32).reshape(B, T, h_c, d_c), r3(v32).reshape(B, T, h_c, d_c), r3(lf)))
            pending = ((o, wts["odd_out"][i]),)
        last = l == depth - 1
        xs = _ffn(xs, g[2:3], wts["w_gu"][l][1], wts["w_down"][l][1], pending,
                  wts["final_g"] if last else None, tm=tm)
        pending = ()
    return xs.reshape(B, T, D), new_states


def kernel(x_prompt, x_sample, state_gla, cache_band_k, cache_band_v, cache_fox_k, cache_fox_v, cache_fox_logf,
           norm_g, ffn_w_gu, ffn_w_down, even_w_in, gla_w_alpha_up, gla_b_alpha, gla_norm_g, band_rel_bias,
           even_w_out, odd_w_in, fox_b_f, odd_w_out, final_norm_g):
    depth = norm_g.shape[0]
    _, _, h_a, dk_a, dv_a = state_gla.shape
    h_b, d_b = cache_band_k.shape[-2:]
    h_c, d_c = cache_fox_k.shape[-2:]
    r_a = gla_w_alpha_up.shape[1]
    a_qk, a_v, b_w, c_w = h_a * dk_a, h_a * dv_a, h_b * d_b, h_c * d_c
    assert cache_band_k.shape[2] == BAND_CHUNKS * CHUNK and band_rel_bias.shape[-1] == 2 * MAX_REL + 1

    r0 = 2 * a_qk + 2 * a_v
    n_rel = 384
    rel_padded = jnp.pad(band_rel_bias, ((0, 0), (0, 0), (0, n_rel - band_rel_bias.shape[-1])))
    win = (BAND_CHUNKS + 1) * CHUNK
    odd_t = odd_w_in.transpose(0, 2, 1)
    wts = {
        "norm_g": norm_g,
        "w_gu": ffn_w_gu.astype(BF16),
        "w_down": ffn_w_down.astype(BF16),
        "even_main": jnp.concatenate([even_w_in[:, :, :r0], even_w_in[:, :, r0 + r_a:]], axis=-1).astype(BF16),
        "even_r": even_w_in[:, :, r0:r0 + r_a].astype(BF16),
        "alpha_up": gla_w_alpha_up.astype(BF16),
        "b_alpha": gla_b_alpha[:, None, :],
        "gla_g": gla_norm_g[:, None, :],
        "band_bias": [_bias_table(rel_padded[i], chunk=CHUNK, win=win) for i in range(rel_padded.shape[0])],
        "even_out_a": even_w_out[:, :a_v].astype(BF16),
        "even_out_b": even_w_out[:, a_v:].astype(BF16),
        "odd_main": odd_w_in[:, :, :3 * c_w].astype(BF16),
        "odd_f": odd_w_in[:, :, 3 * c_w:].astype(BF16),
        "odd_q": odd_w_in[:, :, :c_w].astype(BF16),
        "odd_kvt": odd_t[:, c_w:3 * c_w].astype(BF16),
        "odd_ft": odd_t[:, 3 * c_w:].astype(BF16),
        "b_f": fox_b_f[:, None, :],
        "b_f_col": fox_b_f[:, :, None],
        "odd_out": odd_w_out.astype(BF16),
        "final_g": final_norm_g[None, :],
    }
    dims = (h_a, dk_a, dv_a, h_b, d_b, h_c, d_c)

    sample_states = []
    for l in range(depth):
        i = l // 2
        if l % 2 == 0:
            sample_states.append((state_gla[i], cache_band_k[i], cache_band_v[i]))
        else:
            sample_states.append((cache_fox_k[i], cache_fox_v[i], cache_fox_logf[i]))

    y_prompt, ns_p = _trunk(x_prompt, None, wts, dims, tm=512, gla_bb=8)
    y_sample, ns_s = _trunk(x_sample, sample_states, wts, dims, tm=256, gla_bb=8)

    ev = range(0, depth, 2)
    od = range(1, depth, 2)
    stack = lambda ns, layers, j: jnp.stack([ns[l][j] for l in layers])
    return (y_prompt, y_sample,
            stack(ns_p, ev, 0), stack(ns_p, ev, 1), stack(ns_p, ev, 2),
            stack(ns_p, od, 0), stack(ns_p, od, 1), stack(ns_p, od, 2),
            stack(ns_s, ev, 0), stack(ns_s, ev, 1), stack(ns_s, ev, 2),
            stack(ns_s, od, 0), stack(ns_s, od, 1), stack(ns_s, od, 2))
```

```python
import functools

import jax
import jax.numpy as jnp
from jax import lax
from jax.experimental import pallas as pl
from jax.experimental.pallas import tpu as pltpu

F32 = jnp.float32
BF16 = jnp.bfloat16
HIGHEST = lax.Precision.HIGHEST
NT = (((1,), (1,)), ((), ()))
TN = (((0,), (0,)), ((), ()))

EPS = 1e-6
CHUNK = 64
BAND_CHUNKS = 8
MAX_REL = 128
GATE_NORM_A = 16.0
LOG2E = 1.4426950408889634
VMEM_LIMIT_BYTES = 56 * 1024 * 1024


def _params(*sem):
    return pltpu.CompilerParams(dimension_semantics=sem, vmem_limit_bytes=VMEM_LIMIT_BYTES)


def _resident(shape):
    nd = len(shape)
    return pl.BlockSpec(shape, lambda *_: (0,) * nd, pipeline_mode=pl.Buffered(1))


def _resident_slab(stacked, idx):
    shape = stacked.shape[len(idx):]
    return pl.BlockSpec((None,) * len(idx) + shape, lambda *_: tuple(idx) + (0,) * len(shape),
                        pipeline_mode=pl.Buffered(1))


def _rows(tm, width):
    return pl.BlockSpec((tm, width), lambda i: (i, 0))


def _rms(x, g):
    ms = jnp.mean(x * x, axis=-1, keepdims=True)
    return x * lax.rsqrt(ms + EPS) * g


def _log_sigmoid(x):
    return jnp.minimum(x, 0.0) - jnp.log1p(jnp.exp(-jnp.abs(x)))


def _dot(a, b):
    return jnp.dot(a, b, preferred_element_type=F32)


def _dot_nt(a, b):
    return lax.dot_general(a, b, NT, preferred_element_type=F32)


def _ffn_kernel(*refs, n_proj, has_final, d_ff):
    it = iter(refs)
    x_ref = next(it)
    proj = [(next(it), next(it)) for _ in range(n_proj)]
    g_ref, wgu_ref, wd_ref = next(it), next(it), next(it)
    gf_ref = next(it) if has_final else None
    out_ref = next(it)

    x = x_ref[...]
    for o_ref, w_ref in proj:
        x = x + _dot(o_ref[...], w_ref[...])
    xn = _rms(x, g_ref[...]).astype(BF16)
    h = _dot(xn, wgu_ref[...])
    a = (jax.nn.silu(h[:, :d_ff]) * h[:, d_ff:]).astype(BF16)
    y = x + 0.5 * _dot(a, wd_ref[...])
    if has_final:
        y = _rms(y, gf_ref[...])
    out_ref[...] = y


def _ffn(x, g, wgu, wd, widx, proj=(), final_g=None, *, tm):
    n, d = x.shape
    d_ff = wd.shape[-2]
    args, specs = [x], [_rows(tm, d)]
    for o, w in proj:
        args += [o, w]
        specs += [_rows(tm, o.shape[1]), _resident(w.shape)]
    args += [g, wgu, wd]
    specs += [_resident(g.shape), _resident_slab(wgu, widx), _resident_slab(wd, widx)]
    if final_g is not None:
        args.append(final_g)
        specs.append(_resident(final_g.shape))
    return pl.pallas_call(
        functools.partial(_ffn_kernel, n_proj=len(proj), has_final=final_g is not None, d_ff=d_ff),
        grid=(n // tm,),
        in_specs=specs,
        out_specs=_rows(tm, d),
        out_shape=jax.ShapeDtypeStruct((n, d), F32),
        compiler_params=_params("parallel"),
        name="ffn",
    )(*args)


def _even_in_kernel(x_ref, g_ref, w_ref, wr_ref, wup_ref, ba_ref,
                    qa_ref, ka_ref, va_ref, gate_ref, la_ref, bq_ref, bk_ref, bv_ref, bk32_ref, bv32_ref,
                    *, a_qk, a_v, b_w, dk_a, d_b):
    xn = _rms(x_ref[...], g_ref[...]).astype(BF16)
    h = _dot(xn, w_ref[...])
    o = 0
    qa_ref[...] = h[:, o:o + a_qk] * (dk_a ** -0.5)
    o += a_qk
    ka_ref[...] = h[:, o:o + a_qk]
    o += a_qk
    va_ref[...] = h[:, o:o + a_v].astype(BF16)
    o += a_v
    gate_ref[...] = jax.nn.silu(h[:, o:o + a_v])
    o += a_v
    bq_ref[...] = (h[:, o:o + b_w] * (d_b ** -0.5)).astype(BF16)
    o += b_w
    bk = h[:, o:o + b_w]
    o += b_w
    bv = h[:, o:o + b_w]
    bk32_ref[...] = bk
    bv32_ref[...] = bv
    bk_ref[...] = bk.astype(BF16)
    bv_ref[...] = bv.astype(BF16)
    r = _dot(xn, wr_ref[...]).astype(BF16)
    z = _dot(r, wup_ref[...]) + ba_ref[...]
    la_ref[...] = _log_sigmoid(z) / GATE_NORM_A


def _even_in(x, g, w_main, w_r, w_up, b_alpha, *, tm, a_qk, a_v, b_w, dk_a, d_b):
    n, d = x.shape
    widths = [(a_qk, F32), (a_qk, F32), (a_v, BF16), (a_v, F32), (a_qk, F32),
              (b_w, BF16), (b_w, BF16), (b_w, BF16), (b_w, F32), (b_w, F32)]
    return pl.pallas_call(
        functools.partial(_even_in_kernel, a_qk=a_qk, a_v=a_v, b_w=b_w, dk_a=dk_a, d_b=d_b),
        grid=(n // tm,),
        in_specs=[_rows(tm, d), _resident(g.shape), _resident(w_main.shape), _resident(w_r.shape),
                  _resident(w_up.shape), _resident(b_alpha.shape)],
        out_specs=[_rows(tm, w) for w, _ in widths],
        out_shape=[jax.ShapeDtypeStruct((n, w), dt) for w, dt in widths],
        compiler_params=_params("parallel"),
        name="even_in",
    )(x, g, w_main, w_r, w_up, b_alpha)


def _odd_in_kernel(x_ref, g_ref, w_ref, wf_ref, bf_ref, q_ref, k_ref, v_ref, k32_ref, v32_ref, lf_ref, *, c_w, d_c):
    xn = _rms(x_ref[...], g_ref[...]).astype(BF16)
    h = _dot(xn, w_ref[...])
    q_ref[...] = (h[:, :c_w] * (d_c ** -0.5)).astype(BF16)
    k = h[:, c_w:2 * c_w]
    v = h[:, 2 * c_w:]
    k32_ref[...] = k
    v32_ref[...] = v
    k_ref[...] = k.astype(BF16)
    v_ref[...] = v.astype(BF16)
    lf_ref[...] = _log_sigmoid(_dot(xn, wf_ref[...]) + bf_ref[...])


def _odd_in(x, g, w_main, w_f, b_f, *, tm, c_w, d_c, h_c):
    n, d = x.shape
    outs = [(c_w, BF16), (c_w, BF16), (c_w, BF16), (c_w, F32), (c_w, F32), (h_c, F32)]
    return pl.pallas_call(
        functools.partial(_odd_in_kernel, c_w=c_w, d_c=d_c),
        grid=(n // tm,),
        in_specs=[_rows(tm, d), _resident(g.shape), _resident(w_main.shape), _resident(w_f.shape),
                  _resident(b_f.shape)],
        out_specs=[_rows(tm, w) for w, _ in outs],
        out_shape=[jax.ShapeDtypeStruct((n, w), dt) for w, dt in outs],
        compiler_params=_params("parallel"),
        name="odd_in",
    )(x, g, w_main, w_f, b_f)


def _odd_in_t_kernel(x_ref, g_ref, wq_ref, wkvt_ref, wft_ref, bf_ref, q_ref, kt_ref, vt_ref, lft_ref, *, nh, d_c):
    xn = _rms(x_ref[...], g_ref[...]).astype(BF16)
    tm = xn.shape[0]
    c_w = nh * d_c
    q_ref[...] = (_dot(xn, wq_ref[...]) * (LOG2E * d_c ** -0.5)).astype(BF16)
    ht = _dot_nt(wkvt_ref[...], xn)
    kt_ref[0] = ht[:c_w].reshape(nh, d_c, tm)
    vt_ref[0] = ht[c_w:].reshape(nh, d_c, tm)
    lft_ref[0] = _log_sigmoid(_dot_nt(wft_ref[...], xn) + bf_ref[...])


def _odd_in_t(x, g, wq, wkvt, wft, b_f_col, *, B, T, tm, nh, d_c):
    n, d = x.shape
    nb = T // tm
    c_w = nh * d_c
    tok = lambda w: pl.BlockSpec((tm, w), lambda b, i: (b * nb + i, 0))
    kv = pl.BlockSpec((1, nh, d_c, tm), lambda b, i: (b, 0, 0, i))
    return pl.pallas_call(
        functools.partial(_odd_in_t_kernel, nh=nh, d_c=d_c),
        grid=(B, nb),
        in_specs=[tok(d), _resident(g.shape), _resident(wq.shape), _resident(wkvt.shape), _resident(wft.shape),
                  _resident(b_f_col.shape)],
        out_specs=[tok(c_w), kv, kv, pl.BlockSpec((1, nh, tm), lambda b, i: (b, 0, i))],
        out_shape=[jax.ShapeDtypeStruct((n, c_w), BF16), jax.ShapeDtypeStruct((B, nh, d_c, T), F32),
                   jax.ShapeDtypeStruct((B, nh, d_c, T), F32), jax.ShapeDtypeStruct((B, nh, T), F32)],
        compiler_params=_params("parallel", "parallel"),
        name="odd_in_t",
    )(x, g, wq, wkvt, wft, b_f_col)


def _gla_kernel(q_ref, k_ref, v_ref, gate_ref, la_ref, s0_ref, g_ref, o_ref, sfin_ref, s_scr, *, bb, L, nh, dk, dv):
    c = pl.program_id(1)

    @pl.when(c == 0)
    def _():
        s_scr[...] = s0_ref[...]

    row = lax.broadcasted_iota(jnp.int32, (L, L), 0)
    col = lax.broadcasted_iota(jnp.int32, (L, L), 1)
    causal = row >= col
    tri = causal.astype(F32)
    eye = (lax.broadcasted_iota(jnp.int32, (dk, dk), 0) == lax.broadcasted_iota(jnp.int32, (dk, dk), 1))
    g = g_ref[...]
    bs = range(bb)
    units = [(b, h) for b in bs for h in range(nh)]
    sk = [slice(h * dk, (h + 1) * dk) for h in range(nh)]
    sv = [slice(h * dv, (h + 1) * dv) for h in range(nh)]
    w = nh * dk
    bcum_all = jnp.dot(tri, jnp.concatenate([la_ref[b] for b in bs], axis=1), precision=HIGHEST,
                       preferred_element_type=F32)
    bcum = [bcum_all[:, b * w:(b + 1) * w] for b in bs]
    b_last = [x[L - 1:L, :] for x in bcum]
    q_in = [(q_ref[b] * jnp.exp(bcum[b])).astype(BF16) for b in bs]
    k_in = [(k_ref[b] * jnp.exp(-bcum[b])).astype(BF16) for b in bs]
    k_st = [(k_ref[b] * jnp.exp(b_last[b] - bcum[b])).astype(BF16) for b in bs]
    a_tot = [jnp.exp(x) for x in b_last]
    v = [v_ref[b] for b in bs]
    att = [jnp.where(causal, _dot_nt(q_in[b][:, sk[h]], k_in[b][:, sk[h]]), 0.0).astype(BF16) for b, h in units]
    s_old = [s_scr[b, h] for b, h in units]
    o = [_dot(att[u], v[b][:, sv[h]]) + _dot(q_in[b][:, sk[h]], s_old[u].astype(BF16))
         for u, (b, h) in enumerate(units)]
    for u, (b, h) in enumerate(units):
        a_col = jnp.sum(jnp.where(eye, jnp.broadcast_to(a_tot[b][:, sk[h]], (dk, dk)), 0.0), axis=1, keepdims=True)
        s_scr[b, h] = a_col * s_old[u] + lax.dot_general(k_st[b][:, sk[h]], v[b][:, sv[h]], TN,
                                                          preferred_element_type=F32)
    o = [x * lax.rsqrt(jnp.mean(x * x, axis=-1, keepdims=True) + EPS) * g for x in o]
    for b in bs:
        o_ref[b] = (jnp.concatenate(o[b * nh:(b + 1) * nh], axis=-1) * gate_ref[b]).astype(BF16)

    @pl.when(c == pl.num_programs(1) - 1)
    def _():
        sfin_ref[...] = s_scr[...]


def _gla(q, k, v, gate, la, s0, g, *, bb, L):
    B, T, a_qk = q.shape
    a_v = v.shape[-1]
    _, nh, dk, dv = s0.shape
    n = T // L

    def tok(w):
        return pl.BlockSpec((bb, L, w), lambda i, c: (i, c, 0))

    st = pl.BlockSpec((bb, nh, dk, dv), lambda i, c: (i, 0, 0, 0))
    return pl.pallas_call(
        functools.partial(_gla_kernel, bb=bb, L=L, nh=nh, dk=dk, dv=dv),
        grid=(B // bb, n),
        in_specs=[tok(a_qk), tok(a_qk), tok(a_v), tok(a_v), tok(a_qk), st, pl.BlockSpec(g.shape, lambda i, c: (0, 0))],
        out_specs=[tok(a_v), st],
        out_shape=[jax.ShapeDtypeStruct((B, T, a_v), BF16), jax.ShapeDtypeStruct(s0.shape, F32)],
        scratch_shapes=[pltpu.VMEM((bb, nh, dk, dv), F32)],
        compiler_params=_params("parallel", "arbitrary"),
        name="gla",
    )(q, k, v, gate, la, s0, g)


def _bias_table_kernel(rel_ref, out_ref, *, chunk, win, rows, n_rel):
    width = out_ref.shape[2]
    rel = rel_ref[...]
    m_idx = lax.broadcasted_iota(jnp.int32, (n_rel, width), 0)
    j = lax.broadcasted_iota(jnp.int32, (n_rel, width), 1)
    jr = lax.broadcasted_iota(jnp.int32, (rel.shape[0], width), 1)
    for i in range(rows):
        want = jnp.clip((win - chunk) + i - j, -MAX_REL, MAX_REL) + MAX_REL
        onehot = (m_idx == want).astype(F32)
        bias = jnp.dot(rel, onehot, precision=HIGHEST, preferred_element_type=F32)
        first = (i // chunk) * chunk
        out_ref[:, i, :] = jnp.where((jr >= first) & (jr < first + win), bias, -jnp.inf)


def _bias_table(rel_padded, *, chunk, win, rows):
    nh, n_rel = rel_padded.shape
    return pl.pallas_call(
        functools.partial(_bias_table_kernel, chunk=chunk, win=win, rows=rows, n_rel=n_rel),
        out_shape=jax.ShapeDtypeStruct((nh, rows, win + rows - chunk), F32),
        compiler_params=_params(),
        name="band_bias_table",
    )(rel_padded)


def _band_prompt_kernel(q_ref, k_ref, v_ref, bias_ref, o_ref, *, nh, d, pad):
    tq, width = bias_ref.shape[1], bias_ref.shape[2]
    c = pl.program_id(1)
    start = pl.multiple_of(c * tq, tq)
    in_stream = lax.broadcasted_iota(jnp.int32, (tq, width), 1) >= pad - c * tq
    q = q_ref[0]
    kw = k_ref[0, pl.ds(start, width), :]
    vw = v_ref[0, pl.ds(start, width), :]
    heads = [slice(h * d, (h + 1) * d) for h in range(nh)]
    s = [jnp.where(in_stream, _dot_nt(q[:, sl], kw[:, sl]) + bias_ref[h], -jnp.inf) for h, sl in enumerate(heads)]
    m = [jnp.max(x, axis=-1, keepdims=True) for x in s]
    e = [jnp.exp(x - mx) for x, mx in zip(s, m)]
    l = [jnp.sum(x, axis=-1, keepdims=True) for x in e]
    o = [_dot(x.astype(BF16), vw[:, sl]) / lx for x, lx, sl in zip(e, l, heads)]
    o_ref[0] = jnp.concatenate(o, axis=-1).astype(BF16)


def _band_prompt(q, kpad, vpad, bias, *, nh, d):
    B, T, w = q.shape
    tp = kpad.shape[1]
    tq = bias.shape[1]
    kv = pl.BlockSpec((1, tp, w), lambda b, c: (b, 0, 0))
    return pl.pallas_call(
        functools.partial(_band_prompt_kernel, nh=nh, d=d, pad=tp - T),
        grid=(B, T // tq),
        in_specs=[pl.BlockSpec((1, tq, w), lambda b, c: (b, c, 0)), kv, kv,
                  pl.BlockSpec(bias.shape, lambda b, c: (0, 0, 0))],
        out_specs=pl.BlockSpec((1, tq, w), lambda b, c: (b, c, 0)),
        out_shape=jax.ShapeDtypeStruct((B, T, w), BF16),
        compiler_params=_params("parallel", "arbitrary"),
        name="band_prompt",
    )(q, kpad, vpad, bias)


def _band_sample_kernel(q_ref, ck_ref, cv_ref, kn_ref, vn_ref, bias_ref, o_ref, *, nh, d, lc, s_new):
    q = q_ref[0]
    ck = ck_ref[0].astype(BF16)
    cv = cv_ref[0].astype(BF16)
    kn = kn_ref[0]
    vn = vn_ref[0]
    outs = []
    for h in range(nh):
        sl = slice(h * d, (h + 1) * d)
        bias = bias_ref[h]
        sc = _dot_nt(q[:, sl], ck[:, sl]) + bias[:s_new, :lc]
        sn = _dot_nt(q[:, sl], kn[:, sl]) + bias[:s_new, lc:lc + s_new]
        m = jnp.maximum(jnp.max(sc, axis=-1, keepdims=True), jnp.max(sn, axis=-1, keepdims=True))
        ec = jnp.exp(sc - m)
        en = jnp.exp(sn - m)
        l = jnp.sum(ec, axis=-1, keepdims=True) + jnp.sum(en, axis=-1, keepdims=True)
        outs.append((_dot(ec.astype(BF16), cv[:, sl]) + _dot(en.astype(BF16), vn[:, sl])) / l)
    o_ref[0] = jnp.concatenate(outs, axis=-1).astype(BF16)


def _band_sample(q, ck, cv, kn, vn, bias, *, nh, d):
    B, s_new, w = q.shape
    lc = ck.shape[1]
    new = pl.BlockSpec((1, s_new, w), lambda b: (b, 0, 0))
    old = pl.BlockSpec((1, lc, w), lambda b: (b, 0, 0))
    return pl.pallas_call(
        functools.partial(_band_sample_kernel, nh=nh, d=d, lc=lc, s_new=s_new),
        grid=(B,),
        in_specs=[new, old, old, new, new, pl.BlockSpec(bias.shape, lambda b: (0, 0, 0))],
        out_specs=new,
        out_shape=jax.ShapeDtypeStruct((B, s_new, w), BF16),
        compiler_params=_params("parallel"),
        name="band_sample",
    )(q, ck, cv, kn, vn, bias)


def _cumsum_lanes_kernel(x_ref, out_ref, *, blk, suffix):
    R, T = x_ref.shape
    src = lax.broadcasted_iota(jnp.int32, (blk, blk), 0)
    dst = lax.broadcasted_iota(jnp.int32, (blk, blk), 1)
    tri = ((src > dst) if suffix else (src <= dst)).astype(F32)
    carry = jnp.zeros((R, 1), F32)
    n = T // blk
    for i in (range(n - 1, -1, -1) if suffix else range(n)):
        xb = x_ref[:, i * blk:(i + 1) * blk]
        out_ref[:, i * blk:(i + 1) * blk] = jnp.dot(xb, tri, precision=HIGHEST, preferred_element_type=F32) + carry
        carry = carry + jnp.sum(xb, axis=1, keepdims=True)


def _cumsum_lanes(x, *, suffix):
    return pl.pallas_call(
        functools.partial(_cumsum_lanes_kernel, blk=256, suffix=suffix),
        out_shape=jax.ShapeDtypeStruct(x.shape, F32),
        compiler_params=_params(),
        name="cumsum_lanes",
    )(x)


def _fox_prompt_kernel(q_ref, kt_ref, vt_ref, ft_ref, o_ref, *, tq, tk, d):
    p = pl.program_id(1)
    T = q_ref.shape[1]
    f_rows = [ft_ref[0, pl.ds(2 * p + hh, 1), :] * LOG2E for hh in range(2)]
    assert tq == tk
    nq = T // tq
    diag_mask = (lax.broadcasted_iota(jnp.int32, (tq, tq), 0) >= lax.broadcasted_iota(jnp.int32, (tq, tq), 1))
    units = [(qi, hh) for qi in range(nq) for hh in range(2)]
    qs = {(qi, hh): q_ref[0, qi * tq:(qi + 1) * tq, hh * d:(hh + 1) * d] for qi, hh in units}
    m = {u: jnp.full((tq, 1), -jnp.inf, F32) for u in units}
    l = {u: jnp.zeros((tq, 1), F32) for u in units}
    acc = {u: jnp.zeros((tq, d), F32) for u in units}
    for kv in range(nq):
        ks = slice(kv * tk, (kv + 1) * tk)
        kt = [kt_ref[0, hh, :, ks].astype(BF16) for hh in range(2)]
        vt = [vt_ref[0, hh, :, ks].astype(BF16) for hh in range(2)]
        live = [u for u in units if u[0] >= kv]
        s = {}
        for u in live:
            x = _dot(qs[u], kt[u[1]]) - f_rows[u[1]][:, ks]
            s[u] = jnp.where(diag_mask, x, -jnp.inf) if u[0] == kv else x
        m_new = {u: jnp.maximum(m[u], jnp.max(s[u], axis=-1, keepdims=True)) for u in live}
        e = {u: jnp.exp2(s[u] - m_new[u]) for u in live}
        for u in live:
            alpha = jnp.exp2(m[u] - m_new[u])
            l[u] = alpha * l[u] + jnp.sum(e[u], axis=-1, keepdims=True)
            acc[u] = alpha * acc[u] + _dot_nt(e[u].astype(BF16), vt[u[1]])
            m[u] = m_new[u]
        o_ref[0, kv * tq:(kv + 1) * tq, :] = jnp.concatenate([acc[kv, hh] / l[kv, hh] for hh in range(2)],
                                                              axis=-1).astype(BF16)


def _fox_prompt(q, kt, vt, ft, *, tq, tk):
    B, T, w = q.shape
    _, nh, d, _ = kt.shape
    pair = 2 * d
    kv = pl.BlockSpec((1, 2, d, T), lambda b, p: (b, p, 0, 0))
    qo = pl.BlockSpec((1, T, pair), lambda b, p: (b, 0, p))
    return pl.pallas_call(
        functools.partial(_fox_prompt_kernel, tq=tq, tk=tk, d=d),
        grid=(B, nh // 2),
        in_specs=[qo, kv, kv, pl.BlockSpec((1, nh, T), lambda b, p: (b, 0, 0))],
        out_specs=qo,
        out_shape=jax.ShapeDtypeStruct((B, T, w), BF16),
        compiler_params=_params("parallel", "arbitrary"),
        name="fox_prompt",
    )(q, kt, vt, ft)


def _fox_sample_kernel(q_ref, kt_ref, vt_ref, ct_ref, kn_ref, vn_ref, lf_ref, o_ref, fqt_scr, *, hg, d, s_new):
    g = pl.program_id(1)
    nh = lf_ref.shape[2]
    tri = (lax.broadcasted_iota(jnp.int32, (s_new, s_new), 0)
           >= lax.broadcasted_iota(jnp.int32, (s_new, s_new), 1))
    eye = (lax.broadcasted_iota(jnp.int32, (nh, nh), 0) == lax.broadcasted_iota(jnp.int32, (nh, nh), 1)).astype(F32)
    fq = jnp.dot(tri.astype(F32), lf_ref[0], precision=HIGHEST, preferred_element_type=F32)
    fqt_scr[...] = lax.dot_general(eye, fq, NT, precision=HIGHEST, preferred_element_type=F32)
    outs = []
    for hh in range(hg):
        sl = slice(hh * d, (hh + 1) * d)
        head = g * hg + hh
        qh = q_ref[0, 0, :, sl]
        sc = _dot(qh, kt_ref[0, hh].astype(BF16)) + ct_ref[0, pl.ds(head, 1), :]
        sn = _dot_nt(qh, kn_ref[0, 0, :, sl]) - fqt_scr[pl.ds(head, 1), :]
        sn = jnp.where(tri, sn, -jnp.inf)
        m = jnp.maximum(jnp.max(sc, axis=-1, keepdims=True), jnp.max(sn, axis=-1, keepdims=True))
        ec = jnp.exp(sc - m)
        en = jnp.exp(sn - m)
        l = jnp.sum(ec, axis=-1, keepdims=True) + jnp.sum(en, axis=-1, keepdims=True)
        o = _dot_nt(ec.astype(BF16), vt_ref[0, hh].astype(BF16)) + _dot(en.astype(BF16), vn_ref[0, 0, :, sl])
        outs.append(o / l)
    o_ref[0, 0] = jnp.concatenate(outs, axis=-1).astype(BF16)


def _fox_sample(q, kt, vt, ct, kn, vn, lf, *, hg):
    B, ng, s_new, gw = q.shape
    _, nh, d, lc = kt.shape
    new = pl.BlockSpec((1, 1, s_new, gw), lambda b, g: (b, g, 0, 0))
    old = pl.BlockSpec((1, hg, d, lc), lambda b, g: (b, g, 0, 0))
    return pl.pallas_call(
        functools.partial(_fox_sample_kernel, hg=hg, d=d, s_new=s_new),
        grid=(B, ng),
        in_specs=[new, old, old, pl.BlockSpec((1, nh, lc), lambda b, g: (b, 0, 0)), new, new,
                  pl.BlockSpec((1, s_new, nh), lambda b, g: (b, 0, 0))],
        out_specs=new,
        out_shape=jax.ShapeDtypeStruct(q.shape, BF16),
        scratch_shapes=[pltpu.VMEM((nh, s_new), F32)],
        compiler_params=_params("parallel", "arbitrary"),
        name="fox_sample",
    )(q, kt, vt, ct, kn, vn, lf)


def _trunk(x, state, wts, dims, *, tm, gla_bb):
    B, T, D = x.shape
    n = B * T
    depth = wts["norm_g"].shape[0]
    h_a, dk_a, dv_a, h_b, d_b, h_c, d_c = dims
    a_qk, a_v, b_w, c_w = h_a * dk_a, h_a * dv_a, h_b * d_b, h_c * d_c
    xs = x.reshape(n, D)
    new_states = []
    pending = ()
    r3 = lambda t: t.reshape(B, T, t.shape[-1])
    for l in range(depth):
        i = l // 2
        g = wts["norm_g"][l]
        xs = _ffn(xs, g[0:1], wts["w_gu"], wts["w_down"], (l, 0), pending, tm=tm)
        if l % 2 == 0:
            qa, ka, va, gate, la, bq, bk, bv, bk32, bv32 = _even_in(
                xs, g[1:2], wts["even_main"][i], wts["even_r"][i], wts["alpha_up"][i], wts["b_alpha"][i],
                tm=tm, a_qk=a_qk, a_v=a_v, b_w=b_w, dk_a=dk_a, d_b=d_b)
            s0 = jnp.zeros((B, h_a, dk_a, dv_a), F32) if state is None else state[l][0]
            o_a, s_fin = _gla(r3(qa), r3(ka), r3(va), r3(gate), r3(la), s0, wts["gla_g"][i],
                              bb=gla_bb, L=min(CHUNK, T))
            if state is None:
                pad = ((0, 0), (BAND_CHUNKS * CHUNK, 0), (0, 0))
                o_b = _band_prompt(r3(bq), jnp.pad(r3(bk), pad), jnp.pad(r3(bv), pad), wts["band_bias"][i],
                                   nh=h_b, d=d_b)
                keep = min(BAND_CHUNKS * CHUNK, T)
                nk, nv = r3(bk32)[:, T - keep:], r3(bv32)[:, T - keep:]
            else:
                ck, cv = state[l][1], state[l][2]
                o_b = _band_sample(r3(bq), ck.reshape(B, ck.shape[1], b_w), cv.reshape(B, cv.shape[1], b_w),
                                   r3(bk), r3(bv), wts["band_bias"][i], nh=h_b, d=d_b)
                nk, nv = r3(bk32), r3(bv32)
            new_states.append((s_fin, nk.reshape(B, -1, h_b, d_b), nv.reshape(B, -1, h_b, d_b)))
            pending = ((o_a.reshape(n, a_v), wts["even_out_a"][i]), (o_b.reshape(n, b_w), wts["even_out_b"][i]))
        elif state is None:
            q, kt, vt, lft = _odd_in_t(xs, g[1:2], wts["odd_q"][i], wts["odd_kvt"][i], wts["odd_ft"][i],
                                       wts["b_f_col"][i], B=B, T=T, tm=tm, nh=h_c, d_c=d_c)
            ft = _cumsum_lanes(lft.reshape(B * h_c, T), suffix=False).reshape(B, h_c, T)
            o = _fox_prompt(r3(q), kt, vt, ft, tq=512, tk=512)
            new_states.append((kt.transpose(0, 3, 1, 2), vt.transpose(0, 3, 1, 2), lft.transpose(0, 2, 1)))
            pending = ((o.reshape(n, c_w), wts["odd_out"][i]),)
        else:
            q, k, v, k32, v32, lf = _odd_in(xs, g[1:2], wts["odd_main"][i], wts["odd_f"][i], wts["b_f"][i],
                                             tm=tm, c_w=c_w, d_c=d_c, h_c=h_c)
            ck, cv, clf = state[l]
            lc = ck.shape[1]
            hg = 4
            ct = _cumsum_lanes(clf.astype(F32).transpose(0, 2, 1).reshape(B * h_c, lc), suffix=True)
            grp = lambda t: t.reshape(B, T, h_c // hg, hg * d_c).transpose(0, 2, 1, 3)
            o = _fox_sample(grp(q), ck.transpose(0, 2, 3, 1), cv.transpose(0, 2, 3, 1), ct.reshape(B, h_c, lc),
                            grp(k), grp(v), r3(lf), hg=hg)
            o = o.transpose(0, 2, 1, 3).reshape(n, c_w)
            new_states.append((r3(k32).reshape(B, T, h_c, d_c), r3(v32).reshape(B, T, h_c, d_c), r3(lf)))
            pending = ((o, wts["odd_out"][i]),)
        last = l == depth - 1
        xs = _ffn(xs, g[2:3], wts["w_gu"], wts["w_down"], (l, 1), pending,
                  wts["final_g"] if last else None, tm=tm)
        pending = ()
    return xs.reshape(B, T, D), new_states


def kernel(x_prompt, x_sample, state_gla, cache_band_k, cache_band_v, cache_fox_k, cache_fox_v, cache_fox_logf,
           norm_g, ffn_w_gu, ffn_w_down, even_w_in, gla_w_alpha_up, gla_b_alpha, gla_norm_g, band_rel_bias,
           even_w_out, odd_w_in, fox_b_f, odd_w_out, final_norm_g):
    depth = norm_g.shape[0]
    _, _, h_a, dk_a, dv_a = state_gla.shape
    h_b, d_b = cache_band_k.shape[-2:]
    h_c, d_c = cache_fox_k.shape[-2:]
    r_a = gla_w_alpha_up.shape[1]
    a_qk, a_v, b_w, c_w = h_a * dk_a, h_a * dv_a, h_b * d_b, h_c * d_c
    assert cache_band_k.shape[2] == BAND_CHUNKS * CHUNK and band_rel_bias.shape[-1] == 2 * MAX_REL + 1

    r0 = 2 * a_qk + 2 * a_v
    n_rel = 384
    rel_padded = jnp.pad(band_rel_bias, ((0, 0), (0, 0), (0, n_rel - band_rel_bias.shape[-1])))
    win = (BAND_CHUNKS + 1) * CHUNK
    odd_t = odd_w_in.transpose(0, 2, 1)
    wts = {
        "norm_g": norm_g,
        "w_gu": ffn_w_gu.astype(BF16),
        "w_down": ffn_w_down.astype(BF16),
        "even_main": jnp.concatenate([even_w_in[:, :, :r0], even_w_in[:, :, r0 + r_a:]], axis=-1).astype(BF16),
        "even_r": even_w_in[:, :, r0:r0 + r_a].astype(BF16),
        "alpha_up": gla_w_alpha_up.astype(BF16),
        "b_alpha": gla_b_alpha[:, None, :],
        "gla_g": gla_norm_g[:, None, :],
        "band_bias": [_bias_table(rel_padded[i], chunk=CHUNK, win=win, rows=2 * CHUNK)
                      for i in range(rel_padded.shape[0])],
        "even_out_a": even_w_out[:, :a_v].astype(BF16),
        "even_out_b": even_w_out[:, a_v:].astype(BF16),
        "odd_main": odd_w_in[:, :, :3 * c_w].astype(BF16),
        "odd_f": odd_w_in[:, :, 3 * c_w:].astype(BF16),
        "odd_q": odd_w_in[:, :, :c_w].astype(BF16),
        "odd_kvt": odd_t[:, c_w:3 * c_w].astype(BF16),
        "odd_ft": odd_t[:, 3 * c_w:].astype(BF16),
        "b_f": fox_b_f[:, None, :],
        "b_f_col": fox_b_f[:, :, None],
        "odd_out": odd_w_out.astype(BF16),
        "final_g": final_norm_g[None, :],
    }
    dims = (h_a, dk_a, dv_a, h_b, d_b, h_c, d_c)

    sample_states = []
    for l in range(depth):
        i = l // 2
        if l % 2 == 0:
            sample_states.append((state_gla[i], cache_band_k[i], cache_band_v[i]))
        else:
            sample_states.append((cache_fox_k[i], cache_fox_v[i], cache_fox_logf[i]))

    y_prompt, ns_p = _trunk(x_prompt, None, wts, dims, tm=512, gla_bb=8)
    y_sample, ns_s = _trunk(x_sample, sample_states, wts, dims, tm=256, gla_bb=8)

    ev = range(0, depth, 2)
    od = range(1, depth, 2)
    stack = lambda ns, layers, j: jnp.stack([ns[l][j] for l in layers])
    return (y_prompt, y_sample,
            stack(ns_p, ev, 0), stack(ns_p, ev, 1), stack(ns_p, ev, 2),
            stack(ns_p, od, 0), stack(ns_p, od, 1), stack(ns_p, od, 2),
            stack(ns_s, ev, 0), stack(ns_s, ev, 1), stack(ns_s, ev, 2),
            stack(ns_s, od, 0), stack(ns_s, od, 1), stack(ns_s, od, 2))
```

```python
import functools

import jax
import jax.numpy as jnp
from jax import lax
from jax.experimental import pallas as pl
from jax.experimental.pallas import tpu as pltpu

F32 = jnp.float32
BF16 = jnp.bfloat16
HIGHEST = lax.Precision.HIGHEST
NT = (((1,), (1,)), ((), ()))
TN = (((0,), (0,)), ((), ()))

EPS = 1e-6
CHUNK = 64
BAND_CHUNKS = 8
MAX_REL = 128
GATE_NORM_A = 16.0
LOG2E = 1.4426950408889634
VMEM_LIMIT_BYTES = 56 * 1024 * 1024


def _params(*sem):
    return pltpu.CompilerParams(dimension_semantics=sem, vmem_limit_bytes=VMEM_LIMIT_BYTES)


def _resident(shape):
    nd = len(shape)
    return pl.BlockSpec(shape, lambda *_: (0,) * nd, pipeline_mode=pl.Buffered(1))


def _resident_slab(stacked, idx):
    shape = stacked.shape[len(idx):]
    return pl.BlockSpec((None,) * len(idx) + shape, lambda *_: tuple(idx) + (0,) * len(shape),
                        pipeline_mode=pl.Buffered(1))


def _rows(tm, width):
    return pl.BlockSpec((tm, width), lambda i: (i, 0))


def _rms(x, g):
    ms = jnp.mean(x * x, axis=-1, keepdims=True)
    return x * lax.rsqrt(ms + EPS) * g


def _log_sigmoid(x):
    return jnp.minimum(x, 0.0) - jnp.log1p(jnp.exp(-jnp.abs(x)))


def _dot(a, b):
    return jnp.dot(a, b, preferred_element_type=F32)


def _dot_nt(a, b):
    return lax.dot_general(a, b, NT, preferred_element_type=F32)


def _ffn_kernel(*refs, n_proj, has_final, d_ff):
    it = iter(refs)
    x_ref = next(it)
    proj = [(next(it), next(it)) for _ in range(n_proj)]
    g_ref, wgu_ref, wd_ref = next(it), next(it), next(it)
    gf_ref = next(it) if has_final else None
    out_ref = next(it)

    x = x_ref[...]
    for o_ref, w_ref in proj:
        x = x + _dot(o_ref[...], w_ref[...])
    xn = _rms(x, g_ref[...]).astype(BF16)
    h = _dot(xn, wgu_ref[...])
    a = (jax.nn.silu(h[:, :d_ff]) * h[:, d_ff:]).astype(BF16)
    y = x + 0.5 * _dot(a, wd_ref[...])
    if has_final:
        y = _rms(y, gf_ref[...])
    out_ref[...] = y


def _ffn(x, g, wgu, wd, widx, proj=(), final_g=None, *, tm):
    n, d = x.shape
    d_ff = wd.shape[-2]
    args, specs = [x], [_rows(tm, d)]
    for o, w in proj:
        args += [o, w]
        specs += [_rows(tm, o.shape[1]), _resident(w.shape)]
    args += [g, wgu, wd]
    specs += [_resident(g.shape), _resident_slab(wgu, widx), _resident_slab(wd, widx)]
    if final_g is not None:
        args.append(final_g)
        specs.append(_resident(final_g.shape))
    return pl.pallas_call(
        functools.partial(_ffn_kernel, n_proj=len(proj), has_final=final_g is not None, d_ff=d_ff),
        grid=(n // tm,),
        in_specs=specs,
        out_specs=_rows(tm, d),
        out_shape=jax.ShapeDtypeStruct((n, d), F32),
        compiler_params=_params("parallel"),
        name="ffn",
    )(*args)


def _even_in_kernel(x_ref, g_ref, w_ref, wr_ref, wup_ref, ba_ref, *rest, a_qk, a_v, b_w, dk_a, d_b, tail):
    if tail:
        wkvt_ref, *rest = rest
    qa_ref, ka_ref, va_ref, gate_ref, la_ref, bq_ref, bk_ref, bv_ref, x1_ref, x2_ref = rest
    xn = _rms(x_ref[...], g_ref[...]).astype(BF16)
    h = _dot(xn, w_ref[...])
    o = 0
    qa_ref[...] = h[:, o:o + a_qk] * (dk_a ** -0.5)
    o += a_qk
    ka_ref[...] = h[:, o:o + a_qk]
    o += a_qk
    va_ref[...] = h[:, o:o + a_v].astype(BF16)
    o += a_v
    gate_ref[...] = jax.nn.silu(h[:, o:o + a_v])
    o += a_v
    bq_ref[...] = (h[:, o:o + b_w] * (d_b ** -0.5)).astype(BF16)
    o += b_w
    bk = h[:, o:o + b_w]
    o += b_w
    bv = h[:, o:o + b_w]
    bk_ref[...] = bk.astype(BF16)
    bv_ref[...] = bv.astype(BF16)
    if tail:
        @pl.when(pl.program_id(1) == pl.num_programs(1) - 1)
        def _():
            kvt = _dot_nt(wkvt_ref[...], xn[xn.shape[0] - tail:])
            x1_ref[0] = kvt[:b_w]
            x2_ref[0] = kvt[b_w:]
    else:
        x1_ref[...] = bk
        x2_ref[...] = bv
    r = _dot(xn, wr_ref[...]).astype(BF16)
    z = _dot(r, wup_ref[...]) + ba_ref[...]
    la_ref[...] = _log_sigmoid(z) / GATE_NORM_A


def _even_in(x, g, w_main, w_r, w_up, b_alpha, w_kvt=None, *, B, T, tm, a_qk, a_v, b_w, dk_a, d_b, tail):
    n, d = x.shape
    nb = T // tm
    tok = lambda w: pl.BlockSpec((tm, w), lambda b, i: (b * nb + i, 0))
    widths = [(a_qk, F32), (a_qk, F32), (a_v, BF16), (a_v, F32), (a_qk, F32), (b_w, BF16), (b_w, BF16), (b_w, BF16)]
    args = [x, g, w_main, w_r, w_up, b_alpha]
    if tail:
        assert tail <= tm
        args.append(w_kvt)
        extra_specs = [pl.BlockSpec((1, b_w, tail), lambda b, i: (b, 0, 0))] * 2
        extra_shapes = [jax.ShapeDtypeStruct((B, b_w, tail), F32)] * 2
    else:
        extra_specs = [tok(b_w)] * 2
        extra_shapes = [jax.ShapeDtypeStruct((n, b_w), F32)] * 2
    return pl.pallas_call(
        functools.partial(_even_in_kernel, a_qk=a_qk, a_v=a_v, b_w=b_w, dk_a=dk_a, d_b=d_b, tail=tail),
        grid=(B, nb),
        in_specs=[tok(d)] + [_resident(a.shape) for a in args[1:]],
        out_specs=[tok(w) for w, _ in widths] + extra_specs,
        out_shape=[jax.ShapeDtypeStruct((n, w), dt) for w, dt in widths] + extra_shapes,
        compiler_params=_params("parallel", "arbitrary"),
        name="even_in",
    )(*args)


def _odd_in_kernel(x_ref, g_ref, w_ref, wf_ref, bf_ref, q_ref, k_ref, v_ref, k32_ref, v32_ref, lf_ref, *, c_w, d_c):
    xn = _rms(x_ref[...], g_ref[...]).astype(BF16)
    h = _dot(xn, w_ref[...])
    q_ref[...] = (h[:, :c_w] * (d_c ** -0.5)).astype(BF16)
    k = h[:, c_w:2 * c_w]
    v = h[:, 2 * c_w:]
    k32_ref[...] = k
    v32_ref[...] = v
    k_ref[...] = k.astype(BF16)
    v_ref[...] = v.astype(BF16)
    lf_ref[...] = _log_sigmoid(_dot(xn, wf_ref[...]) + bf_ref[...])


def _odd_in(x, g, w_main, w_f, b_f, *, tm, c_w, d_c, h_c):
    n, d = x.shape
    outs = [(c_w, BF16), (c_w, BF16), (c_w, BF16), (c_w, F32), (c_w, F32), (h_c, F32)]
    return pl.pallas_call(
        functools.partial(_odd_in_kernel, c_w=c_w, d_c=d_c),
        grid=(n // tm,),
        in_specs=[_rows(tm, d), _resident(g.shape), _resident(w_main.shape), _resident(w_f.shape),
                  _resident(b_f.shape)],
        out_specs=[_rows(tm, w) for w, _ in outs],
        out_shape=[jax.ShapeDtypeStruct((n, w), dt) for w, dt in outs],
        compiler_params=_params("parallel"),
        name="odd_in",
    )(x, g, w_main, w_f, b_f)


def _odd_in_t_kernel(x_ref, g_ref, wq_ref, wkvt_ref, wft_ref, bf_ref, q_ref, kt_ref, vt_ref, lft_ref, *, nh, d_c):
    xn = _rms(x_ref[...], g_ref[...]).astype(BF16)
    tm = xn.shape[0]
    c_w = nh * d_c
    q_ref[...] = (_dot(xn, wq_ref[...]) * (LOG2E * d_c ** -0.5)).astype(BF16)
    ht = _dot_nt(wkvt_ref[...], xn)
    kt_ref[0] = ht[:c_w].reshape(nh, d_c, tm)
    vt_ref[0] = ht[c_w:].reshape(nh, d_c, tm)
    lft_ref[0] = _log_sigmoid(_dot_nt(wft_ref[...], xn) + bf_ref[...])


def _odd_in_t(x, g, wq, wkvt, wft, b_f_col, *, B, T, tm, nh, d_c):
    n, d = x.shape
    nb = T // tm
    c_w = nh * d_c
    tok = lambda w: pl.BlockSpec((tm, w), lambda b, i: (b * nb + i, 0))
    kv = pl.BlockSpec((1, nh, d_c, tm), lambda b, i: (b, 0, 0, i))
    return pl.pallas_call(
        functools.partial(_odd_in_t_kernel, nh=nh, d_c=d_c),
        grid=(B, nb),
        in_specs=[tok(d), _resident(g.shape), _resident(wq.shape), _resident(wkvt.shape), _resident(wft.shape),
                  _resident(b_f_col.shape)],
        out_specs=[tok(c_w), kv, kv, pl.BlockSpec((1, nh, tm), lambda b, i: (b, 0, i))],
        out_shape=[jax.ShapeDtypeStruct((n, c_w), BF16), jax.ShapeDtypeStruct((B, nh, d_c, T), F32),
                   jax.ShapeDtypeStruct((B, nh, d_c, T), F32), jax.ShapeDtypeStruct((B, nh, T), F32)],
        compiler_params=_params("parallel", "parallel"),
        name="odd_in_t",
    )(x, g, wq, wkvt, wft, b_f_col)


def _gla_kernel(q_ref, k_ref, v_ref, gate_ref, la_ref, s0_ref, g_ref, o_ref, sfin_ref, s_scr, *, bb, L, nh, dk, dv):
    c = pl.program_id(1)

    @pl.when(c == 0)
    def _():
        s_scr[...] = s0_ref[...]

    row = lax.broadcasted_iota(jnp.int32, (L, L), 0)
    col = lax.broadcasted_iota(jnp.int32, (L, L), 1)
    causal = row >= col
    tri = causal.astype(F32)
    eye = (lax.broadcasted_iota(jnp.int32, (dk, dk), 0) == lax.broadcasted_iota(jnp.int32, (dk, dk), 1))
    g = g_ref[...]
    bs = range(bb)
    units = [(b, h) for b in bs for h in range(nh)]
    sk = [slice(h * dk, (h + 1) * dk) for h in range(nh)]
    sv = [slice(h * dv, (h + 1) * dv) for h in range(nh)]
    w = nh * dk
    bcum_all = jnp.dot(tri, jnp.concatenate([la_ref[b] for b in bs], axis=1), precision=HIGHEST,
                       preferred_element_type=F32)
    bcum = [bcum_all[:, b * w:(b + 1) * w] for b in bs]
    b_last = [x[L - 1:L, :] for x in bcum]
    q_in = [(q_ref[b] * jnp.exp(bcum[b])).astype(BF16) for b in bs]
    k_in = [(k_ref[b] * jnp.exp(-bcum[b])).astype(BF16) for b in bs]
    k_st = [(k_ref[b] * jnp.exp(b_last[b] - bcum[b])).astype(BF16) for b in bs]
    a_tot = [jnp.exp(x) for x in b_last]
    v = [v_ref[b] for b in bs]
    att = [jnp.where(causal, _dot_nt(q_in[b][:, sk[h]], k_in[b][:, sk[h]]), 0.0).astype(BF16) for b, h in units]
    s_old = [s_scr[b, h] for b, h in units]
    o = [_dot(att[u], v[b][:, sv[h]]) + _dot(q_in[b][:, sk[h]], s_old[u].astype(BF16))
         for u, (b, h) in enumerate(units)]
    for u, (b, h) in enumerate(units):
        a_col = jnp.sum(jnp.where(eye, jnp.broadcast_to(a_tot[b][:, sk[h]], (dk, dk)), 0.0), axis=1, keepdims=True)
        s_scr[b, h] = a_col * s_old[u] + lax.dot_general(k_st[b][:, sk[h]], v[b][:, sv[h]], TN,
                                                          preferred_element_type=F32)
    o = [x * lax.rsqrt(jnp.mean(x * x, axis=-1, keepdims=True) + EPS) * g for x in o]
    for b in bs:
        o_ref[b] = (jnp.concatenate(o[b * nh:(b + 1) * nh], axis=-1) * gate_ref[b]).astype(BF16)

    @pl.when(c == pl.num_programs(1) - 1)
    def _():
        sfin_ref[...] = s_scr[...]


def _gla(q, k, v, gate, la, s0, g, *, bb, L):
    B, T, a_qk = q.shape
    a_v = v.shape[-1]
    _, nh, dk, dv = s0.shape
    n = T // L

    def tok(w):
        return pl.BlockSpec((bb, L, w), lambda i, c: (i, c, 0))

    st = pl.BlockSpec((bb, nh, dk, dv), lambda i, c: (i, 0, 0, 0))
    return pl.pallas_call(
        functools.partial(_gla_kernel, bb=bb, L=L, nh=nh, dk=dk, dv=dv),
        grid=(B // bb, n),
        in_specs=[tok(a_qk), tok(a_qk), tok(a_v), tok(a_v), tok(a_qk), st, pl.BlockSpec(g.shape, lambda i, c: (0, 0))],
        out_specs=[tok(a_v), st],
        out_shape=[jax.ShapeDtypeStruct((B, T, a_v), BF16), jax.ShapeDtypeStruct(s0.shape, F32)],
        scratch_shapes=[pltpu.VMEM((bb, nh, dk, dv), F32)],
        compiler_params=_params("parallel", "arbitrary"),
        name="gla",
    )(q, k, v, gate, la, s0, g)


def _bias_table_kernel(rel_ref, out_ref, *, chunk, win, n_rel):
    n_var, nh, rows, width = out_ref.shape
    back = win - chunk
    wide = width + back + rows
    assert wide % 128 == 0 and chunk & (chunk - 1) == 0
    u = lax.broadcasted_iota(jnp.int32, (n_rel, wide), 1)
    want = jnp.clip(back + rows - 1 - u, -MAX_REL, MAX_REL) + MAX_REL
    onehot = (lax.broadcasted_iota(jnp.int32, (n_rel, wide), 0) == want).astype(F32)
    vec = jnp.dot(rel_ref[...], onehot, precision=HIGHEST, preferred_element_type=F32)
    i = lax.broadcasted_iota(jnp.int32, (rows, width), 0)
    j = lax.broadcasted_iota(jnp.int32, (rows, width), 1)
    first = i - (i & (chunk - 1))
    for t in range(n_var):
        shift = back - t * rows if t < n_var - 1 else 0
        ok = (j + shift >= first) & (j + shift < first + win)
        for h in range(nh):
            x = jnp.broadcast_to(vec[h:h + 1, :], (rows, wide))
            x = pltpu.roll(x, (wide - (shift + rows - 1)) % wide, 1, stride=1, stride_axis=0)
            out_ref[t, h] = jnp.where(ok, x[:, :width], -jnp.inf)


def _bias_table(rel_padded, *, chunk, win, rows):
    nh, n_rel = rel_padded.shape
    n_var = (win - chunk) // rows + 1
    return pl.pallas_call(
        functools.partial(_bias_table_kernel, chunk=chunk, win=win, n_rel=n_rel),
        out_shape=jax.ShapeDtypeStruct((n_var, nh, rows, win + rows - chunk), F32),
        compiler_params=_params(),
        name="band_bias_table",
    )(rel_padded)


def _pair_masks(d):
    lane = lax.broadcasted_iota(jnp.int32, (1, 2 * d), 1)
    return lane, [lane < d, lane >= d]


def _band_prompt_kernel(q_ref, k_ref, v_ref, bias_ref, o_ref, *, nh, d, back):
    n_var, _, tq, width = bias_ref.shape
    c = pl.program_id(1)
    start = pl.multiple_of(jnp.maximum(c * tq - back, 0), tq)
    var = jnp.minimum(c, n_var - 1)
    q = q_ref[0]
    kw = k_ref[0, pl.ds(start, width), :]
    vw = v_ref[0, pl.ds(start, width), :]
    lane, mine = _pair_masks(d)
    ones_lane = [d, 0]
    units = [(p, hh) for p in range(nh // 2) for hh in range(2)]
    cols = [slice(p * 2 * d, (p + 1) * 2 * d) for p in range(nh // 2)]
    zero = jnp.zeros((), BF16)
    s = [_dot_nt(jnp.where(mine[hh], q[:, cols[p]], zero), kw[:, cols[p]]) + bias_ref[var, 2 * p + hh]
         for p, hh in units]
    m = [jnp.max(x, axis=-1, keepdims=True) for x in s]
    e = [jnp.exp(x - mx).astype(BF16) for x, mx in zip(s, m)]
    r = [_dot(e[u], jnp.where(mine[hh], vw[:, cols[p]], (lane == ones_lane[hh]).astype(BF16)))
         for u, (p, hh) in enumerate(units)]
    r = [x / x[:, ones_lane[hh]:ones_lane[hh] + 1] for x, (p, hh) in zip(r, units)]
    o_ref[0] = jnp.concatenate([jnp.where(mine[0], r[2 * p], r[2 * p + 1]) for p in range(nh // 2)],
                               axis=-1).astype(BF16)


def _band_prompt(q, k, v, bias, *, nh, d):
    B, T, w = q.shape
    n_var, _, tq, width = bias.shape
    assert T >= width and T % tq == 0
    kv = pl.BlockSpec((1, T, w), lambda b, c: (b, 0, 0))
    return pl.pallas_call(
        functools.partial(_band_prompt_kernel, nh=nh, d=d, back=width - tq),
        grid=(B, T // tq),
        in_specs=[pl.BlockSpec((1, tq, w), lambda b, c: (b, c, 0)), kv, kv, _resident(bias.shape)],
        out_specs=pl.BlockSpec((1, tq, w), lambda b, c: (b, c, 0)),
        out_shape=jax.ShapeDtypeStruct((B, T, w), BF16),
        compiler_params=_params("parallel", "arbitrary"),
        name="band_prompt",
    )(q, k, v, bias)


def _band_sample_kernel(q_ref, ckt_ref, cvt_ref, kn_ref, vn_ref, bias_ref, o_ref, *, nh, d, s_new):
    lc = ckt_ref.shape[3]
    q = q_ref[0]
    kn = kn_ref[0]
    vn = vn_ref[0]
    outs = []
    for h in range(nh):
        sl = slice(h * d, (h + 1) * d)
        bias = bias_ref[0, h]
        sc = _dot(q[:, sl], ckt_ref[0, h].astype(BF16)) + bias[:s_new, :lc]
        sn = _dot_nt(q[:, sl], kn[:, sl]) + bias[:s_new, lc:lc + s_new]
        m = jnp.maximum(jnp.max(sc, axis=-1, keepdims=True), jnp.max(sn, axis=-1, keepdims=True))
        ec = jnp.exp(sc - m)
        en = jnp.exp(sn - m)
        l = jnp.sum(ec, axis=-1, keepdims=True) + jnp.sum(en, axis=-1, keepdims=True)
        outs.append((_dot_nt(ec.astype(BF16), cvt_ref[0, h].astype(BF16)) + _dot(en.astype(BF16), vn[:, sl])) / l)
    o_ref[0] = jnp.concatenate(outs, axis=-1).astype(BF16)


def _band_sample(q, ckt, cvt, kn, vn, bias, *, nh, d):
    B, s_new, w = q.shape
    lc = ckt.shape[3]
    n_var, _, rows, width = bias.shape
    assert s_new <= CHUNK and lc == width - rows
    new = pl.BlockSpec((1, s_new, w), lambda b: (b, 0, 0))
    old = pl.BlockSpec((1, nh, d, lc), lambda b: (b, 0, 0, 0))
    return pl.pallas_call(
        functools.partial(_band_sample_kernel, nh=nh, d=d, s_new=s_new),
        grid=(B,),
        in_specs=[new, old, old, new, new, pl.BlockSpec((1, nh, rows, width), lambda b: (n_var - 1, 0, 0, 0))],
        out_specs=new,
        out_shape=jax.ShapeDtypeStruct((B, s_new, w), BF16),
        compiler_params=_params("parallel"),
        name="band_sample",
    )(q, ckt, cvt, kn, vn, bias)


def _cumsum_lanes_kernel(x_ref, out_ref, *, blk, suffix):
    R, T = x_ref.shape
    src = lax.broadcasted_iota(jnp.int32, (blk, blk), 0)
    dst = lax.broadcasted_iota(jnp.int32, (blk, blk), 1)
    tri = ((src > dst) if suffix else (src <= dst)).astype(F32)
    carry = jnp.zeros((R, 1), F32)
    n = T // blk
    for i in (range(n - 1, -1, -1) if suffix else range(n)):
        xb = x_ref[:, i * blk:(i + 1) * blk]
        out_ref[:, i * blk:(i + 1) * blk] = jnp.dot(xb, tri, precision=HIGHEST, preferred_element_type=F32) + carry
        carry = carry + jnp.sum(xb, axis=1, keepdims=True)


def _cumsum_lanes(x, *, suffix):
    return pl.pallas_call(
        functools.partial(_cumsum_lanes_kernel, blk=256, suffix=suffix),
        out_shape=jax.ShapeDtypeStruct(x.shape, F32),
        compiler_params=_params(),
        name="cumsum_lanes",
    )(x)


def _fox_prompt_kernel(q_ref, kt_ref, vt_ref, ft_ref, o_ref, *, tq, tk, d):
    p = pl.program_id(1)
    T = q_ref.shape[1]
    assert tq == tk and d % 8 == 0
    nq = T // tq
    lane, mine = _pair_masks(d)
    q_fill = [((lane >= d) & (lane < d + 3)).astype(BF16), (lane < 3).astype(BF16)]
    den = [d, 0]
    sub = lax.broadcasted_iota(jnp.int32, (8, tk), 0)
    ones_blk = (sub == 0).astype(F32)
    pad_blk = jnp.zeros((d - 8, tk), F32)

    def stacked(x, blk, hh):
        return jnp.concatenate([x, blk, pad_blk] if hh == 0 else [blk, pad_blk, x], axis=0).astype(BF16)

    diag_mask = (lax.broadcasted_iota(jnp.int32, (tq, tq), 0) >= lax.broadcasted_iota(jnp.int32, (tq, tq), 1))
    units = [(qi, hh) for qi in range(nq) for hh in range(2)]
    qs = {(qi, hh): jnp.where(mine[hh], q_ref[0, qi * tq:(qi + 1) * tq, :], q_fill[hh]) for qi, hh in units}
    m = {u: jnp.full((tq, 1), -jnp.inf, F32) for u in units}
    acc = {u: jnp.zeros((tq, 2 * d), F32) for u in units}
    for kv in range(nq):
        ks = slice(kv * tk, (kv + 1) * tk)
        kt, vt = [], []
        for hh in range(2):
            f = ft_ref[0, pl.ds(2 * p + hh, 1), ks] * LOG2E
            hi = f.astype(BF16).astype(F32)
            mid = (f - hi).astype(BF16).astype(F32)
            f_blk = jnp.where(sub == 0, -hi, jnp.where(sub == 1, -mid, jnp.where(sub == 2, -(f - hi - mid), 0.0)))
            kt.append(stacked(kt_ref[0, hh, :, ks], f_blk, hh))
            vt.append(stacked(vt_ref[0, hh, :, ks], ones_blk, hh))
        live = [u for u in units if u[0] >= kv]
        s = {u: _dot(qs[u], kt[u[1]]) for u in live}
        s[kv, 0], s[kv, 1] = (jnp.where(diag_mask, s[kv, hh], -jnp.inf) for hh in range(2))
        m_new = {u: jnp.maximum(m[u], jnp.max(s[u], axis=-1, keepdims=True)) for u in live}
        e = {u: jnp.exp2(s[u] - m_new[u]).astype(BF16) for u in live}
        for u in live:
            acc[u] = jnp.exp2(m[u] - m_new[u]) * acc[u] + _dot_nt(e[u], vt[u[1]])
            m[u] = m_new[u]
        out = [acc[kv, hh] / acc[kv, hh][:, den[hh]:den[hh] + 1] for hh in range(2)]
        o_ref[0, kv * tq:(kv + 1) * tq, :] = jnp.where(mine[0], out[0], out[1]).astype(BF16)


def _fox_prompt(q, kt, vt, ft, *, tq, tk):
    B, T, w = q.shape
    _, nh, d, _ = kt.shape
    pair = 2 * d
    kv = pl.BlockSpec((1, 2, d, T), lambda b, p: (b, p, 0, 0))
    qo = pl.BlockSpec((1, T, pair), lambda b, p: (b, 0, p))
    return pl.pallas_call(
        functools.partial(_fox_prompt_kernel, tq=tq, tk=tk, d=d),
        grid=(B, nh // 2),
        in_specs=[qo, kv, kv, pl.BlockSpec((1, nh, T), lambda b, p: (b, 0, 0))],
        out_specs=qo,
        out_shape=jax.ShapeDtypeStruct((B, T, w), BF16),
        compiler_params=_params("parallel", "arbitrary"),
        name="fox_prompt",
    )(q, kt, vt, ft)


def _fox_sample_kernel(q_ref, kt_ref, vt_ref, ct_ref, kn_ref, vn_ref, lf_ref, o_ref, fqt_scr, *, hg, d, s_new):
    g = pl.program_id(1)
    nh = lf_ref.shape[2]
    tri = (lax.broadcasted_iota(jnp.int32, (s_new, s_new), 0)
           >= lax.broadcasted_iota(jnp.int32, (s_new, s_new), 1))
    eye = (lax.broadcasted_iota(jnp.int32, (nh, nh), 0) == lax.broadcasted_iota(jnp.int32, (nh, nh), 1)).astype(F32)
    fq = jnp.dot(tri.astype(F32), lf_ref[0], precision=HIGHEST, preferred_element_type=F32)
    fqt_scr[...] = lax.dot_general(eye, fq, NT, precision=HIGHEST, preferred_element_type=F32)
    outs = []
    for hh in range(hg):
        sl = slice(hh * d, (hh + 1) * d)
        head = g * hg + hh
        qh = q_ref[0, 0, :, sl]
        sc = _dot(qh, kt_ref[0, hh].astype(BF16)) + ct_ref[0, pl.ds(head, 1), :]
        sn = _dot_nt(qh, kn_ref[0, 0, :, sl]) - fqt_scr[pl.ds(head, 1), :]
        sn = jnp.where(tri, sn, -jnp.inf)
        m = jnp.maximum(jnp.max(sc, axis=-1, keepdims=True), jnp.max(sn, axis=-1, keepdims=True))
        ec = jnp.exp(sc - m)
        en = jnp.exp(sn - m)
        l = jnp.sum(ec, axis=-1, keepdims=True) + jnp.sum(en, axis=-1, keepdims=True)
        o = _dot_nt(ec.astype(BF16), vt_ref[0, hh].astype(BF16)) + _dot(en.astype(BF16), vn_ref[0, 0, :, sl])
        outs.append(o / l)
    o_ref[0, 0] = jnp.concatenate(outs, axis=-1).astype(BF16)


def _fox_sample(q, kt, vt, ct, kn, vn, lf, *, hg):
    B, ng, s_new, gw = q.shape
    _, nh, d, lc = kt.shape
    new = pl.BlockSpec((1, 1, s_new, gw), lambda b, g: (b, g, 0, 0))
    old = pl.BlockSpec((1, hg, d, lc), lambda b, g: (b, g, 0, 0))
    return pl.pallas_call(
        functools.partial(_fox_sample_kernel, hg=hg, d=d, s_new=s_new),
        grid=(B, ng),
        in_specs=[new, old, old, pl.BlockSpec((1, nh, lc), lambda b, g: (b, 0, 0)), new, new,
                  pl.BlockSpec((1, s_new, nh), lambda b, g: (b, 0, 0))],
        out_specs=new,
        out_shape=jax.ShapeDtypeStruct(q.shape, BF16),
        scratch_shapes=[pltpu.VMEM((nh, s_new), F32)],
        compiler_params=_params("parallel", "arbitrary"),
        name="fox_sample",
    )(q, kt, vt, ct, kn, vn, lf)


def _trunk(x, state, wts, dims, *, tm, gla_bb):
    B, T, D = x.shape
    n = B * T
    depth = wts["norm_g"].shape[0]
    h_a, dk_a, dv_a, h_b, d_b, h_c, d_c = dims
    a_qk, a_v, b_w, c_w = h_a * dk_a, h_a * dv_a, h_b * d_b, h_c * d_c
    xs = x.reshape(n, D)
    new_states = []
    pending = ()
    r3 = lambda t: t.reshape(B, T, t.shape[-1])
    for l in range(depth):
        i = l // 2
        g = wts["norm_g"][l]
        xs = _ffn(xs, g[0:1], wts["w_gu"], wts["w_down"], (l, 0), pending, tm=tm)
        if l % 2 == 0:
            prompt = state is None
            keep = min(BAND_CHUNKS * CHUNK, T)
            qa, ka, va, gate, la, bq, bk, bv, nk, nv = _even_in(
                xs, g[1:2], wts["even_main"][i], wts["even_r"][i], wts["alpha_up"][i], wts["b_alpha"][i],
                wts["even_kvt"][i] if prompt else None,
                B=B if prompt else 1, T=T if prompt else n, tm=tm, a_qk=a_qk, a_v=a_v, b_w=b_w, dk_a=dk_a, d_b=d_b,
                tail=keep if prompt else 0)
            s0 = jnp.zeros((B, h_a, dk_a, dv_a), F32) if prompt else state[l][0]
            o_a, s_fin = _gla(r3(qa), r3(ka), r3(va), r3(gate), r3(la), s0, wts["gla_g"][i],
                              bb=gla_bb, L=min(CHUNK, T))
            if prompt:
                o_b = _band_prompt(r3(bq), r3(bk), r3(bv), wts["band_bias"][i], nh=h_b, d=d_b)
                nk, nv = (t.reshape(B, h_b, d_b, keep).transpose(0, 3, 1, 2) for t in (nk, nv))
            else:
                ck, cv = state[l][1], state[l][2]
                o_b = _band_sample(r3(bq), ck.transpose(0, 2, 3, 1), cv.transpose(0, 2, 3, 1), r3(bk), r3(bv),
                                   wts["band_bias"][i], nh=h_b, d=d_b)
                nk, nv = (t.reshape(B, T, h_b, d_b) for t in (nk, nv))
            new_states.append((s_fin, nk, nv))
            pending = ((o_a.reshape(n, a_v), wts["even_out_a"][i]), (o_b.reshape(n, b_w), wts["even_out_b"][i]))
        elif state is None:
            q, kt, vt, lft = _odd_in_t(xs, g[1:2], wts["odd_q"][i], wts["odd_kvt"][i], wts["odd_ft"][i],
                                       wts["b_f_col"][i], B=B, T=T, tm=tm, nh=h_c, d_c=d_c)
            ft = _cumsum_lanes(lft.reshape(B * h_c, T), suffix=False).reshape(B, h_c, T)
            o = _fox_prompt(r3(q), kt, vt, ft, tq=512, tk=512)
            new_states.append((kt.transpose(0, 3, 1, 2), vt.transpose(0, 3, 1, 2), lft.transpose(0, 2, 1)))
            pending = ((o.reshape(n, c_w), wts["odd_out"][i]),)
        else:
            q, k, v, k32, v32, lf = _odd_in(xs, g[1:2], wts["odd_main"][i], wts["odd_f"][i], wts["b_f"][i],
                                             tm=tm, c_w=c_w, d_c=d_c, h_c=h_c)
            ck, cv, clf = state[l]
            lc = ck.shape[1]
            hg = 4
            ct = _cumsum_lanes(clf.astype(F32).transpose(0, 2, 1).reshape(B * h_c, lc), suffix=True)
            grp = lambda t: t.reshape(B, T, h_c // hg, hg * d_c).transpose(0, 2, 1, 3)
            o = _fox_sample(grp(q), ck.transpose(0, 2, 3, 1), cv.transpose(0, 2, 3, 1), ct.reshape(B, h_c, lc),
                            grp(k), grp(v), r3(lf), hg=hg)
            o = o.transpose(0, 2, 1, 3).reshape(n, c_w)
            new_states.append((r3(k32).reshape(B, T, h_c, d_c), r3(v32).reshape(B, T, h_c, d_c), r3(lf)))
            pending = ((o, wts["odd_out"][i]),)
        last = l == depth - 1
        xs = _ffn(xs, g[2:3], wts["w_gu"], wts["w_down"], (l, 1), pending,
                  wts["final_g"] if last else None, tm=tm)
        pending = ()
    return xs.reshape(B, T, D), new_states


def kernel(x_prompt, x_sample, state_gla, cache_band_k, cache_band_v, cache_fox_k, cache_fox_v, cache_fox_logf,
           norm_g, ffn_w_gu, ffn_w_down, even_w_in, gla_w_alpha_up, gla_b_alpha, gla_norm_g, band_rel_bias,
           even_w_out, odd_w_in, fox_b_f, odd_w_out, final_norm_g):
    depth = norm_g.shape[0]
    _, _, h_a, dk_a, dv_a = state_gla.shape
    h_b, d_b = cache_band_k.shape[-2:]
    h_c, d_c = cache_fox_k.shape[-2:]
    r_a = gla_w_alpha_up.shape[1]
    a_qk, a_v, b_w, c_w = h_a * dk_a, h_a * dv_a, h_b * d_b, h_c * d_c
    assert cache_band_k.shape[2] == BAND_CHUNKS * CHUNK and band_rel_bias.shape[-1] == 2 * MAX_REL + 1

    r0 = 2 * a_qk + 2 * a_v
    n_rel = 384
    rel_padded = jnp.pad(band_rel_bias, ((0, 0), (0, 0), (0, n_rel - band_rel_bias.shape[-1])))
    win = (BAND_CHUNKS + 1) * CHUNK
    odd_t = odd_w_in.transpose(0, 2, 1)
    wts = {
        "norm_g": norm_g,
        "w_gu": ffn_w_gu.astype(BF16),
        "w_down": ffn_w_down.astype(BF16),
        "even_main": jnp.concatenate([even_w_in[:, :, :r0], even_w_in[:, :, r0 + r_a:]], axis=-1).astype(BF16),
        "even_r": even_w_in[:, :, r0:r0 + r_a].astype(BF16),
        "even_kvt": even_w_in.transpose(0, 2, 1)[:, r0 + r_a + b_w:].astype(BF16),
        "alpha_up": gla_w_alpha_up.astype(BF16),
        "b_alpha": gla_b_alpha[:, None, :],
        "gla_g": gla_norm_g[:, None, :],
        "band_bias": [_bias_table(rel_padded[i], chunk=CHUNK, win=win, rows=2 * CHUNK)
                      for i in range(rel_padded.shape[0])],
        "even_out_a": even_w_out[:, :a_v].astype(BF16),
        "even_out_b": even_w_out[:, a_v:].astype(BF16),
        "odd_main": odd_w_in[:, :, :3 * c_w].astype(BF16),
        "odd_f": odd_w_in[:, :, 3 * c_w:].astype(BF16),
        "odd_q": odd_w_in[:, :, :c_w].astype(BF16),
        "odd_kvt": odd_t[:, c_w:3 * c_w].astype(BF16),
        "odd_ft": odd_t[:, 3 * c_w:].astype(BF16),
        "b_f": fox_b_f[:, None, :],
        "b_f_col": fox_b_f[:, :, None],
        "odd_out": odd_w_out.astype(BF16),
        "final_g": final_norm_g[None, :],
    }
    dims = (h_a, dk_a, dv_a, h_b, d_b, h_c, d_c)

    sample_states = []
    for l in range(depth):
        i = l // 2
        if l % 2 == 0:
            sample_states.append((state_gla[i], cache_band_k[i], cache_band_v[i]))
        else:
            sample_states.append((cache_fox_k[i], cache_fox_v[i], cache_fox_logf[i]))

    y_prompt, ns_p = _trunk(x_prompt, None, wts, dims, tm=512, gla_bb=8)
    y_sample, ns_s = _trunk(x_sample, sample_states, wts, dims, tm=256, gla_bb=8)

    ev = range(0, depth, 2)
    od = range(1, depth, 2)
    stack = lambda ns, layers, j: jnp.stack([ns[l][j] for l in layers])
    return (y_prompt, y_sample,
            stack(ns_p, ev, 0), stack(ns_p, ev, 1), stack(ns_p, ev, 2),
            stack(ns_p, od, 0), stack(ns_p, od, 1), stack(ns_p, od, 2),
            stack(ns_s, ev, 0), stack(ns_s, ev, 1), stack(ns_s, ev, 2),
            stack(ns_s, od, 0), stack(ns_s, od, 1), stack(ns_s, od, 2))
```

```python
import functools

import jax
import jax.numpy as jnp
from jax import lax
from jax.experimental import pallas as pl
from jax.experimental.pallas import tpu as pltpu

F32 = jnp.float32
BF16 = jnp.bfloat16
HIGHEST = lax.Precision.HIGHEST
NT = (((1,), (1,)), ((), ()))
TN = (((0,), (0,)), ((), ()))

EPS = 1e-6
CHUNK = 64
BAND_CHUNKS = 8
MAX_REL = 128
GATE_NORM_A = 16.0
LOG2E = 1.4426950408889634
VMEM_LIMIT_BYTES = 56 * 1024 * 1024


def _params(*sem):
    return pltpu.CompilerParams(dimension_semantics=sem, vmem_limit_bytes=VMEM_LIMIT_BYTES)


def _resident(shape):
    nd = len(shape)
    return pl.BlockSpec(shape, lambda *_: (0,) * nd, pipeline_mode=pl.Buffered(1))


def _resident_slab(stacked, idx):
    shape = stacked.shape[len(idx):]
    return pl.BlockSpec((None,) * len(idx) + shape, lambda *_: tuple(idx) + (0,) * len(shape),
                        pipeline_mode=pl.Buffered(1))


def _rows(tm, width):
    return pl.BlockSpec((tm, width), lambda i: (i, 0))


def _rms(x, g):
    ms = jnp.mean(x * x, axis=-1, keepdims=True)
    return x * lax.rsqrt(ms + EPS) * g


def _log_sigmoid(x):
    return jnp.minimum(x, 0.0) - jnp.log1p(jnp.exp(-jnp.abs(x)))


def _dot(a, b):
    return jnp.dot(a, b, preferred_element_type=F32)


def _dot_nt(a, b):
    return lax.dot_general(a, b, NT, preferred_element_type=F32)


def _ffn_kernel(*refs, n_proj, has_final, d_ff):
    it = iter(refs)
    x_ref = next(it)
    proj = [(next(it), next(it)) for _ in range(n_proj)]
    g_ref, wgu_ref, wd_ref = next(it), next(it), next(it)
    gf_ref = next(it) if has_final else None
    out_ref = next(it)

    x = x_ref[...]
    for o_ref, w_ref in proj:
        x = x + _dot(o_ref[...], w_ref[...])
    xn = _rms(x, g_ref[...]).astype(BF16)
    h = _dot(xn, wgu_ref[...])
    a = (jax.nn.silu(h[:, :d_ff]) * h[:, d_ff:]).astype(BF16)
    y = x + 0.5 * _dot(a, wd_ref[...])
    if has_final:
        y = _rms(y, gf_ref[...])
    out_ref[...] = y


def _ffn(x, g, wgu, wd, widx, proj=(), final_g=None, *, tm):
    n, d = x.shape
    d_ff = wd.shape[-2]
    args, specs = [x], [_rows(tm, d)]
    for o, w in proj:
        args += [o, w]
        specs += [_rows(tm, o.shape[1]), _resident(w.shape)]
    args += [g, wgu, wd]
    specs += [_resident(g.shape), _resident_slab(wgu, widx), _resident_slab(wd, widx)]
    if final_g is not None:
        args.append(final_g)
        specs.append(_resident(final_g.shape))
    return pl.pallas_call(
        functools.partial(_ffn_kernel, n_proj=len(proj), has_final=final_g is not None, d_ff=d_ff),
        grid=(n // tm,),
        in_specs=specs,
        out_specs=_rows(tm, d),
        out_shape=jax.ShapeDtypeStruct((n, d), F32),
        compiler_params=_params("parallel"),
        name="ffn",
    )(*args)


def _even_in_kernel(x_ref, g_ref, w_ref, wr_ref, wup_ref, ba_ref, *rest, a_qk, a_v, b_w, dk_a, d_b, tail):
    if tail:
        wkvt_ref, *rest = rest
    qa_ref, ka_ref, va_ref, gate_ref, la_ref, bq_ref, bk_ref, bv_ref, x1_ref, x2_ref = rest
    xn = _rms(x_ref[...], g_ref[...]).astype(BF16)
    h = _dot(xn, w_ref[...])
    o = 0
    qa_ref[...] = h[:, o:o + a_qk] * (dk_a ** -0.5)
    o += a_qk
    ka_ref[...] = h[:, o:o + a_qk]
    o += a_qk
    va_ref[...] = h[:, o:o + a_v].astype(BF16)
    o += a_v
    gate_ref[...] = jax.nn.silu(h[:, o:o + a_v])
    o += a_v
    bq_ref[...] = (h[:, o:o + b_w] * (LOG2E * d_b ** -0.5)).astype(BF16)
    o += b_w
    bk = h[:, o:o + b_w]
    o += b_w
    bv = h[:, o:o + b_w]
    bk_ref[...] = bk.astype(BF16)
    bv_ref[...] = bv.astype(BF16)
    if tail:
        @pl.when(pl.program_id(1) == pl.num_programs(1) - 1)
        def _():
            kvt = _dot_nt(wkvt_ref[...], xn[xn.shape[0] - tail:])
            x1_ref[0] = kvt[:b_w]
            x2_ref[0] = kvt[b_w:]
    else:
        x1_ref[...] = bk
        x2_ref[...] = bv
    r = _dot(xn, wr_ref[...]).astype(BF16)
    z = _dot(r, wup_ref[...]) + ba_ref[...]
    la_ref[...] = _log_sigmoid(z) / GATE_NORM_A


def _even_in(x, g, w_main, w_r, w_up, b_alpha, w_kvt=None, *, B, T, tm, a_qk, a_v, b_w, dk_a, d_b, tail):
    n, d = x.shape
    nb = T // tm
    tok = lambda w: pl.BlockSpec((tm, w), lambda b, i: (b * nb + i, 0))
    widths = [(a_qk, F32), (a_qk, F32), (a_v, BF16), (a_v, F32), (a_qk, F32), (b_w, BF16), (b_w, BF16), (b_w, BF16)]
    args = [x, g, w_main, w_r, w_up, b_alpha]
    if tail:
        assert tail <= tm
        args.append(w_kvt)
        extra_specs = [pl.BlockSpec((1, b_w, tail), lambda b, i: (b, 0, 0))] * 2
        extra_shapes = [jax.ShapeDtypeStruct((B, b_w, tail), F32)] * 2
    else:
        extra_specs = [tok(b_w)] * 2
        extra_shapes = [jax.ShapeDtypeStruct((n, b_w), F32)] * 2
    return pl.pallas_call(
        functools.partial(_even_in_kernel, a_qk=a_qk, a_v=a_v, b_w=b_w, dk_a=dk_a, d_b=d_b, tail=tail),
        grid=(B, nb),
        in_specs=[tok(d)] + [_resident(a.shape) for a in args[1:]],
        out_specs=[tok(w) for w, _ in widths] + extra_specs,
        out_shape=[jax.ShapeDtypeStruct((n, w), dt) for w, dt in widths] + extra_shapes,
        compiler_params=_params("parallel", "arbitrary"),
        name="even_in",
    )(*args)


def _odd_in_kernel(x_ref, g_ref, w_ref, wf_ref, bf_ref, q_ref, k_ref, v_ref, k32_ref, v32_ref, lf_ref, *, c_w, d_c):
    xn = _rms(x_ref[...], g_ref[...]).astype(BF16)
    h = _dot(xn, w_ref[...])
    q_ref[...] = (h[:, :c_w] * (d_c ** -0.5)).astype(BF16)
    k = h[:, c_w:2 * c_w]
    v = h[:, 2 * c_w:]
    k32_ref[...] = k
    v32_ref[...] = v
    k_ref[...] = k.astype(BF16)
    v_ref[...] = v.astype(BF16)
    lf_ref[...] = _log_sigmoid(_dot(xn, wf_ref[...]) + bf_ref[...])


def _odd_in(x, g, w_main, w_f, b_f, *, tm, c_w, d_c, h_c):
    n, d = x.shape
    outs = [(c_w, BF16), (c_w, BF16), (c_w, BF16), (c_w, F32), (c_w, F32), (h_c, F32)]
    return pl.pallas_call(
        functools.partial(_odd_in_kernel, c_w=c_w, d_c=d_c),
        grid=(n // tm,),
        in_specs=[_rows(tm, d), _resident(g.shape), _resident(w_main.shape), _resident(w_f.shape),
                  _resident(b_f.shape)],
        out_specs=[_rows(tm, w) for w, _ in outs],
        out_shape=[jax.ShapeDtypeStruct((n, w), dt) for w, dt in outs],
        compiler_params=_params("parallel"),
        name="odd_in",
    )(x, g, w_main, w_f, b_f)


def _odd_in_t_kernel(x_ref, g_ref, wq_ref, wkvt_ref, wft_ref, bf_ref, q_ref, kt_ref, vt_ref, lft_ref, *, nh, d_c):
    xn = _rms(x_ref[...], g_ref[...]).astype(BF16)
    tm = xn.shape[0]
    c_w = nh * d_c
    q_ref[...] = (_dot(xn, wq_ref[...]) * (LOG2E * d_c ** -0.5)).astype(BF16)
    ht = _dot_nt(wkvt_ref[...], xn)
    kt_ref[0] = ht[:c_w].reshape(nh, d_c, tm)
    vt_ref[0] = ht[c_w:].reshape(nh, d_c, tm)
    lft_ref[0] = _log_sigmoid(_dot_nt(wft_ref[...], xn) + bf_ref[...])


def _odd_in_t(x, g, wq, wkvt, wft, b_f_col, *, B, T, tm, nh, d_c):
    n, d = x.shape
    nb = T // tm
    c_w = nh * d_c
    tok = lambda w: pl.BlockSpec((tm, w), lambda b, i: (b * nb + i, 0))
    kv = pl.BlockSpec((1, nh, d_c, tm), lambda b, i: (b, 0, 0, i))
    return pl.pallas_call(
        functools.partial(_odd_in_t_kernel, nh=nh, d_c=d_c),
        grid=(B, nb),
        in_specs=[tok(d), _resident(g.shape), _resident(wq.shape), _resident(wkvt.shape), _resident(wft.shape),
                  _resident(b_f_col.shape)],
        out_specs=[tok(c_w), kv, kv, pl.BlockSpec((1, nh, tm), lambda b, i: (b, 0, i))],
        out_shape=[jax.ShapeDtypeStruct((n, c_w), BF16), jax.ShapeDtypeStruct((B, nh, d_c, T), F32),
                   jax.ShapeDtypeStruct((B, nh, d_c, T), F32), jax.ShapeDtypeStruct((B, nh, T), F32)],
        compiler_params=_params("parallel", "parallel"),
        name="odd_in_t",
    )(x, g, wq, wkvt, wft, b_f_col)


def _gla_kernel(q_ref, k_ref, v_ref, gate_ref, la_ref, s0_ref, g_ref, o_ref, sfin_ref, s_scr, *, bb, L, nh, dk, dv):
    c = pl.program_id(1)

    @pl.when(c == 0)
    def _():
        s_scr[...] = s0_ref[...]

    row = lax.broadcasted_iota(jnp.int32, (L, L), 0)
    col = lax.broadcasted_iota(jnp.int32, (L, L), 1)
    causal = row >= col
    tri = causal.astype(F32)
    eye = (lax.broadcasted_iota(jnp.int32, (dk, dk), 0) == lax.broadcasted_iota(jnp.int32, (dk, dk), 1))
    g = g_ref[...]
    bs = range(bb)
    units = [(b, h) for b in bs for h in range(nh)]
    sk = [slice(h * dk, (h + 1) * dk) for h in range(nh)]
    sv = [slice(h * dv, (h + 1) * dv) for h in range(nh)]
    w = nh * dk
    bcum_all = jnp.dot(tri, jnp.concatenate([la_ref[b] for b in bs], axis=1), precision=HIGHEST,
                       preferred_element_type=F32)
    bcum = [bcum_all[:, b * w:(b + 1) * w] for b in bs]
    b_last = [x[L - 1:L, :] for x in bcum]
    q_in = [(q_ref[b] * jnp.exp(bcum[b])).astype(BF16) for b in bs]
    k_in = [(k_ref[b] * jnp.exp(-bcum[b])).astype(BF16) for b in bs]
    k_st = [(k_ref[b] * jnp.exp(b_last[b] - bcum[b])).astype(BF16) for b in bs]
    a_tot = [jnp.exp(x) for x in b_last]
    v = [v_ref[b] for b in bs]
    att = [jnp.where(causal, _dot_nt(q_in[b][:, sk[h]], k_in[b][:, sk[h]]), 0.0).astype(BF16) for b, h in units]
    s_old = [s_scr[b, h] for b, h in units]
    o = [_dot(att[u], v[b][:, sv[h]]) + _dot(q_in[b][:, sk[h]], s_old[u].astype(BF16))
         for u, (b, h) in enumerate(units)]
    for u, (b, h) in enumerate(units):
        a_col = jnp.sum(jnp.where(eye, jnp.broadcast_to(a_tot[b][:, sk[h]], (dk, dk)), 0.0), axis=1, keepdims=True)
        s_scr[b, h] = a_col * s_old[u] + lax.dot_general(k_st[b][:, sk[h]], v[b][:, sv[h]], TN,
                                                          preferred_element_type=F32)
    o = [x * lax.rsqrt(jnp.mean(x * x, axis=-1, keepdims=True) + EPS) * g for x in o]
    for b in bs:
        o_ref[b] = (jnp.concatenate(o[b * nh:(b + 1) * nh], axis=-1) * gate_ref[b]).astype(BF16)

    @pl.when(c == pl.num_programs(1) - 1)
    def _():
        sfin_ref[...] = s_scr[...]


def _gla(q, k, v, gate, la, s0, g, *, bb, L):
    B, T, a_qk = q.shape
    a_v = v.shape[-1]
    _, nh, dk, dv = s0.shape
    n = T // L

    def tok(w):
        return pl.BlockSpec((bb, L, w), lambda i, c: (i, c, 0))

    st = pl.BlockSpec((bb, nh, dk, dv), lambda i, c: (i, 0, 0, 0))
    return pl.pallas_call(
        functools.partial(_gla_kernel, bb=bb, L=L, nh=nh, dk=dk, dv=dv),
        grid=(B // bb, n),
        in_specs=[tok(a_qk), tok(a_qk), tok(a_v), tok(a_v), tok(a_qk), st, pl.BlockSpec(g.shape, lambda i, c: (0, 0))],
        out_specs=[tok(a_v), st],
        out_shape=[jax.ShapeDtypeStruct((B, T, a_v), BF16), jax.ShapeDtypeStruct(s0.shape, F32)],
        scratch_shapes=[pltpu.VMEM((bb, nh, dk, dv), F32)],
        compiler_params=_params("parallel", "arbitrary"),
        name="gla",
    )(q, k, v, gate, la, s0, g)


def _bias_table_kernel(rel_ref, out_ref, *, chunk, win, n_rel):
    n_var, nh, rows, width = out_ref.shape
    back = win - chunk
    wide = width + back + rows
    assert wide % 128 == 0 and chunk & (chunk - 1) == 0
    u = lax.broadcasted_iota(jnp.int32, (n_rel, wide), 1)
    want = jnp.clip(back + rows - 1 - u, -MAX_REL, MAX_REL) + MAX_REL
    onehot = (lax.broadcasted_iota(jnp.int32, (n_rel, wide), 0) == want).astype(F32)
    vec = jnp.dot(rel_ref[...], onehot, precision=HIGHEST, preferred_element_type=F32)
    i = lax.broadcasted_iota(jnp.int32, (rows, width), 0)
    j = lax.broadcasted_iota(jnp.int32, (rows, width), 1)
    first = i - (i & (chunk - 1))
    for t in range(n_var):
        shift = back - t * rows if t < n_var - 1 else 0
        ok = (j + shift >= first) & (j + shift < first + win)
        for h in range(nh):
            x = jnp.broadcast_to(vec[h:h + 1, :], (rows, wide))
            x = pltpu.roll(x, (wide - (shift + rows - 1)) % wide, 1, stride=1, stride_axis=0)
            out_ref[t, h] = jnp.where(ok, x[:, :width] * LOG2E, -jnp.inf)


def _bias_table(rel_padded, *, chunk, win, rows):
    nh, n_rel = rel_padded.shape
    n_var = (win - chunk) // rows + 1
    return pl.pallas_call(
        functools.partial(_bias_table_kernel, chunk=chunk, win=win, n_rel=n_rel),
        out_shape=jax.ShapeDtypeStruct((n_var, nh, rows, win + rows - chunk), F32),
        compiler_params=_params(),
        name="band_bias_table",
    )(rel_padded)


def _pair_masks(d):
    lane = lax.broadcasted_iota(jnp.int32, (1, 2 * d), 1)
    return lane, [lane < d, lane >= d]


def _band_prompt_kernel(q_ref, k_ref, v_ref, bias_ref, o_ref, *, nh, d, back):
    n_var, _, tq, width = bias_ref.shape
    nsub = q_ref.shape[1] // tq
    lane, mine = _pair_masks(d)
    ones_lane = [d, 0]
    cols = [slice(p * 2 * d, (p + 1) * 2 * d) for p in range(nh // 2)]
    zero = jnp.zeros((), BF16)
    q, kw, vw, var = [], [], [], []
    for sb in range(nsub):
        c = pl.program_id(1) * nsub + sb
        start = pl.multiple_of(jnp.maximum(c * tq - back, 0), tq)
        var.append(jnp.minimum(c, n_var - 1))
        q.append(q_ref[0, sb * tq:(sb + 1) * tq, :])
        kw.append(k_ref[0, pl.ds(start, width), :])
        vw.append(v_ref[0, pl.ds(start, width), :])
    units = [(sb, p, hh) for sb in range(nsub) for p in range(nh // 2) for hh in range(2)]
    s = [_dot_nt(jnp.where(mine[hh], q[sb][:, cols[p]], zero), kw[sb][:, cols[p]]) + bias_ref[var[sb], 2 * p + hh]
         for sb, p, hh in units]
    m = [jnp.max(x, axis=-1, keepdims=True) for x in s]
    e = [jnp.exp2(x - mx).astype(BF16) for x, mx in zip(s, m)]
    r = [_dot(e[u], jnp.where(mine[hh], vw[sb][:, cols[p]], (lane == ones_lane[hh]).astype(BF16)))
         for u, (sb, p, hh) in enumerate(units)]
    r = [x / x[:, ones_lane[hh]:ones_lane[hh] + 1] for x, (sb, p, hh) in zip(r, units)]
    for sb in range(nsub):
        rs = r[sb * nh:(sb + 1) * nh]
        o_ref[0, sb * tq:(sb + 1) * tq, :] = jnp.concatenate(
            [jnp.where(mine[0], rs[2 * p], rs[2 * p + 1]) for p in range(nh // 2)], axis=-1).astype(BF16)


def _band_prompt(q, k, v, bias, *, nh, d, nsub=4):
    B, T, w = q.shape
    n_var, _, tq, width = bias.shape
    tq *= nsub
    assert T >= width and T % tq == 0
    kv = pl.BlockSpec((1, T, w), lambda b, c: (b, 0, 0))
    return pl.pallas_call(
        functools.partial(_band_prompt_kernel, nh=nh, d=d, back=width - tq // nsub),
        grid=(B, T // tq),
        in_specs=[pl.BlockSpec((1, tq, w), lambda b, c: (b, c, 0)), kv, kv, _resident(bias.shape)],
        out_specs=pl.BlockSpec((1, tq, w), lambda b, c: (b, c, 0)),
        out_shape=jax.ShapeDtypeStruct((B, T, w), BF16),
        compiler_params=_params("parallel", "arbitrary"),
        name="band_prompt",
    )(q, k, v, bias)


def _band_sample_kernel(q_ref, ckt_ref, cvt_ref, kn_ref, vn_ref, bias_ref, o_ref, *, nh, d, s_new):
    lc = ckt_ref.shape[3]
    q = q_ref[0]
    kn = kn_ref[0]
    vn = vn_ref[0]
    outs = []
    for h in range(nh):
        sl = slice(h * d, (h + 1) * d)
        bias = bias_ref[0, h]
        sc = _dot(q[:, sl], ckt_ref[0, h].astype(BF16)) + bias[:s_new, :lc]
        sn = _dot_nt(q[:, sl], kn[:, sl]) + bias[:s_new, lc:lc + s_new]
        m = jnp.maximum(jnp.max(sc, axis=-1, keepdims=True), jnp.max(sn, axis=-1, keepdims=True))
        ec = jnp.exp2(sc - m)
        en = jnp.exp2(sn - m)
        l = jnp.sum(ec, axis=-1, keepdims=True) + jnp.sum(en, axis=-1, keepdims=True)
        outs.append((_dot_nt(ec.astype(BF16), cvt_ref[0, h].astype(BF16)) + _dot(en.astype(BF16), vn[:, sl])) / l)
    o_ref[0] = jnp.concatenate(outs, axis=-1).astype(BF16)


def _band_sample(q, ckt, cvt, kn, vn, bias, *, nh, d):
    B, s_new, w = q.shape
    lc = ckt.shape[3]
    n_var, _, rows, width = bias.shape
    assert s_new <= CHUNK and lc == width - rows
    new = pl.BlockSpec((1, s_new, w), lambda b: (b, 0, 0))
    old = pl.BlockSpec((1, nh, d, lc), lambda b: (b, 0, 0, 0))
    return pl.pallas_call(
        functools.partial(_band_sample_kernel, nh=nh, d=d, s_new=s_new),
        grid=(B,),
        in_specs=[new, old, old, new, new, pl.BlockSpec((1, nh, rows, width), lambda b: (n_var - 1, 0, 0, 0))],
        out_specs=new,
        out_shape=jax.ShapeDtypeStruct((B, s_new, w), BF16),
        compiler_params=_params("parallel"),
        name="band_sample",
    )(q, ckt, cvt, kn, vn, bias)


def _cumsum_lanes_kernel(x_ref, out_ref, *, blk, suffix):
    R, T = x_ref.shape
    src = lax.broadcasted_iota(jnp.int32, (blk, blk), 0)
    dst = lax.broadcasted_iota(jnp.int32, (blk, blk), 1)
    tri = ((src > dst) if suffix else (src <= dst)).astype(F32)
    carry = jnp.zeros((R, 1), F32)
    n = T // blk
    for i in (range(n - 1, -1, -1) if suffix else range(n)):
        xb = x_ref[:, i * blk:(i + 1) * blk]
        out_ref[:, i * blk:(i + 1) * blk] = jnp.dot(xb, tri, precision=HIGHEST, preferred_element_type=F32) + carry
        carry = carry + jnp.sum(xb, axis=1, keepdims=True)


def _cumsum_lanes(x, *, suffix):
    return pl.pallas_call(
        functools.partial(_cumsum_lanes_kernel, blk=256, suffix=suffix),
        out_shape=jax.ShapeDtypeStruct(x.shape, F32),
        compiler_params=_params(),
        name="cumsum_lanes",
    )(x)


def _fox_prompt_kernel(q_ref, kt_ref, vt_ref, ft_ref, o_ref, *, tq, tk, d):
    p = pl.program_id(1)
    T = q_ref.shape[1]
    assert tq == tk and d % 8 == 0
    nq = T // tq
    lane, mine = _pair_masks(d)
    q_fill = [((lane >= d) & (lane < d + 3)).astype(BF16), (lane < 3).astype(BF16)]
    den = [d, 0]
    sub = lax.broadcasted_iota(jnp.int32, (8, tk), 0)
    ones_blk = (sub == 0).astype(F32)
    pad_blk = jnp.zeros((d - 8, tk), F32)

    def stacked(x, blk, hh):
        return jnp.concatenate([x, blk, pad_blk] if hh == 0 else [blk, pad_blk, x], axis=0)

    diag_mask = (lax.broadcasted_iota(jnp.int32, (tk, tq), 0) <= lax.broadcasted_iota(jnp.int32, (tk, tq), 1))
    units = [(qi, hh) for qi in range(nq) for hh in range(2)]
    qs = {(qi, hh): jnp.where(mine[hh], q_ref[0, qi * tq:(qi + 1) * tq, :], q_fill[hh]) for qi, hh in units}
    m = {u: jnp.full((1, tq), -jnp.inf, F32) for u in units}
    acc = {u: jnp.zeros((2 * d, tq), F32) for u in units}
    for kv in range(nq):
        ks = slice(kv * tk, (kv + 1) * tk)
        k_rows, v_all = [], []
        for hh in range(2):
            f = ft_ref[0, pl.ds(2 * p + hh, 1), ks] * LOG2E
            hi = f.astype(BF16).astype(F32)
            mid = (f - hi).astype(BF16).astype(F32)
            f_blk = jnp.where(sub == 0, -hi, jnp.where(sub == 1, -mid, jnp.where(sub == 2, -(f - hi - mid), 0.0)))
            k_rows.append(stacked(kt_ref[0, hh, :, ks], f_blk, hh).T.astype(BF16))
            v_all.append(stacked(vt_ref[0, hh, :, ks], ones_blk, hh).astype(BF16))
        live = [u for u in units if u[0] >= kv]
        s = {u: _dot_nt(k_rows[u[1]], qs[u]) for u in live}
        s[kv, 0], s[kv, 1] = (jnp.where(diag_mask, s[kv, hh], -jnp.inf) for hh in range(2))
        m_new = {u: jnp.maximum(m[u], jnp.max(s[u], axis=0, keepdims=True)) for u in live}
        e = {u: jnp.exp2(s[u] - m_new[u]).astype(BF16) for u in live}
        for u in live:
            acc[u] = jnp.exp2(m[u] - m_new[u]) * acc[u] + _dot(v_all[u[1]], e[u])
            m[u] = m_new[u]
        out = [(acc[kv, hh] / acc[kv, hh][den[hh]:den[hh] + 1, :]).T for hh in range(2)]
        o_ref[0, kv * tq:(kv + 1) * tq, :] = jnp.where(mine[0], out[0], out[1]).astype(BF16)


def _fox_prompt(q, kt, vt, ft, *, tq, tk):
    B, T, w = q.shape
    _, nh, d, _ = kt.shape
    pair = 2 * d
    kv = pl.BlockSpec((1, 2, d, T), lambda b, p: (b, p, 0, 0))
    qo = pl.BlockSpec((1, T, pair), lambda b, p: (b, 0, p))
    return pl.pallas_call(
        functools.partial(_fox_prompt_kernel, tq=tq, tk=tk, d=d),
        grid=(B, nh // 2),
        in_specs=[qo, kv, kv, pl.BlockSpec((1, nh, T), lambda b, p: (b, 0, 0))],
        out_specs=qo,
        out_shape=jax.ShapeDtypeStruct((B, T, w), BF16),
        compiler_params=_params("parallel", "arbitrary"),
        name="fox_prompt",
    )(q, kt, vt, ft)


def _fox_sample_kernel(q_ref, kt_ref, vt_ref, ct_ref, kn_ref, vn_ref, lf_ref, o_ref, fqt_scr, *, hg, d, s_new):
    g = pl.program_id(1)
    nh = lf_ref.shape[2]
    tri = (lax.broadcasted_iota(jnp.int32, (s_new, s_new), 0)
           >= lax.broadcasted_iota(jnp.int32, (s_new, s_new), 1))
    eye = (lax.broadcasted_iota(jnp.int32, (nh, nh), 0) == lax.broadcasted_iota(jnp.int32, (nh, nh), 1)).astype(F32)
    fq = jnp.dot(tri.astype(F32), lf_ref[0], precision=HIGHEST, preferred_element_type=F32)
    fqt_scr[...] = lax.dot_general(eye, fq, NT, precision=HIGHEST, preferred_element_type=F32)
    outs = []
    for hh in range(hg):
        sl = slice(hh * d, (hh + 1) * d)
        head = g * hg + hh
        qh = q_ref[0, 0, :, sl]
        sc = _dot(qh, kt_ref[0, hh].astype(BF16)) + ct_ref[0, pl.ds(head, 1), :]
        sn = _dot_nt(qh, kn_ref[0, 0, :, sl]) - fqt_scr[pl.ds(head, 1), :]
        sn = jnp.where(tri, sn, -jnp.inf)
        m = jnp.maximum(jnp.max(sc, axis=-1, keepdims=True), jnp.max(sn, axis=-1, keepdims=True))
        ec = jnp.exp(sc - m)
        en = jnp.exp(sn - m)
        l = jnp.sum(ec, axis=-1, keepdims=True) + jnp.sum(en, axis=-1, keepdims=True)
        o = _dot_nt(ec.astype(BF16), vt_ref[0, hh].astype(BF16)) + _dot(en.astype(BF16), vn_ref[0, 0, :, sl])
        outs.append(o / l)
    o_ref[0, 0] = jnp.concatenate(outs, axis=-1).astype(BF16)


def _fox_sample(q, kt, vt, ct, kn, vn, lf, *, hg):
    B, ng, s_new, gw = q.shape
    _, nh, d, lc = kt.shape
    new = pl.BlockSpec((1, 1, s_new, gw), lambda b, g: (b, g, 0, 0))
    old = pl.BlockSpec((1, hg, d, lc), lambda b, g: (b, g, 0, 0))
    return pl.pallas_call(
        functools.partial(_fox_sample_kernel, hg=hg, d=d, s_new=s_new),
        grid=(B, ng),
        in_specs=[new, old, old, pl.BlockSpec((1, nh, lc), lambda b, g: (b, 0, 0)), new, new,
                  pl.BlockSpec((1, s_new, nh), lambda b, g: (b, 0, 0))],
        out_specs=new,
        out_shape=jax.ShapeDtypeStruct(q.shape, BF16),
        scratch_shapes=[pltpu.VMEM((nh, s_new), F32)],
        compiler_params=_params("parallel", "arbitrary"),
        name="fox_sample",
    )(q, kt, vt, ct, kn, vn, lf)


def _trunk(x, state, wts, dims, *, tm, gla_bb):
    B, T, D = x.shape
    n = B * T
    depth = wts["norm_g"].shape[0]
    h_a, dk_a, dv_a, h_b, d_b, h_c, d_c = dims
    a_qk, a_v, b_w, c_w = h_a * dk_a, h_a * dv_a, h_b * d_b, h_c * d_c
    xs = x.reshape(n, D)
    new_states = []
    pending = ()
    r3 = lambda t: t.reshape(B, T, t.shape[-1])
    for l in range(depth):
        i = l // 2
        g = wts["norm_g"][l]
        xs = _ffn(xs, g[0:1], wts["w_gu"], wts["w_down"], (l, 0), pending, tm=tm)
        if l % 2 == 0:
            prompt = state is None
            keep = min(BAND_CHUNKS * CHUNK, T)
            qa, ka, va, gate, la, bq, bk, bv, nk, nv = _even_in(
                xs, g[1:2], wts["even_main"][i], wts["even_r"][i], wts["alpha_up"][i], wts["b_alpha"][i],
                wts["even_kvt"][i] if prompt else None,
                B=B if prompt else 1, T=T if prompt else n, tm=tm, a_qk=a_qk, a_v=a_v, b_w=b_w, dk_a=dk_a, d_b=d_b,
                tail=keep if prompt else 0)
            s0 = jnp.zeros((B, h_a, dk_a, dv_a), F32) if prompt else state[l][0]
            o_a, s_fin = _gla(r3(qa), r3(ka), r3(va), r3(gate), r3(la), s0, wts["gla_g"][i],
                              bb=gla_bb, L=min(CHUNK, T))
            if prompt:
                o_b = _band_prompt(r3(bq), r3(bk), r3(bv), wts["band_bias"][i], nh=h_b, d=d_b)
                nk, nv = (t.reshape(B, h_b, d_b, keep).transpose(0, 3, 1, 2) for t in (nk, nv))
            else:
                ck, cv = state[l][1], state[l][2]
                o_b = _band_sample(r3(bq), ck.transpose(0, 2, 3, 1), cv.transpose(0, 2, 3, 1), r3(bk), r3(bv),
                                   wts["band_bias"][i], nh=h_b, d=d_b)
                nk, nv = (t.reshape(B, T, h_b, d_b) for t in (nk, nv))
            new_states.append((s_fin, nk, nv))
            pending = ((o_a.reshape(n, a_v), wts["even_out_a"][i]), (o_b.reshape(n, b_w), wts["even_out_b"][i]))
        elif state is None:
            q, kt, vt, lft = _odd_in_t(xs, g[1:2], wts["odd_q"][i], wts["odd_kvt"][i], wts["odd_ft"][i],
                                       wts["b_f_col"][i], B=B, T=T, tm=tm, nh=h_c, d_c=d_c)
            ft = _cumsum_lanes(lft.reshape(B * h_c, T), suffix=False).reshape(B, h_c, T)
            o = _fox_prompt(r3(q), kt, vt, ft, tq=512, tk=512)
            new_states.append((kt.transpose(0, 3, 1, 2), vt.transpose(0, 3, 1, 2), lft.transpose(0, 2, 1)))
            pending = ((o.reshape(n, c_w), wts["odd_out"][i]),)
        else:
            q, k, v, k32, v32, lf = _odd_in(xs, g[1:2], wts["odd_main"][i], wts["odd_f"][i], wts["b_f"][i],
                                             tm=tm, c_w=c_w, d_c=d_c, h_c=h_c)
            ck, cv, clf = state[l]
            lc = ck.shape[1]
            hg = 4
            ct = _cumsum_lanes(clf.astype(F32).transpose(0, 2, 1).reshape(B * h_c, lc), suffix=True)
            grp = lambda t: t.reshape(B, T, h_c // hg, hg * d_c).transpose(0, 2, 1, 3)
            o = _fox_sample(grp(q), ck.transpose(0, 2, 3, 1), cv.transpose(0, 2, 3, 1), ct.reshape(B, h_c, lc),
                            grp(k), grp(v), r3(lf), hg=hg)
            o = o.transpose(0, 2, 1, 3).reshape(n, c_w)
            new_states.append((r3(k32).reshape(B, T, h_c, d_c), r3(v32).reshape(B, T, h_c, d_c), r3(lf)))
            pending = ((o, wts["odd_out"][i]),)
        last = l == depth - 1
        xs = _ffn(xs, g[2:3], wts["w_gu"], wts["w_down"], (l, 1), pending,
                  wts["final_g"] if last else None, tm=tm)
        pending = ()
    return xs.reshape(B, T, D), new_states


def kernel(x_prompt, x_sample, state_gla, cache_band_k, cache_band_v, cache_fox_k, cache_fox_v, cache_fox_logf,
           norm_g, ffn_w_gu, ffn_w_down, even_w_in, gla_w_alpha_up, gla_b_alpha, gla_norm_g, band_rel_bias,
           even_w_out, odd_w_in, fox_b_f, odd_w_out, final_norm_g):
    depth = norm_g.shape[0]
    _, _, h_a, dk_a, dv_a = state_gla.shape
    h_b, d_b = cache_band_k.shape[-2:]
    h_c, d_c = cache_fox_k.shape[-2:]
    r_a = gla_w_alpha_up.shape[1]
    a_qk, a_v, b_w, c_w = h_a * dk_a, h_a * dv_a, h_b * d_b, h_c * d_c
    assert cache_band_k.shape[2] == BAND_CHUNKS * CHUNK and band_rel_bias.shape[-1] == 2 * MAX_REL + 1

    r0 = 2 * a_qk + 2 * a_v
    n_rel = 384
    rel_padded = jnp.pad(band_rel_bias, ((0, 0), (0, 0), (0, n_rel - band_rel_bias.shape[-1])))
    win = (BAND_CHUNKS + 1) * CHUNK
    odd_t = odd_w_in.transpose(0, 2, 1)
    wts = {
        "norm_g": norm_g,
        "w_gu": ffn_w_gu.astype(BF16),
        "w_down": ffn_w_down.astype(BF16),
        "even_main": jnp.concatenate([even_w_in[:, :, :r0], even_w_in[:, :, r0 + r_a:]], axis=-1).astype(BF16),
        "even_r": even_w_in[:, :, r0:r0 + r_a].astype(BF16),
        "even_kvt": even_w_in.transpose(0, 2, 1)[:, r0 + r_a + b_w:].astype(BF16),
        "alpha_up": gla_w_alpha_up.astype(BF16),
        "b_alpha": gla_b_alpha[:, None, :],
        "gla_g": gla_norm_g[:, None, :],
        "band_bias": [_bias_table(rel_padded[i], chunk=CHUNK, win=win, rows=2 * CHUNK)
                      for i in range(rel_padded.shape[0])],
        "even_out_a": even_w_out[:, :a_v].astype(BF16),
        "even_out_b": even_w_out[:, a_v:].astype(BF16),
        "odd_main": odd_w_in[:, :, :3 * c_w].astype(BF16),
        "odd_f": odd_w_in[:, :, 3 * c_w:].astype(BF16),
        "odd_q": odd_w_in[:, :, :c_w].astype(BF16),
        "odd_kvt": odd_t[:, c_w:3 * c_w].astype(BF16),
        "odd_ft": odd_t[:, 3 * c_w:].astype(BF16),
        "b_f": fox_b_f[:, None, :],
        "b_f_col": fox_b_f[:, :, None],
        "odd_out": odd_w_out.astype(BF16),
        "final_g": final_norm_g[None, :],
    }
    dims = (h_a, dk_a, dv_a, h_b, d_b, h_c, d_c)

    sample_states = []
    for l in range(depth):
        i = l // 2
        if l % 2 == 0:
            sample_states.append((state_gla[i], cache_band_k[i], cache_band_v[i]))
        else:
            sample_states.append((cache_fox_k[i], cache_fox_v[i], cache_fox_logf[i]))

    y_prompt, ns_p = _trunk(x_prompt, None, wts, dims, tm=512, gla_bb=8)
    y_sample, ns_s = _trunk(x_sample, sample_states, wts, dims, tm=256, gla_bb=8)

    ev = range(0, depth, 2)
    od = range(1, depth, 2)
    stack = lambda ns, layers, j: jnp.stack([ns[l][j] for l in layers])
    return (y_prompt, y_sample,
            stack(ns_p, ev, 0), stack(ns_p, ev, 1), stack(ns_p, ev, 2),
            stack(ns_p, od, 0), stack(ns_p, od, 1), stack(ns_p, od, 2),
            stack(ns_s, ev, 0), stack(ns_s, ev, 1), stack(ns_s, ev, 2),
            stack(ns_s, od, 0), stack(ns_s, od, 1), stack(ns_s, od, 2))
```

```python
import functools

import jax
import jax.numpy as jnp
from jax import lax
from jax.experimental import pallas as pl
from jax.experimental.pallas import tpu as pltpu

F32 = jnp.float32
BF16 = jnp.bfloat16
HIGHEST = lax.Precision.HIGHEST
NT = (((1,), (1,)), ((), ()))
TN = (((0,), (0,)), ((), ()))

EPS = 1e-6
CHUNK = 64
BAND_CHUNKS = 8
MAX_REL = 128
GATE_NORM_A = 16.0
LOG2E = 1.4426950408889634
VMEM_LIMIT_BYTES = 56 * 1024 * 1024


def _params(*sem):
    return pltpu.CompilerParams(dimension_semantics=sem, vmem_limit_bytes=VMEM_LIMIT_BYTES)


def _resident(shape):
    nd = len(shape)
    return pl.BlockSpec(shape, lambda *_: (0,) * nd, pipeline_mode=pl.Buffered(1))


def _rows(tm, width):
    return pl.BlockSpec((tm, width), lambda i: (i, 0))


def _rms(x, g):
    ms = jnp.mean(x * x, axis=-1, keepdims=True)
    return x * lax.rsqrt(ms + EPS) * g


def _log_sigmoid(x):
    return jnp.minimum(x, 0.0) - jnp.log1p(jnp.exp(-jnp.abs(x)))


def _dot(a, b):
    return jnp.dot(a, b, preferred_element_type=F32)


def _dot_nt(a, b):
    return lax.dot_general(a, b, NT, preferred_element_type=F32)


def _ffn_kernel(*refs, n_proj, has_final, casts, d_ff):
    it = iter(refs)
    x_ref = next(it)
    proj = [(next(it), next(it)) for _ in range(n_proj)]
    g_ref, wgu_ref, wd_ref = next(it), next(it), next(it)
    gf_ref = next(it) if has_final else None
    cast_src = [next(it) for _ in range(casts)]
    out_ref = next(it)
    for src in cast_src:
        next(it)[...] = src[...].astype(BF16)

    x = x_ref[...]
    for o_ref, w_ref in proj:
        x = x + _dot(o_ref[...], w_ref[...])
    xn = _rms(x, g_ref[...]).astype(BF16)
    h = _dot(xn, wgu_ref[...])
    a = (jax.nn.silu(h[:, :d_ff]) * h[:, d_ff:]).astype(BF16)
    y = x + 0.5 * _dot(a, wd_ref[...])
    if has_final:
        y = _rms(y, gf_ref[...])
    out_ref[...] = y


def _ffn(x, g, wgu, wd, proj=(), final_g=None, cast_next=None, *, tm):
    n, d = x.shape
    d_ff = wd.shape[0]
    steps = n // tm
    args, specs = [x], [_rows(tm, d)]
    for o, w in proj:
        args += [o, w]
        specs += [_rows(tm, o.shape[1]), _resident(w.shape)]
    args += [g, wgu, wd]
    specs += [_resident(g.shape), _resident(wgu.shape), _resident(wd.shape)]
    if final_g is not None:
        args.append(final_g)
        specs.append(_resident(final_g.shape))
    out_specs, out_shapes = [_rows(tm, d)], [jax.ShapeDtypeStruct((n, d), F32)]
    if cast_next is not None:
        src_gu, src_d, idx = cast_next
        for src, rows, per in ((src_gu, d, 1), (src_d, d_ff, 2)):
            slab = rows * per // steps
            assert rows * per % steps == 0 and slab % 16 == 0
            args.append(src)
            specs.append(pl.BlockSpec((None,) * len(idx) + (slab, src.shape[-1]),
                                      lambda i, per=per: tuple(idx) + (i // per, 0)))
            out_specs.append(pl.BlockSpec((slab, src.shape[-1]), lambda i, per=per: (i // per, 0)))
            out_shapes.append(jax.ShapeDtypeStruct((rows, src.shape[-1]), BF16))
    outs = pl.pallas_call(
        functools.partial(_ffn_kernel, n_proj=len(proj), has_final=final_g is not None,
                          casts=0 if cast_next is None else 2, d_ff=d_ff),
        grid=(steps,),
        in_specs=specs,
        out_specs=out_specs,
        out_shape=out_shapes,
        compiler_params=_params("arbitrary" if cast_next is not None else "parallel"),
        name="ffn",
    )(*args)
    return outs[0] if cast_next is None else outs


def _even_in_kernel(x_ref, g_ref, w_ref, wr_ref, wup_ref, ba_ref, *rest, a_qk, a_v, b_w, dk_a, d_b, tail):
    if tail:
        wkvt_ref, *rest = rest
    qa_ref, ka_ref, va_ref, gate_ref, la_ref, bq_ref, bk_ref, bv_ref, x1_ref, x2_ref = rest
    xn = _rms(x_ref[...], g_ref[...]).astype(BF16)
    h = _dot(xn, w_ref[...])
    o = 0
    qa_ref[...] = h[:, o:o + a_qk] * (dk_a ** -0.5)
    o += a_qk
    ka_ref[...] = h[:, o:o + a_qk]
    o += a_qk
    va_ref[...] = h[:, o:o + a_v].astype(BF16)
    o += a_v
    gate_ref[...] = jax.nn.silu(h[:, o:o + a_v])
    o += a_v
    bq_ref[...] = (h[:, o:o + b_w] * (LOG2E * d_b ** -0.5)).astype(BF16)
    o += b_w
    bk = h[:, o:o + b_w]
    o += b_w
    bv = h[:, o:o + b_w]
    bk_ref[...] = bk.astype(BF16)
    bv_ref[...] = bv.astype(BF16)
    if tail:
        @pl.when(pl.program_id(1) == pl.num_programs(1) - 1)
        def _():
            kvt = _dot_nt(wkvt_ref[...], xn[xn.shape[0] - tail:])
            x1_ref[0] = kvt[:b_w]
            x2_ref[0] = kvt[b_w:]
    else:
        x1_ref[...] = bk
        x2_ref[...] = bv
    r = _dot(xn, wr_ref[...]).astype(BF16)
    z = _dot(r, wup_ref[...]) + ba_ref[...]
    la_ref[...] = _log_sigmoid(z) / GATE_NORM_A


def _even_in(x, g, w_main, w_r, w_up, b_alpha, w_kvt=None, *, B, T, tm, a_qk, a_v, b_w, dk_a, d_b, tail):
    n, d = x.shape
    nb = T // tm
    tok = lambda w: pl.BlockSpec((tm, w), lambda b, i: (b * nb + i, 0))
    widths = [(a_qk, F32), (a_qk, F32), (a_v, BF16), (a_v, F32), (a_qk, F32), (b_w, BF16), (b_w, BF16), (b_w, BF16)]
    args = [x, g, w_main, w_r, w_up, b_alpha]
    if tail:
        assert tail <= tm
        args.append(w_kvt)
        extra_specs = [pl.BlockSpec((1, b_w, tail), lambda b, i: (b, 0, 0))] * 2
        extra_shapes = [jax.ShapeDtypeStruct((B, b_w, tail), F32)] * 2
    else:
        extra_specs = [tok(b_w)] * 2
        extra_shapes = [jax.ShapeDtypeStruct((n, b_w), F32)] * 2
    return pl.pallas_call(
        functools.partial(_even_in_kernel, a_qk=a_qk, a_v=a_v, b_w=b_w, dk_a=dk_a, d_b=d_b, tail=tail),
        grid=(B, nb),
        in_specs=[tok(d)] + [_resident(a.shape) for a in args[1:]],
        out_specs=[tok(w) for w, _ in widths] + extra_specs,
        out_shape=[jax.ShapeDtypeStruct((n, w), dt) for w, dt in widths] + extra_shapes,
        compiler_params=_params("parallel", "arbitrary"),
        name="even_in",
    )(*args)


def _odd_in_kernel(x_ref, g_ref, w_ref, wf_ref, bf_ref, q_ref, k_ref, v_ref, k32_ref, v32_ref, lf_ref, *, c_w, d_c):
    xn = _rms(x_ref[...], g_ref[...]).astype(BF16)
    h = _dot(xn, w_ref[...])
    q_ref[...] = (h[:, :c_w] * (d_c ** -0.5)).astype(BF16)
    k = h[:, c_w:2 * c_w]
    v = h[:, 2 * c_w:]
    k32_ref[...] = k
    v32_ref[...] = v
    k_ref[...] = k.astype(BF16)
    v_ref[...] = v.astype(BF16)
    lf_ref[...] = _log_sigmoid(_dot(xn, wf_ref[...]) + bf_ref[...])


def _odd_in(x, g, w_main, w_f, b_f, *, tm, c_w, d_c, h_c):
    n, d = x.shape
    outs = [(c_w, BF16), (c_w, BF16), (c_w, BF16), (c_w, F32), (c_w, F32), (h_c, F32)]
    return pl.pallas_call(
        functools.partial(_odd_in_kernel, c_w=c_w, d_c=d_c),
        grid=(n // tm,),
        in_specs=[_rows(tm, d), _resident(g.shape), _resident(w_main.shape), _resident(w_f.shape),
                  _resident(b_f.shape)],
        out_specs=[_rows(tm, w) for w, _ in outs],
        out_shape=[jax.ShapeDtypeStruct((n, w), dt) for w, dt in outs],
        compiler_params=_params("parallel"),
        name="odd_in",
    )(x, g, w_main, w_f, b_f)


def _odd_in_t_kernel(x_ref, g_ref, wq_ref, wkvt_ref, wft_ref, bf_ref, q_ref, kt_ref, vt_ref, lft_ref, *, nh, d_c):
    xn = _rms(x_ref[...], g_ref[...]).astype(BF16)
    tm = xn.shape[0]
    c_w = nh * d_c
    q_ref[...] = (_dot(xn, wq_ref[...]) * (LOG2E * d_c ** -0.5)).astype(BF16)
    ht = _dot_nt(wkvt_ref[...], xn)
    kt_ref[0] = ht[:c_w].reshape(nh, d_c, tm)
    vt_ref[0] = ht[c_w:].reshape(nh, d_c, tm)
    lft_ref[0] = _log_sigmoid(_dot_nt(wft_ref[...], xn) + bf_ref[...])


def _odd_in_t(x, g, wq, wkvt, wft, b_f_col, *, B, T, tm, nh, d_c):
    n, d = x.shape
    nb = T // tm
    c_w = nh * d_c
    tok = lambda w: pl.BlockSpec((tm, w), lambda b, i: (b * nb + i, 0))
    kv = pl.BlockSpec((1, nh, d_c, tm), lambda b, i: (b, 0, 0, i))
    return pl.pallas_call(
        functools.partial(_odd_in_t_kernel, nh=nh, d_c=d_c),
        grid=(B, nb),
        in_specs=[tok(d), _resident(g.shape), _resident(wq.shape), _resident(wkvt.shape), _resident(wft.shape),
                  _resident(b_f_col.shape)],
        out_specs=[tok(c_w), kv, kv, pl.BlockSpec((1, nh, tm), lambda b, i: (b, 0, i))],
        out_shape=[jax.ShapeDtypeStruct((n, c_w), BF16), jax.ShapeDtypeStruct((B, nh, d_c, T), F32),
                   jax.ShapeDtypeStruct((B, nh, d_c, T), F32), jax.ShapeDtypeStruct((B, nh, T), F32)],
        compiler_params=_params("parallel", "parallel"),
        name="odd_in_t",
    )(x, g, wq, wkvt, wft, b_f_col)


def _gla_kernel(q_ref, k_ref, v_ref, gate_ref, la_ref, s0_ref, g_ref, o_ref, sfin_ref, s_scr, *, bb, L, nh, dk, dv):
    c = pl.program_id(1)

    @pl.when(c == 0)
    def _():
        s_scr[...] = s0_ref[...]

    row = lax.broadcasted_iota(jnp.int32, (L, L), 0)
    col = lax.broadcasted_iota(jnp.int32, (L, L), 1)
    causal = row >= col
    tri = causal.astype(F32)
    eye = (lax.broadcasted_iota(jnp.int32, (dk, dk), 0) == lax.broadcasted_iota(jnp.int32, (dk, dk), 1))
    g = g_ref[...]
    bs = range(bb)
    units = [(b, h) for b in bs for h in range(nh)]
    sk = [slice(h * dk, (h + 1) * dk) for h in range(nh)]
    sv = [slice(h * dv, (h + 1) * dv) for h in range(nh)]
    w = nh * dk
    bcum_all = jnp.dot(tri, jnp.concatenate([la_ref[b] for b in bs], axis=1), precision=HIGHEST,
                       preferred_element_type=F32)
    bcum = [bcum_all[:, b * w:(b + 1) * w] for b in bs]
    b_last = [x[L - 1:L, :] for x in bcum]
    q_in = [(q_ref[b] * jnp.exp(bcum[b])).astype(BF16) for b in bs]
    k_in = [(k_ref[b] * jnp.exp(-bcum[b])).astype(BF16) for b in bs]
    k_st = [(k_ref[b] * jnp.exp(b_last[b] - bcum[b])).astype(BF16) for b in bs]
    a_tot = [jnp.exp(x) for x in b_last]
    v = [v_ref[b] for b in bs]
    att = [jnp.where(causal, _dot_nt(q_in[b][:, sk[h]], k_in[b][:, sk[h]]), 0.0).astype(BF16) for b, h in units]
    s_old = [s_scr[b, h] for b, h in units]
    o = [_dot(att[u], v[b][:, sv[h]]) + _dot(q_in[b][:, sk[h]], s_old[u].astype(BF16))
         for u, (b, h) in enumerate(units)]
    for u, (b, h) in enumerate(units):
        a_col = jnp.sum(jnp.where(eye, jnp.broadcast_to(a_tot[b][:, sk[h]], (dk, dk)), 0.0), axis=1, keepdims=True)
        s_scr[b, h] = a_col * s_old[u] + lax.dot_general(k_st[b][:, sk[h]], v[b][:, sv[h]], TN,
                                                          preferred_element_type=F32)
    o = [x * lax.rsqrt(jnp.mean(x * x, axis=-1, keepdims=True) + EPS) * g for x in o]
    for b in bs:
        o_ref[b] = (jnp.concatenate(o[b * nh:(b + 1) * nh], axis=-1) * gate_ref[b]).astype(BF16)

    @pl.when(c == pl.num_programs(1) - 1)
    def _():
        sfin_ref[...] = s_scr[...]


def _gla(q, k, v, gate, la, s0, g, *, bb, L):
    B, T, a_qk = q.shape
    a_v = v.shape[-1]
    _, nh, dk, dv = s0.shape
    n = T // L

    def tok(w):
        return pl.BlockSpec((bb, L, w), lambda i, c: (i, c, 0))

    st = pl.BlockSpec((bb, nh, dk, dv), lambda i, c: (i, 0, 0, 0))
    return pl.pallas_call(
        functools.partial(_gla_kernel, bb=bb, L=L, nh=nh, dk=dk, dv=dv),
        grid=(B // bb, n),
        in_specs=[tok(a_qk), tok(a_qk), tok(a_v), tok(a_v), tok(a_qk), st, pl.BlockSpec(g.shape, lambda i, c: (0, 0))],
        out_specs=[tok(a_v), st],
        out_shape=[jax.ShapeDtypeStruct((B, T, a_v), BF16), jax.ShapeDtypeStruct(s0.shape, F32)],
        scratch_shapes=[pltpu.VMEM((bb, nh, dk, dv), F32)],
        compiler_params=_params("parallel", "arbitrary"),
        name="gla",
    )(q, k, v, gate, la, s0, g)


def _bias_table_kernel(rel_ref, out_ref, *, chunk, win, n_rel):
    n_var, nh, rows, width = out_ref.shape
    back = win - chunk
    wide = width + back + rows
    assert wide % 128 == 0 and chunk & (chunk - 1) == 0
    u = lax.broadcasted_iota(jnp.int32, (n_rel, wide), 1)
    want = jnp.clip(back + rows - 1 - u, -MAX_REL, MAX_REL) + MAX_REL
    onehot = (lax.broadcasted_iota(jnp.int32, (n_rel, wide), 0) == want).astype(F32)
    vec = jnp.dot(rel_ref[...], onehot, precision=HIGHEST, preferred_element_type=F32)
    i = lax.broadcasted_iota(jnp.int32, (rows, width), 0)
    j = lax.broadcasted_iota(jnp.int32, (rows, width), 1)
    first = i - (i & (chunk - 1))
    for t in range(n_var):
        shift = back - t * rows if t < n_var - 1 else 0
        ok = (j + shift >= first) & (j + shift < first + win)
        for h in range(nh):
            x = jnp.broadcast_to(vec[h:h + 1, :], (rows, wide))
            x = pltpu.roll(x, (wide - (shift + rows - 1)) % wide, 1, stride=1, stride_axis=0)
            out_ref[t, h] = jnp.where(ok, x[:, :width] * LOG2E, -jnp.inf)


def _bias_table(rel_padded, *, chunk, win, rows):
    nh, n_rel = rel_padded.shape
    n_var = (win - chunk) // rows + 1
    return pl.pallas_call(
        functools.partial(_bias_table_kernel, chunk=chunk, win=win, n_rel=n_rel),
        out_shape=jax.ShapeDtypeStruct((n_var, nh, rows, win + rows - chunk), F32),
        compiler_params=_params(),
        name="band_bias_table",
    )(rel_padded)


def _pair_masks(d):
    lane = lax.broadcasted_iota(jnp.int32, (1, 2 * d), 1)
    return lane, [lane < d, lane >= d]


def _band_prompt_kernel(q_ref, k_ref, v_ref, bias_ref, o_ref, *, nh, d, back):
    n_var, _, tq, width = bias_ref.shape
    nsub = q_ref.shape[1] // tq
    lane, mine = _pair_masks(d)
    ones_lane = [d, 0]
    cols = [slice(p * 2 * d, (p + 1) * 2 * d) for p in range(nh // 2)]
    zero = jnp.zeros((), BF16)
    q, kw, vw, var = [], [], [], []
    for sb in range(nsub):
        c = pl.program_id(1) * nsub + sb
        start = pl.multiple_of(jnp.maximum(c * tq - back, 0), tq)
        var.append(jnp.minimum(c, n_var - 1))
        q.append(q_ref[0, sb * tq:(sb + 1) * tq, :])
        kw.append(k_ref[0, pl.ds(start, width), :])
        vw.append(v_ref[0, pl.ds(start, width), :])
    units = [(sb, p, hh) for sb in range(nsub) for p in range(nh // 2) for hh in range(2)]
    s = [_dot_nt(jnp.where(mine[hh], q[sb][:, cols[p]], zero), kw[sb][:, cols[p]]) + bias_ref[var[sb], 2 * p + hh]
         for sb, p, hh in units]
    m = [jnp.max(x, axis=-1, keepdims=True) for x in s]
    e = [jnp.exp2(x - mx).astype(BF16) for x, mx in zip(s, m)]
    r = [_dot(e[u], jnp.where(mine[hh], vw[sb][:, cols[p]], (lane == ones_lane[hh]).astype(BF16)))
         for u, (sb, p, hh) in enumerate(units)]
    r = [x / x[:, ones_lane[hh]:ones_lane[hh] + 1] for x, (sb, p, hh) in zip(r, units)]
    for sb in range(nsub):
        rs = r[sb * nh:(sb + 1) * nh]
        o_ref[0, sb * tq:(sb + 1) * tq, :] = jnp.concatenate(
            [jnp.where(mine[0], rs[2 * p], rs[2 * p + 1]) for p in range(nh // 2)], axis=-1).astype(BF16)


def _band_prompt(q, k, v, bias, *, nh, d, nsub=4):
    B, T, w = q.shape
    n_var, _, tq, width = bias.shape
    tq *= nsub
    assert T >= width and T % tq == 0
    kv = pl.BlockSpec((1, T, w), lambda b, c: (b, 0, 0))
    return pl.pallas_call(
        functools.partial(_band_prompt_kernel, nh=nh, d=d, back=width - tq // nsub),
        grid=(B, T // tq),
        in_specs=[pl.BlockSpec((1, tq, w), lambda b, c: (b, c, 0)), kv, kv, _resident(bias.shape)],
        out_specs=pl.BlockSpec((1, tq, w), lambda b, c: (b, c, 0)),
        out_shape=jax.ShapeDtypeStruct((B, T, w), BF16),
        compiler_params=_params("parallel", "arbitrary"),
        name="band_prompt",
    )(q, k, v, bias)


def _band_sample_kernel(q_ref, ckt_ref, cvt_ref, kn_ref, vn_ref, bias_ref, o_ref, *, nh, d, s_new):
    lc = ckt_ref.shape[3]
    q = q_ref[0]
    kn = kn_ref[0]
    vn = vn_ref[0]
    outs = []
    for h in range(nh):
        sl = slice(h * d, (h + 1) * d)
        bias = bias_ref[0, h]
        sc = _dot(q[:, sl], ckt_ref[0, h].astype(BF16)) + bias[:s_new, :lc]
        sn = _dot_nt(q[:, sl], kn[:, sl]) + bias[:s_new, lc:lc + s_new]
        m = jnp.maximum(jnp.max(sc, axis=-1, keepdims=True), jnp.max(sn, axis=-1, keepdims=True))
        ec = jnp.exp2(sc - m)
        en = jnp.exp2(sn - m)
        l = jnp.sum(ec, axis=-1, keepdims=True) + jnp.sum(en, axis=-1, keepdims=True)
        outs.append((_dot_nt(ec.astype(BF16), cvt_ref[0, h].astype(BF16)) + _dot(en.astype(BF16), vn[:, sl])) / l)
    o_ref[0] = jnp.concatenate(outs, axis=-1).astype(BF16)


def _band_sample(q, ckt, cvt, kn, vn, bias, *, nh, d):
    B, s_new, w = q.shape
    lc = ckt.shape[3]
    n_var, _, rows, width = bias.shape
    assert s_new <= CHUNK and lc == width - rows
    new = pl.BlockSpec((1, s_new, w), lambda b: (b, 0, 0))
    old = pl.BlockSpec((1, nh, d, lc), lambda b: (b, 0, 0, 0))
    return pl.pallas_call(
        functools.partial(_band_sample_kernel, nh=nh, d=d, s_new=s_new),
        grid=(B,),
        in_specs=[new, old, old, new, new, pl.BlockSpec((1, nh, rows, width), lambda b: (n_var - 1, 0, 0, 0))],
        out_specs=new,
        out_shape=jax.ShapeDtypeStruct((B, s_new, w), BF16),
        compiler_params=_params("parallel"),
        name="band_sample",
    )(q, ckt, cvt, kn, vn, bias)


def _cumsum_lanes_kernel(x_ref, out_ref, *, blk, suffix):
    R, T = x_ref.shape
    src = lax.broadcasted_iota(jnp.int32, (blk, blk), 0)
    dst = lax.broadcasted_iota(jnp.int32, (blk, blk), 1)
    tri = ((src > dst) if suffix else (src <= dst)).astype(F32)
    carry = jnp.zeros((R, 1), F32)
    n = T // blk
    for i in (range(n - 1, -1, -1) if suffix else range(n)):
        xb = x_ref[:, i * blk:(i + 1) * blk]
        out_ref[:, i * blk:(i + 1) * blk] = jnp.dot(xb, tri, precision=HIGHEST, preferred_element_type=F32) + carry
        carry = carry + jnp.sum(xb, axis=1, keepdims=True)


def _cumsum_lanes(x, *, suffix):
    return pl.pallas_call(
        functools.partial(_cumsum_lanes_kernel, blk=256, suffix=suffix),
        out_shape=jax.ShapeDtypeStruct(x.shape, F32),
        compiler_params=_params(),
        name="cumsum_lanes",
    )(x)


def _fox_prompt_kernel(q_ref, kt_ref, vt_ref, ft_ref, o_ref, *, tq, tk, d):
    p = pl.program_id(1)
    T = q_ref.shape[1]
    assert tq == tk and d % 8 == 0
    nq = T // tq
    lane, mine = _pair_masks(d)
    q_fill = [((lane >= d) & (lane < d + 3)).astype(BF16), (lane < 3).astype(BF16)]
    den = [d, 0]
    sub = lax.broadcasted_iota(jnp.int32, (8, tk), 0)
    ones_blk = (sub == 0).astype(F32)
    pad_blk = jnp.zeros((d - 8, tk), F32)

    def stacked(x, blk, hh):
        return jnp.concatenate([x, blk, pad_blk] if hh == 0 else [blk, pad_blk, x], axis=0)

    diag_mask = (lax.broadcasted_iota(jnp.int32, (tk, tq), 0) <= lax.broadcasted_iota(jnp.int32, (tk, tq), 1))
    npair = q_ref.shape[2] // (2 * d)
    heads = [(pr, hh) for pr in range(npair) for hh in range(2)]
    units = [(qi, pr, hh) for qi in range(nq) for pr, hh in heads]
    qs = {(qi, pr, hh): jnp.where(mine[hh], q_ref[0, qi * tq:(qi + 1) * tq, pr * 2 * d:(pr + 1) * 2 * d], q_fill[hh])
          for qi, pr, hh in units}
    m = {u: jnp.full((1, tq), -jnp.inf, F32) for u in units}
    acc = {u: jnp.zeros((2 * d, tq), F32) for u in units}
    for kv in range(nq):
        ks = slice(kv * tk, (kv + 1) * tk)
        k_rows, v_all = {}, {}
        for pr, hh in heads:
            f = ft_ref[0, pl.ds(2 * (p * npair + pr) + hh, 1), ks] * LOG2E
            hi = f.astype(BF16).astype(F32)
            mid = (f - hi).astype(BF16).astype(F32)
            f_blk = jnp.where(sub == 0, -hi, jnp.where(sub == 1, -mid, jnp.where(sub == 2, -(f - hi - mid), 0.0)))
            k_rows[pr, hh] = stacked(kt_ref[0, 2 * pr + hh, :, ks], f_blk, hh).T.astype(BF16)
            v_all[pr, hh] = stacked(vt_ref[0, 2 * pr + hh, :, ks], ones_blk, hh).astype(BF16)
        live = [u for u in units if u[0] >= kv]
        s = {u: _dot_nt(k_rows[u[1:]], qs[u]) for u in live}
        for u in live:
            if u[0] == kv:
                s[u] = jnp.where(diag_mask, s[u], -jnp.inf)
        m_new = {u: jnp.maximum(m[u], jnp.max(s[u], axis=0, keepdims=True)) for u in live}
        e = {u: jnp.exp2(s[u] - m_new[u]).astype(BF16) for u in live}
        for u in live:
            acc[u] = jnp.exp2(m[u] - m_new[u]) * acc[u] + _dot(v_all[u[1:]], e[u])
            m[u] = m_new[u]
        for pr in range(npair):
            out = [(acc[kv, pr, hh] / acc[kv, pr, hh][den[hh]:den[hh] + 1, :]).T for hh in range(2)]
            o_ref[0, kv * tq:(kv + 1) * tq, pr * 2 * d:(pr + 1) * 2 * d] = jnp.where(mine[0], out[0],
                                                                                   out[1]).astype(BF16)


def _fox_prompt(q, kt, vt, ft, *, tq, tk, npair=2):
    B, T, w = q.shape
    _, nh, d, _ = kt.shape
    kv = pl.BlockSpec((1, 2 * npair, d, T), lambda b, p: (b, p, 0, 0))
    qo = pl.BlockSpec((1, T, 2 * npair * d), lambda b, p: (b, 0, p))
    return pl.pallas_call(
        functools.partial(_fox_prompt_kernel, tq=tq, tk=tk, d=d),
        grid=(B, nh // (2 * npair)),
        in_specs=[qo, kv, kv, pl.BlockSpec((1, nh, T), lambda b, p: (b, 0, 0))],
        out_specs=qo,
        out_shape=jax.ShapeDtypeStruct((B, T, w), BF16),
        compiler_params=_params("parallel", "arbitrary"),
        name="fox_prompt",
    )(q, kt, vt, ft)


def _fox_sample_kernel(q_ref, kt_ref, vt_ref, ct_ref, kn_ref, vn_ref, lf_ref, o_ref, fqt_scr, *, hg, d, s_new):
    g = pl.program_id(1)
    nh = lf_ref.shape[2]
    tri = (lax.broadcasted_iota(jnp.int32, (s_new, s_new), 0)
           >= lax.broadcasted_iota(jnp.int32, (s_new, s_new), 1))
    eye = (lax.broadcasted_iota(jnp.int32, (nh, nh), 0) == lax.broadcasted_iota(jnp.int32, (nh, nh), 1)).astype(F32)
    fq = jnp.dot(tri.astype(F32), lf_ref[0], precision=HIGHEST, preferred_element_type=F32)
    fqt_scr[...] = lax.dot_general(eye, fq, NT, precision=HIGHEST, preferred_element_type=F32)
    outs = []
    for hh in range(hg):
        sl = slice(hh * d, (hh + 1) * d)
        head = g * hg + hh
        qh = q_ref[0, 0, :, sl]
        sc = _dot(qh, kt_ref[0, hh].astype(BF16)) + ct_ref[0, pl.ds(head, 1), :]
        sn = _dot_nt(qh, kn_ref[0, 0, :, sl]) - fqt_scr[pl.ds(head, 1), :]
        sn = jnp.where(tri, sn, -jnp.inf)
        m = jnp.maximum(jnp.max(sc, axis=-1, keepdims=True), jnp.max(sn, axis=-1, keepdims=True))
        ec = jnp.exp(sc - m)
        en = jnp.exp(sn - m)
        l = jnp.sum(ec, axis=-1, keepdims=True) + jnp.sum(en, axis=-1, keepdims=True)
        o = _dot_nt(ec.astype(BF16), vt_ref[0, hh].astype(BF16)) + _dot(en.astype(BF16), vn_ref[0, 0, :, sl])
        outs.append(o / l)
    o_ref[0, 0] = jnp.concatenate(outs, axis=-1).astype(BF16)


def _fox_sample(q, kt, vt, ct, kn, vn, lf, *, hg):
    B, ng, s_new, gw = q.shape
    _, nh, d, lc = kt.shape
    new = pl.BlockSpec((1, 1, s_new, gw), lambda b, g: (b, g, 0, 0))
    old = pl.BlockSpec((1, hg, d, lc), lambda b, g: (b, g, 0, 0))
    return pl.pallas_call(
        functools.partial(_fox_sample_kernel, hg=hg, d=d, s_new=s_new),
        grid=(B, ng),
        in_specs=[new, old, old, pl.BlockSpec((1, nh, lc), lambda b, g: (b, 0, 0)), new, new,
                  pl.BlockSpec((1, s_new, nh), lambda b, g: (b, 0, 0))],
        out_specs=new,
        out_shape=jax.ShapeDtypeStruct(q.shape, BF16),
        scratch_shapes=[pltpu.VMEM((nh, s_new), F32)],
        compiler_params=_params("parallel", "arbitrary"),
        name="fox_sample",
    )(q, kt, vt, ct, kn, vn, lf)


def _trunk(x, state, wts, dims, *, tm, gla_bb):
    B, T, D = x.shape
    n = B * T
    depth = wts["norm_g"].shape[0]
    h_a, dk_a, dv_a, h_b, d_b, h_c, d_c = dims
    a_qk, a_v, b_w, c_w = h_a * dk_a, h_a * dv_a, h_b * d_b, h_c * d_c
    xs = x.reshape(n, D)
    new_states = []
    pending = ()
    r3 = lambda t: t.reshape(B, T, t.shape[-1])
    ffn_w = wts["ffn_bf16"]

    def ffn(xs, g, key, proj, final_g=None):
        nxt = (key[0], 1) if key[1] == 0 else (key[0] + 1, 0)
        if state is not None or nxt[0] >= depth:
            return _ffn(xs, g, *ffn_w[key], proj, final_g, tm=tm)
        xs, *ffn_w[nxt] = _ffn(xs, g, *ffn_w[key], proj, final_g, (wts["w_gu"], wts["w_down"], nxt), tm=tm)
        return xs

    for l in range(depth):
        i = l // 2
        g = wts["norm_g"][l]
        xs = ffn(xs, g[0:1], (l, 0), pending)
        if l % 2 == 0:
            prompt = state is None
            keep = min(BAND_CHUNKS * CHUNK, T)
            qa, ka, va, gate, la, bq, bk, bv, nk, nv = _even_in(
                xs, g[1:2], wts["even_main"][i], wts["even_r"][i], wts["alpha_up"][i], wts["b_alpha"][i],
                wts["even_kvt"][i] if prompt else None,
                B=B if prompt else 1, T=T if prompt else n, tm=tm, a_qk=a_qk, a_v=a_v, b_w=b_w, dk_a=dk_a, d_b=d_b,
                tail=keep if prompt else 0)
            s0 = jnp.zeros((B, h_a, dk_a, dv_a), F32) if prompt else state[l][0]
            o_a, s_fin = _gla(r3(qa), r3(ka), r3(va), r3(gate), r3(la), s0, wts["gla_g"][i],
                              bb=gla_bb, L=min(CHUNK, T))
            if prompt:
                o_b = _band_prompt(r3(bq), r3(bk), r3(bv), wts["band_bias"][i], nh=h_b, d=d_b)
                nk, nv = (t.reshape(B, h_b, d_b, keep).transpose(0, 3, 1, 2) for t in (nk, nv))
            else:
                ck, cv = state[l][1], state[l][2]
                o_b = _band_sample(r3(bq), ck.transpose(0, 2, 3, 1), cv.transpose(0, 2, 3, 1), r3(bk), r3(bv),
                                   wts["band_bias"][i], nh=h_b, d=d_b)
                nk, nv = (t.reshape(B, T, h_b, d_b) for t in (nk, nv))
            new_states.append((s_fin, nk, nv))
            pending = ((o_a.reshape(n, a_v), wts["even_out_a"][i]), (o_b.reshape(n, b_w), wts["even_out_b"][i]))
        elif state is None:
            q, kt, vt, lft = _odd_in_t(xs, g[1:2], wts["odd_q"][i], wts["odd_kvt"][i], wts["odd_ft"][i],
                                       wts["b_f_col"][i], B=B, T=T, tm=tm, nh=h_c, d_c=d_c)
            ft = _cumsum_lanes(lft.reshape(B * h_c, T), suffix=False).reshape(B, h_c, T)
            o = _fox_prompt(r3(q), kt, vt, ft, tq=512, tk=512)
            new_states.append((kt.transpose(0, 3, 1, 2), vt.transpose(0, 3, 1, 2), lft.transpose(0, 2, 1)))
            pending = ((o.reshape(n, c_w), wts["odd_out"][i]),)
        else:
            q, k, v, k32, v32, lf = _odd_in(xs, g[1:2], wts["odd_main"][i], wts["odd_f"][i], wts["b_f"][i],
                                             tm=tm, c_w=c_w, d_c=d_c, h_c=h_c)
            ck, cv, clf = state[l]
            lc = ck.shape[1]
            hg = 4
            ct = _cumsum_lanes(clf.astype(F32).transpose(0, 2, 1).reshape(B * h_c, lc), suffix=True)
            grp = lambda t: t.reshape(B, T, h_c // hg, hg * d_c).transpose(0, 2, 1, 3)
            o = _fox_sample(grp(q), ck.transpose(0, 2, 3, 1), cv.transpose(0, 2, 3, 1), ct.reshape(B, h_c, lc),
                            grp(k), grp(v), r3(lf), hg=hg)
            o = o.transpose(0, 2, 1, 3).reshape(n, c_w)
            new_states.append((r3(k32).reshape(B, T, h_c, d_c), r3(v32).reshape(B, T, h_c, d_c), r3(lf)))
            pending = ((o, wts["odd_out"][i]),)
        last = l == depth - 1
        xs = ffn(xs, g[2:3], (l, 1), pending, wts["final_g"] if last else None)
        pending = ()
    return xs.reshape(B, T, D), new_states


def kernel(x_prompt, x_sample, state_gla, cache_band_k, cache_band_v, cache_fox_k, cache_fox_v, cache_fox_logf,
           norm_g, ffn_w_gu, ffn_w_down, even_w_in, gla_w_alpha_up, gla_b_alpha, gla_norm_g, band_rel_bias,
           even_w_out, odd_w_in, fox_b_f, odd_w_out, final_norm_g):
    depth = norm_g.shape[0]
    _, _, h_a, dk_a, dv_a = state_gla.shape
    h_b, d_b = cache_band_k.shape[-2:]
    h_c, d_c = cache_fox_k.shape[-2:]
    r_a = gla_w_alpha_up.shape[1]
    a_qk, a_v, b_w, c_w = h_a * dk_a, h_a * dv_a, h_b * d_b, h_c * d_c
    assert cache_band_k.shape[2] == BAND_CHUNKS * CHUNK and band_rel_bias.shape[-1] == 2 * MAX_REL + 1

    r0 = 2 * a_qk + 2 * a_v
    n_rel = 384
    rel_padded = jnp.pad(band_rel_bias, ((0, 0), (0, 0), (0, n_rel - band_rel_bias.shape[-1])))
    win = (BAND_CHUNKS + 1) * CHUNK
    odd_t = odd_w_in.transpose(0, 2, 1)
    wts = {
        "norm_g": norm_g,
        "w_gu": ffn_w_gu,
        "w_down": ffn_w_down,
        "ffn_bf16": {(0, 0): [ffn_w_gu[0, 0].astype(BF16), ffn_w_down[0, 0].astype(BF16)]},
        "even_main": jnp.concatenate([even_w_in[:, :, :r0], even_w_in[:, :, r0 + r_a:]], axis=-1).astype(BF16),
        "even_r": even_w_in[:, :, r0:r0 + r_a].astype(BF16),
        "even_kvt": even_w_in.transpose(0, 2, 1)[:, r0 + r_a + b_w:].astype(BF16),
        "alpha_up": gla_w_alpha_up.astype(BF16),
        "b_alpha": gla_b_alpha[:, None, :],
        "gla_g": gla_norm_g[:, None, :],
        "band_bias": [_bias_table(rel_padded[i], chunk=CHUNK, win=win, rows=2 * CHUNK)
                      for i in range(rel_padded.shape[0])],
        "even_out_a": even_w_out[:, :a_v].astype(BF16),
        "even_out_b": even_w_out[:, a_v:].astype(BF16),
        "odd_main": odd_w_in[:, :, :3 * c_w].astype(BF16),
        "odd_f": odd_w_in[:, :, 3 * c_w:].astype(BF16),
        "odd_q": odd_w_in[:, :, :c_w].astype(BF16),
        "odd_kvt": odd_t[:, c_w:3 * c_w].astype(BF16),
        "odd_ft": odd_t[:, 3 * c_w:].astype(BF16),
        "b_f": fox_b_f[:, None, :],
        "b_f_col": fox_b_f[:, :, None],
        "odd_out": odd_w_out.astype(BF16),
        "final_g": final_norm_g[None, :],
    }
    dims = (h_a, dk_a, dv_a, h_b, d_b, h_c, d_c)

    sample_states = []
    for l in range(depth):
        i = l // 2
        if l % 2 == 0:
            sample_states.append((state_gla[i], cache_band_k[i], cache_band_v[i]))
        else:
            sample_states.append((cache_fox_k[i], cache_fox_v[i], cache_fox_logf[i]))

    y_prompt, ns_p = _trunk(x_prompt, None, wts, dims, tm=512, gla_bb=8)
    y_sample, ns_s = _trunk(x_sample, sample_states, wts, dims, tm=256, gla_bb=8)

    ev = range(0, depth, 2)
    od = range(1, depth, 2)
    stack = lambda ns, layers, j: jnp.stack([ns[l][j] for l in layers])
    return (y_prompt, y_sample,
            stack(ns_p, ev, 0), stack(ns_p, ev, 1), stack(ns_p, ev, 2),
            stack(ns_p, od, 0), stack(ns_p, od, 1), stack(ns_p, od, 2),
            stack(ns_s, ev, 0), stack(ns_s, ev, 1), stack(ns_s, ev, 2),
            stack(ns_s, od, 0), stack(ns_s, od, 1), stack(ns_s, od, 2))
```

```python
import functools

import jax
import jax.numpy as jnp
from jax import lax
from jax.experimental import pallas as pl
from jax.experimental.pallas import tpu as pltpu

F32 = jnp.float32
BF16 = jnp.bfloat16
HIGHEST = lax.Precision.HIGHEST
NT = (((1,), (1,)), ((), ()))
TN = (((0,), (0,)), ((), ()))

EPS = 1e-6
CHUNK = 64
BAND_CHUNKS = 8
MAX_REL = 128
GATE_NORM_A = 16.0
LOG2E = 1.4426950408889634
VMEM_LIMIT_BYTES = 56 * 1024 * 1024


def _params(*sem):
    return pltpu.CompilerParams(dimension_semantics=sem, vmem_limit_bytes=VMEM_LIMIT_BYTES)


def _resident(shape):
    nd = len(shape)
    return pl.BlockSpec(shape, lambda *_: (0,) * nd, pipeline_mode=pl.Buffered(1))


def _rows(tm, width):
    return pl.BlockSpec((tm, width), lambda i: (i, 0))


def _rms(x, g):
    ms = jnp.mean(x * x, axis=-1, keepdims=True)
    return x * lax.rsqrt(ms + EPS) * g


def _log_sigmoid(x):
    return jnp.minimum(x, 0.0) - jnp.log1p(jnp.exp(-jnp.abs(x)))


def _dot(a, b):
    return jnp.dot(a, b, preferred_element_type=F32)


def _dot_nt(a, b):
    return lax.dot_general(a, b, NT, preferred_element_type=F32)


def _ffn_kernel(*refs, n_proj, has_final, casts, d_ff):
    it = iter(refs)
    x_ref = next(it)
    proj = [(next(it), next(it)) for _ in range(n_proj)]
    g_ref, wgu_ref, wd_ref = next(it), next(it), next(it)
    gf_ref = next(it) if has_final else None
    cast_src = [next(it) for _ in range(casts)]
    out_ref = next(it)
    for src in cast_src:
        next(it)[...] = src[...].astype(BF16)

    x = x_ref[...]
    for o_ref, w_ref in proj:
        x = x + _dot(o_ref[...], w_ref[...])
    xn = _rms(x, g_ref[...]).astype(BF16)
    h = _dot(xn, wgu_ref[...])
    a = (jax.nn.silu(h[:, :d_ff]) * h[:, d_ff:]).astype(BF16)
    y = x + 0.5 * _dot(a, wd_ref[...])
    if has_final:
        y = _rms(y, gf_ref[...])
    out_ref[...] = y


def _ffn(x, g, wgu, wd, proj=(), final_g=None, cast_next=None, *, tm):
    n, d = x.shape
    d_ff = wd.shape[0]
    steps = n // tm
    args, specs = [x], [_rows(tm, d)]
    for o, w in proj:
        args += [o, w]
        specs += [_rows(tm, o.shape[1]), _resident(w.shape)]
    args += [g, wgu, wd]
    specs += [_resident(g.shape), _resident(wgu.shape), _resident(wd.shape)]
    if final_g is not None:
        args.append(final_g)
        specs.append(_resident(final_g.shape))
    out_specs, out_shapes = [_rows(tm, d)], [jax.ShapeDtypeStruct((n, d), F32)]
    if cast_next is not None:
        src_gu, src_d, idx = cast_next
        for src, rows, per in ((src_gu, d, 1), (src_d, d_ff, 2)):
            slab = rows * per // steps
            assert rows * per % steps == 0 and slab % 16 == 0
            args.append(src)
            specs.append(pl.BlockSpec((None,) * len(idx) + (slab, src.shape[-1]),
                                      lambda i, per=per: tuple(idx) + (i // per, 0)))
            out_specs.append(pl.BlockSpec((slab, src.shape[-1]), lambda i, per=per: (i // per, 0)))
            out_shapes.append(jax.ShapeDtypeStruct((rows, src.shape[-1]), BF16))
    outs = pl.pallas_call(
        functools.partial(_ffn_kernel, n_proj=len(proj), has_final=final_g is not None,
                          casts=0 if cast_next is None else 2, d_ff=d_ff),
        grid=(steps,),
        in_specs=specs,
        out_specs=out_specs,
        out_shape=out_shapes,
        compiler_params=_params("arbitrary" if cast_next is not None else "parallel"),
        name="ffn",
    )(*args)
    return outs[0] if cast_next is None else outs


def _ffn_streamed_kernel(*refs, n_proj, has_final):
    it = iter(refs)
    x_ref = next(it)
    proj = [(next(it), next(it)) for _ in range(n_proj)]
    g_ref, wg_ref, wu_ref, wd_ref = next(it), next(it), next(it), next(it)
    gf_ref = next(it) if has_final else None
    out_ref, x_scr, xn_scr, acc_scr = next(it), next(it), next(it), next(it)
    c = pl.program_id(0)

    @pl.when(c == 0)
    def _():
        x = x_ref[...]
        for o_ref, w_ref in proj:
            x = x + _dot(o_ref[...], w_ref[...])
        x_scr[...] = x
        xn_scr[...] = _rms(x, g_ref[...]).astype(BF16)
        acc_scr[...] = jnp.zeros(acc_scr.shape, F32)

    xn = xn_scr[...]
    a = (jax.nn.silu(_dot(xn, wg_ref[...])) * _dot(xn, wu_ref[...])).astype(BF16)
    acc_scr[...] += _dot(a, wd_ref[...])

    @pl.when(c == pl.num_programs(0) - 1)
    def _():
        y = x_scr[...] + 0.5 * acc_scr[...]
        if has_final:
            y = _rms(y, gf_ref[...])
        out_ref[...] = y


def _ffn_streamed(x, g, wgu, wd, proj=(), final_g=None, *, n_chunks):
    n, d = x.shape
    d_ff = wd.shape[0]
    ck = d_ff // n_chunks
    assert d_ff % n_chunks == 0 and ck % 128 == 0
    whole = lambda a: pl.BlockSpec(a.shape, lambda c: (0,) * a.ndim)
    args, specs = [x], [whole(x)]
    for o, w in proj:
        args += [o, w]
        specs += [whole(o), whole(w)]
    args += [g, wgu, wgu, wd]
    specs += [whole(g), pl.BlockSpec((d, ck), lambda c: (0, c)), pl.BlockSpec((d, ck), lambda c: (0, n_chunks + c)),
              pl.BlockSpec((ck, d), lambda c: (c, 0))]
    if final_g is not None:
        args.append(final_g)
        specs.append(whole(final_g))
    return pl.pallas_call(
        functools.partial(_ffn_streamed_kernel, n_proj=len(proj), has_final=final_g is not None),
        grid=(n_chunks,),
        in_specs=specs,
        out_specs=pl.BlockSpec((n, d), lambda c: (0, 0)),
        out_shape=jax.ShapeDtypeStruct((n, d), F32),
        scratch_shapes=[pltpu.VMEM((n, d), F32), pltpu.VMEM((n, d), BF16), pltpu.VMEM((n, d), F32)],
        compiler_params=_params("arbitrary"),
        name="ffn_streamed",
    )(*args)


def _even_in_kernel(x_ref, g_ref, w_ref, wr_ref, wup_ref, ba_ref,
                    qa_ref, ka_ref, va_ref, gate_ref, la_ref, bq_ref, bk_ref, bv_ref, *kv32_refs,
                    a_qk, a_v, b_w, dk_a, d_b):
    xn = _rms(x_ref[...], g_ref[...]).astype(BF16)
    h = _dot(xn, w_ref[...])
    o = 0
    qa_ref[...] = h[:, o:o + a_qk] * (dk_a ** -0.5)
    o += a_qk
    ka_ref[...] = h[:, o:o + a_qk]
    o += a_qk
    va_ref[...] = h[:, o:o + a_v].astype(BF16)
    o += a_v
    gate_ref[...] = jax.nn.silu(h[:, o:o + a_v])
    o += a_v
    bq_ref[...] = (h[:, o:o + b_w] * (LOG2E * d_b ** -0.5)).astype(BF16)
    o += b_w
    bk = h[:, o:o + b_w]
    o += b_w
    bv = h[:, o:o + b_w]
    bk_ref[...] = bk.astype(BF16)
    bv_ref[...] = bv.astype(BF16)
    if kv32_refs:
        kv32_refs[0][...] = bk
        kv32_refs[1][...] = bv
    r = _dot(xn, wr_ref[...]).astype(BF16)
    z = _dot(r, wup_ref[...]) + ba_ref[...]
    la_ref[...] = _log_sigmoid(z) / GATE_NORM_A


def _even_in(x, g, w_main, w_r, w_up, b_alpha, *, tm, a_qk, a_v, b_w, dk_a, d_b, kv32):
    n, d = x.shape
    widths = [(a_qk, F32), (a_qk, F32), (a_v, BF16), (a_v, F32), (a_qk, F32), (b_w, BF16), (b_w, BF16), (b_w, BF16)]
    if kv32:
        widths += [(b_w, F32), (b_w, F32)]
    args = [x, g, w_main, w_r, w_up, b_alpha]
    return pl.pallas_call(
        functools.partial(_even_in_kernel, a_qk=a_qk, a_v=a_v, b_w=b_w, dk_a=dk_a, d_b=d_b),
        grid=(n // tm,),
        in_specs=[_rows(tm, d)] + [_resident(a.shape) for a in args[1:]],
        out_specs=[_rows(tm, w) for w, _ in widths],
        out_shape=[jax.ShapeDtypeStruct((n, w), dt) for w, dt in widths],
        compiler_params=_params("parallel"),
        name="even_in",
    )(*args)


def _band_tail_kernel(x_ref, g_ref, wkvt_ref, kt_ref, vt_ref):
    xn = _rms(x_ref[...], g_ref[...]).astype(BF16)
    kvt = _dot_nt(wkvt_ref[...], xn)
    b_w = kt_ref.shape[1]
    kt_ref[0] = kvt[:b_w]
    vt_ref[0] = kvt[b_w:]


def _band_tail(x, g, w_kvt, *, B, T, keep):
    n, d = x.shape
    b_w = w_kvt.shape[0] // 2
    assert T % keep == 0
    nb = T // keep
    out = pl.BlockSpec((1, b_w, keep), lambda b: (b, 0, 0))
    return pl.pallas_call(
        _band_tail_kernel,
        grid=(B,),
        in_specs=[pl.BlockSpec((keep, d), lambda b: (b * nb + nb - 1, 0)), _resident(g.shape), _resident(w_kvt.shape)],
        out_specs=[out, out],
        out_shape=[jax.ShapeDtypeStruct((B, b_w, keep), F32)] * 2,
        compiler_params=_params("parallel"),
        name="band_tail",
    )(x, g, w_kvt)


def _odd_in_kernel(x_ref, g_ref, w_ref, wf_ref, bf_ref, q_ref, k_ref, v_ref, k32_ref, v32_ref, lf_ref, *, c_w, d_c):
    xn = _rms(x_ref[...], g_ref[...]).astype(BF16)
    h = _dot(xn, w_ref[...])
    q_ref[...] = (h[:, :c_w] * (d_c ** -0.5)).astype(BF16)
    k = h[:, c_w:2 * c_w]
    v = h[:, 2 * c_w:]
    k32_ref[...] = k
    v32_ref[...] = v
    k_ref[...] = k.astype(BF16)
    v_ref[...] = v.astype(BF16)
    lf_ref[...] = _log_sigmoid(_dot(xn, wf_ref[...]) + bf_ref[...])


def _odd_in(x, g, w_main, w_f, b_f, *, tm, c_w, d_c, h_c):
    n, d = x.shape
    outs = [(c_w, BF16), (c_w, BF16), (c_w, BF16), (c_w, F32), (c_w, F32), (h_c, F32)]
    return pl.pallas_call(
        functools.partial(_odd_in_kernel, c_w=c_w, d_c=d_c),
        grid=(n // tm,),
        in_specs=[_rows(tm, d), _resident(g.shape), _resident(w_main.shape), _resident(w_f.shape),
                  _resident(b_f.shape)],
        out_specs=[_rows(tm, w) for w, _ in outs],
        out_shape=[jax.ShapeDtypeStruct((n, w), dt) for w, dt in outs],
        compiler_params=_params("parallel"),
        name="odd_in",
    )(x, g, w_main, w_f, b_f)


def _odd_in_t_kernel(x_ref, g_ref, wq_ref, wkvt_ref, wft_ref, bf_ref, q_ref, kt_ref, vt_ref, lft_ref, *, nh, d_c):
    xn = _rms(x_ref[...], g_ref[...]).astype(BF16)
    tm = xn.shape[0]
    c_w = nh * d_c
    q_ref[...] = (_dot(xn, wq_ref[...]) * (LOG2E * d_c ** -0.5)).astype(BF16)
    ht = _dot_nt(wkvt_ref[...], xn)
    kt_ref[0] = ht[:c_w].reshape(nh, d_c, tm)
    vt_ref[0] = ht[c_w:].reshape(nh, d_c, tm)
    lft_ref[0] = _log_sigmoid(_dot_nt(wft_ref[...], xn) + bf_ref[...])


def _odd_in_t(x, g, wq, wkvt, wft, b_f_col, *, B, T, tm, nh, d_c):
    n, d = x.shape
    nb = T // tm
    c_w = nh * d_c
    tok = lambda w: pl.BlockSpec((tm, w), lambda b, i: (b * nb + i, 0))
    kv = pl.BlockSpec((1, nh, d_c, tm), lambda b, i: (b, 0, 0, i))
    return pl.pallas_call(
        functools.partial(_odd_in_t_kernel, nh=nh, d_c=d_c),
        grid=(B, nb),
        in_specs=[tok(d), _resident(g.shape), _resident(wq.shape), _resident(wkvt.shape), _resident(wft.shape),
                  _resident(b_f_col.shape)],
        out_specs=[tok(c_w), kv, kv, pl.BlockSpec((1, nh, tm), lambda b, i: (b, 0, i))],
        out_shape=[jax.ShapeDtypeStruct((n, c_w), BF16), jax.ShapeDtypeStruct((B, nh, d_c, T), F32),
                   jax.ShapeDtypeStruct((B, nh, d_c, T), F32), jax.ShapeDtypeStruct((B, nh, T), F32)],
        compiler_params=_params("parallel", "parallel"),
        name="odd_in_t",
    )(x, g, wq, wkvt, wft, b_f_col)


def _gla_kernel(q_ref, k_ref, v_ref, gate_ref, la_ref, s0_ref, g_ref, o_ref, sfin_ref, s_scr, *, bb, L, nh, dk, dv):
    c = pl.program_id(1)

    @pl.when(c == 0)
    def _():
        s_scr[...] = s0_ref[...]

    row = lax.broadcasted_iota(jnp.int32, (L, L), 0)
    col = lax.broadcasted_iota(jnp.int32, (L, L), 1)
    causal = row >= col
    tri = causal.astype(F32)
    eye = (lax.broadcasted_iota(jnp.int32, (dk, dk), 0) == lax.broadcasted_iota(jnp.int32, (dk, dk), 1))
    g = g_ref[...]
    bs = range(bb)
    units = [(b, h) for b in bs for h in range(nh)]
    sk = [slice(h * dk, (h + 1) * dk) for h in range(nh)]
    sv = [slice(h * dv, (h + 1) * dv) for h in range(nh)]
    w = nh * dk
    state = [s_scr[b, h] for b, h in units]
    for cc in range(q_ref.shape[1] // L):
        rows = slice(cc * L, (cc + 1) * L)
        bcum_all = jnp.dot(tri, jnp.concatenate([la_ref[b, rows] for b in bs], axis=1), precision=HIGHEST,
                           preferred_element_type=F32)
        bcum = [bcum_all[:, b * w:(b + 1) * w] for b in bs]
        b_last = [x[L - 1:L, :] for x in bcum]
        q_in = [(q_ref[b, rows] * jnp.exp(bcum[b])).astype(BF16) for b in bs]
        k_in = [(k_ref[b, rows] * jnp.exp(-bcum[b])).astype(BF16) for b in bs]
        k_st = [(k_ref[b, rows] * jnp.exp(b_last[b] - bcum[b])).astype(BF16) for b in bs]
        a_tot = [jnp.exp(x) for x in b_last]
        v = [v_ref[b, rows] for b in bs]
        att = [jnp.where(causal, _dot_nt(q_in[b][:, sk[h]], k_in[b][:, sk[h]]), 0.0).astype(BF16) for b, h in units]
        o = [_dot(att[u], v[b][:, sv[h]]) + _dot(q_in[b][:, sk[h]], state[u].astype(BF16))
             for u, (b, h) in enumerate(units)]
        for u, (b, h) in enumerate(units):
            a_col = jnp.sum(jnp.where(eye, jnp.broadcast_to(a_tot[b][:, sk[h]], (dk, dk)), 0.0), axis=1,
                            keepdims=True)
            state[u] = a_col * state[u] + lax.dot_general(k_st[b][:, sk[h]], v[b][:, sv[h]], TN,
                                                          preferred_element_type=F32)
        o = [x * lax.rsqrt(jnp.mean(x * x, axis=-1, keepdims=True) + EPS) * g for x in o]
        for b in bs:
            o_ref[b, rows] = (jnp.concatenate(o[b * nh:(b + 1) * nh], axis=-1) * gate_ref[b, rows]).astype(BF16)
    for u, (b, h) in enumerate(units):
        s_scr[b, h] = state[u]

    @pl.when(c == pl.num_programs(1) - 1)
    def _():
        sfin_ref[...] = s_scr[...]


def _gla(q, k, v, gate, la, s0, g, *, bb, L, nsub=1):
    B, T, a_qk = q.shape
    a_v = v.shape[-1]
    _, nh, dk, dv = s0.shape
    n = T // (L * nsub)
    assert T % (L * nsub) == 0

    def tok(w):
        return pl.BlockSpec((bb, L * nsub, w), lambda i, c: (i, c, 0))

    st = pl.BlockSpec((bb, nh, dk, dv), lambda i, c: (i, 0, 0, 0))
    return pl.pallas_call(
        functools.partial(_gla_kernel, bb=bb, L=L, nh=nh, dk=dk, dv=dv),
        grid=(B // bb, n),
        in_specs=[tok(a_qk), tok(a_qk), tok(a_v), tok(a_v), tok(a_qk), st, pl.BlockSpec(g.shape, lambda i, c: (0, 0))],
        out_specs=[tok(a_v), st],
        out_shape=[jax.ShapeDtypeStruct((B, T, a_v), BF16), jax.ShapeDtypeStruct(s0.shape, F32)],
        scratch_shapes=[pltpu.VMEM((bb, nh, dk, dv), F32)],
        compiler_params=_params("parallel", "arbitrary"),
        name="gla",
    )(q, k, v, gate, la, s0, g)


def _bias_table_kernel(rel_ref, out_ref, *, chunk, win, n_rel):
    n_var, nh, rows, width = out_ref.shape
    back = win - chunk
    wide = width + back + rows
    assert wide % 128 == 0 and chunk & (chunk - 1) == 0
    u = lax.broadcasted_iota(jnp.int32, (n_rel, wide), 1)
    want = jnp.clip(back + rows - 1 - u, -MAX_REL, MAX_REL) + MAX_REL
    onehot = (lax.broadcasted_iota(jnp.int32, (n_rel, wide), 0) == want).astype(F32)
    vec = jnp.dot(rel_ref[...], onehot, precision=HIGHEST, preferred_element_type=F32)
    i = lax.broadcasted_iota(jnp.int32, (rows, width), 0)
    j = lax.broadcasted_iota(jnp.int32, (rows, width), 1)
    first = i - (i & (chunk - 1))
    for t in range(n_var):
        shift = back - t * rows if t < n_var - 1 else 0
        ok = (j + shift >= first) & (j + shift < first + win)
        for h in range(nh):
            x = jnp.broadcast_to(vec[h:h + 1, :], (rows, wide))
            x = pltpu.roll(x, (wide - (shift + rows - 1)) % wide, 1, stride=1, stride_axis=0)
            out_ref[t, h] = jnp.where(ok, x[:, :width] * LOG2E, -jnp.inf)


def _bias_table(rel_padded, *, chunk, win, rows):
    nh, n_rel = rel_padded.shape
    n_var = (win - chunk) // rows + 1
    return pl.pallas_call(
        functools.partial(_bias_table_kernel, chunk=chunk, win=win, n_rel=n_rel),
        out_shape=jax.ShapeDtypeStruct((n_var, nh, rows, win + rows - chunk), F32),
        compiler_params=_params(),
        name="band_bias_table",
    )(rel_padded)


def _pair_masks(d):
    lane = lax.broadcasted_iota(jnp.int32, (1, 2 * d), 1)
    return lane, [lane < d, lane >= d]


def _band_prompt_kernel(q_ref, k_ref, v_ref, bias_ref, o_ref, *, nh, d, back):
    n_var, _, tq, width = bias_ref.shape
    nsub = q_ref.shape[1] // tq
    lane, mine = _pair_masks(d)
    ones_lane = [d, 0]
    cols = [slice(p * 2 * d, (p + 1) * 2 * d) for p in range(nh // 2)]
    zero = jnp.zeros((), BF16)
    q, kw, vw, var = [], [], [], []
    for sb in range(nsub):
        c = pl.program_id(1) * nsub + sb
        start = pl.multiple_of(jnp.maximum(c * tq - back, 0), tq)
        var.append(jnp.minimum(c, n_var - 1))
        q.append(q_ref[0, sb * tq:(sb + 1) * tq, :])
        kw.append(k_ref[0, pl.ds(start, width), :])
        vw.append(v_ref[0, pl.ds(start, width), :])
    units = [(sb, p, hh) for sb in range(nsub) for p in range(nh // 2) for hh in range(2)]
    s = [_dot_nt(jnp.where(mine[hh], q[sb][:, cols[p]], zero), kw[sb][:, cols[p]]) + bias_ref[var[sb], 2 * p + hh]
         for sb, p, hh in units]
    m = [jnp.max(x, axis=-1, keepdims=True) for x in s]
    e = [jnp.exp2(x - mx).astype(BF16) for x, mx in zip(s, m)]
    r = [_dot(e[u], jnp.where(mine[hh], vw[sb][:, cols[p]], (lane == ones_lane[hh]).astype(BF16)))
         for u, (sb, p, hh) in enumerate(units)]
    r = [x / x[:, ones_lane[hh]:ones_lane[hh] + 1] for x, (sb, p, hh) in zip(r, units)]
    for sb in range(nsub):
        rs = r[sb * nh:(sb + 1) * nh]
        o_ref[0, sb * tq:(sb + 1) * tq, :] = jnp.concatenate(
            [jnp.where(mine[0], rs[2 * p], rs[2 * p + 1]) for p in range(nh // 2)], axis=-1).astype(BF16)


def _band_prompt(q, k, v, bias, *, nh, d, nsub=4):
    B, T, w = q.shape
    n_var, _, tq, width = bias.shape
    tq *= nsub
    assert T >= width and T % tq == 0
    kv = pl.BlockSpec((1, T, w), lambda b, c: (b, 0, 0))
    return pl.pallas_call(
        functools.partial(_band_prompt_kernel, nh=nh, d=d, back=width - tq // nsub),
        grid=(B, T // tq),
        in_specs=[pl.BlockSpec((1, tq, w), lambda b, c: (b, c, 0)), kv, kv, _resident(bias.shape)],
        out_specs=pl.BlockSpec((1, tq, w), lambda b, c: (b, c, 0)),
        out_shape=jax.ShapeDtypeStruct((B, T, w), BF16),
        compiler_params=_params("parallel", "arbitrary"),
        name="band_prompt",
    )(q, k, v, bias)


def _band_sample_kernel(q_ref, ckt_ref, cvt_ref, kn_ref, vn_ref, bias_ref, o_ref, *, nh, d, s_new):
    lc = ckt_ref.shape[3]
    q = q_ref[0]
    kn = kn_ref[0]
    vn = vn_ref[0]
    hs = range(nh)
    sl = [slice(h * d, (h + 1) * d) for h in hs]
    sc = [_dot(q[:, sl[h]], ckt_ref[0, h].astype(BF16)) + bias_ref[0, h][:s_new, :lc] for h in hs]
    sn = [_dot_nt(q[:, sl[h]], kn[:, sl[h]]) + bias_ref[0, h][:s_new, lc:lc + s_new] for h in hs]
    m = [jnp.maximum(jnp.max(sc[h], axis=-1, keepdims=True), jnp.max(sn[h], axis=-1, keepdims=True)) for h in hs]
    ec = [jnp.exp2(sc[h] - m[h]) for h in hs]
    en = [jnp.exp2(sn[h] - m[h]) for h in hs]
    l = [jnp.sum(ec[h], axis=-1, keepdims=True) + jnp.sum(en[h], axis=-1, keepdims=True) for h in hs]
    o = [(_dot_nt(ec[h].astype(BF16), cvt_ref[0, h].astype(BF16)) + _dot(en[h].astype(BF16), vn[:, sl[h]])) / l[h]
         for h in hs]
    o_ref[0] = jnp.concatenate(o, axis=-1).astype(BF16)


def _band_sample(q, ckt, cvt, kn, vn, bias, *, nh, d):
    B, s_new, w = q.shape
    lc = ckt.shape[3]
    n_var, _, rows, width = bias.shape
    assert s_new <= CHUNK and lc == width - rows
    new = pl.BlockSpec((1, s_new, w), lambda b: (b, 0, 0))
    old = pl.BlockSpec((1, nh, d, lc), lambda b: (b, 0, 0, 0))
    return pl.pallas_call(
        functools.partial(_band_sample_kernel, nh=nh, d=d, s_new=s_new),
        grid=(B,),
        in_specs=[new, old, old, new, new, pl.BlockSpec((1, nh, rows, width), lambda b: (n_var - 1, 0, 0, 0))],
        out_specs=new,
        out_shape=jax.ShapeDtypeStruct((B, s_new, w), BF16),
        compiler_params=_params("parallel"),
        name="band_sample",
    )(q, ckt, cvt, kn, vn, bias)


def _cumsum_lanes_kernel(x_ref, out_ref, *, blk, suffix):
    R, T = x_ref.shape
    src = lax.broadcasted_iota(jnp.int32, (blk, blk), 0)
    dst = lax.broadcasted_iota(jnp.int32, (blk, blk), 1)
    tri = ((src > dst) if suffix else (src <= dst)).astype(F32)
    carry = jnp.zeros((R, 1), F32)
    n = T // blk
    for i in (range(n - 1, -1, -1) if suffix else range(n)):
        xb = x_ref[:, i * blk:(i + 1) * blk]
        out_ref[:, i * blk:(i + 1) * blk] = jnp.dot(xb, tri, precision=HIGHEST, preferred_element_type=F32) + carry
        carry = carry + jnp.sum(xb, axis=1, keepdims=True)


def _cumsum_lanes(x, *, suffix):
    return pl.pallas_call(
        functools.partial(_cumsum_lanes_kernel, blk=256, suffix=suffix),
        out_shape=jax.ShapeDtypeStruct(x.shape, F32),
        compiler_params=_params(),
        name="cumsum_lanes",
    )(x)


def _fox_prompt_kernel(q_ref, kt_ref, vt_ref, ft_ref, o_ref, *, tq, tk, d):
    p = pl.program_id(1)
    T = q_ref.shape[1]
    assert tq == tk and d % 8 == 0
    nq = T // tq
    lane, mine = _pair_masks(d)
    q_fill = [((lane >= d) & (lane < d + 3)).astype(BF16), (lane < 3).astype(BF16)]
    den = [d, 0]
    sub = lax.broadcasted_iota(jnp.int32, (8, tk), 0)
    ones_blk = (sub == 0).astype(F32)
    pad_blk = jnp.zeros((d - 8, tk), F32)

    def stacked(x, blk, hh):
        return jnp.concatenate([x, blk, pad_blk] if hh == 0 else [blk, pad_blk, x], axis=0)

    diag_mask = (lax.broadcasted_iota(jnp.int32, (tk, tq), 0) <= lax.broadcasted_iota(jnp.int32, (tk, tq), 1))
    npair = q_ref.shape[2] // (2 * d)
    heads = [(pr, hh) for pr in range(npair) for hh in range(2)]
    units = [(qi, pr, hh) for qi in range(nq) for pr, hh in heads]
    qs = {(qi, pr, hh): jnp.where(mine[hh], q_ref[0, qi * tq:(qi + 1) * tq, pr * 2 * d:(pr + 1) * 2 * d], q_fill[hh])
          for qi, pr, hh in units}
    m = {u: jnp.full((1, tq), -jnp.inf, F32) for u in units}
    acc = {u: jnp.zeros((2 * d, tq), F32) for u in units}
    for kv in range(nq):
        ks = slice(kv * tk, (kv + 1) * tk)
        k_rows, v_all = {}, {}
        for pr, hh in heads:
            f = ft_ref[0, pl.ds(2 * (p * npair + pr) + hh, 1), ks] * LOG2E
            hi = f.astype(BF16).astype(F32)
            mid = (f - hi).astype(BF16).astype(F32)
            f_blk = jnp.where(sub == 0, -hi, jnp.where(sub == 1, -mid, jnp.where(sub == 2, -(f - hi - mid), 0.0)))
            k_rows[pr, hh] = stacked(kt_ref[0, 2 * pr + hh, :, ks], f_blk, hh).T.astype(BF16)
            v_all[pr, hh] = stacked(vt_ref[0, 2 * pr + hh, :, ks], ones_blk, hh).astype(BF16)
        live = [u for u in units if u[0] >= kv]
        s = {u: _dot_nt(k_rows[u[1:]], qs[u]) for u in live}
        for u in live:
            if u[0] == kv:
                s[u] = jnp.where(diag_mask, s[u], -jnp.inf)
        m_new = {u: jnp.maximum(m[u], jnp.max(s[u], axis=0, keepdims=True)) for u in live}
        e = {u: jnp.exp2(s[u] - m_new[u]).astype(BF16) for u in live}
        for u in live:
            acc[u] = jnp.exp2(m[u] - m_new[u]) * acc[u] + _dot(v_all[u[1:]], e[u])
            m[u] = m_new[u]
        for pr in range(npair):
            out = [(acc[kv, pr, hh] / acc[kv, pr, hh][den[hh]:den[hh] + 1, :]).T for hh in range(2)]
            o_ref[0, kv * tq:(kv + 1) * tq, pr * 2 * d:(pr + 1) * 2 * d] = jnp.where(mine[0], out[0],
                                                                                   out[1]).astype(BF16)


def _fox_prompt(q, kt, vt, ft, *, tq, tk, npair=2):
    B, T, w = q.shape
    _, nh, d, _ = kt.shape
    kv = pl.BlockSpec((1, 2 * npair, d, T), lambda b, p: (b, p, 0, 0))
    qo = pl.BlockSpec((1, T, 2 * npair * d), lambda b, p: (b, 0, p))
    return pl.pallas_call(
        functools.partial(_fox_prompt_kernel, tq=tq, tk=tk, d=d),
        grid=(B, nh // (2 * npair)),
        in_specs=[qo, kv, kv, pl.BlockSpec((1, nh, T), lambda b, p: (b, 0, 0))],
        out_specs=qo,
        out_shape=jax.ShapeDtypeStruct((B, T, w), BF16),
        compiler_params=_params("parallel", "arbitrary"),
        name="fox_prompt",
    )(q, kt, vt, ft)


def _fox_sample_kernel(q_ref, kt_ref, vt_ref, ct_ref, kn_ref, vn_ref, lf_ref, o_ref, fqt_scr, *, hg, d, s_new):
    g = pl.program_id(1)
    nh = lf_ref.shape[2]
    tri = (lax.broadcasted_iota(jnp.int32, (s_new, s_new), 0)
           >= lax.broadcasted_iota(jnp.int32, (s_new, s_new), 1))
    eye = (lax.broadcasted_iota(jnp.int32, (nh, nh), 0) == lax.broadcasted_iota(jnp.int32, (nh, nh), 1)).astype(F32)
    fq = jnp.dot(tri.astype(F32), lf_ref[0], precision=HIGHEST, preferred_element_type=F32)
    fqt_scr[...] = lax.dot_general(eye, fq, NT, precision=HIGHEST, preferred_element_type=F32)
    outs = []
    for hh in range(hg):
        sl = slice(hh * d, (hh + 1) * d)
        head = g * hg + hh
        qh = q_ref[0, 0, :, sl]
        sc = _dot(qh, kt_ref[0, hh].astype(BF16)) + ct_ref[0, pl.ds(head, 1), :]
        sn = _dot_nt(qh, kn_ref[0, 0, :, sl]) - fqt_scr[pl.ds(head, 1), :]
        sn = jnp.where(tri, sn, -jnp.inf)
        m = jnp.maximum(jnp.max(sc, axis=-1, keepdims=True), jnp.max(sn, axis=-1, keepdims=True))
        ec = jnp.exp(sc - m)
        en = jnp.exp(sn - m)
        l = jnp.sum(ec, axis=-1, keepdims=True) + jnp.sum(en, axis=-1, keepdims=True)
        o = _dot_nt(ec.astype(BF16), vt_ref[0, hh].astype(BF16)) + _dot(en.astype(BF16), vn_ref[0, 0, :, sl])
        outs.append(o / l)
    o_ref[0, 0] = jnp.concatenate(outs, axis=-1).astype(BF16)


def _fox_sample(q, kt, vt, ct, kn, vn, lf, *, hg):
    B, ng, s_new, gw = q.shape
    _, nh, d, lc = kt.shape
    new = pl.BlockSpec((1, 1, s_new, gw), lambda b, g: (b, g, 0, 0))
    old = pl.BlockSpec((1, hg, d, lc), lambda b, g: (b, g, 0, 0))
    return pl.pallas_call(
        functools.partial(_fox_sample_kernel, hg=hg, d=d, s_new=s_new),
        grid=(B, ng),
        in_specs=[new, old, old, pl.BlockSpec((1, nh, lc), lambda b, g: (b, 0, 0)), new, new,
                  pl.BlockSpec((1, s_new, nh), lambda b, g: (b, 0, 0))],
        out_specs=new,
        out_shape=jax.ShapeDtypeStruct(q.shape, BF16),
        scratch_shapes=[pltpu.VMEM((nh, s_new), F32)],
        compiler_params=_params("parallel", "arbitrary"),
        name="fox_sample",
    )(q, kt, vt, ct, kn, vn, lf)


def _trunk(x, state, wts, dims, *, tm, gla_bb):
    B, T, D = x.shape
    n = B * T
    depth = wts["norm_g"].shape[0]
    h_a, dk_a, dv_a, h_b, d_b, h_c, d_c = dims
    a_qk, a_v, b_w, c_w = h_a * dk_a, h_a * dv_a, h_b * d_b, h_c * d_c
    xs = x.reshape(n, D)
    new_states = []
    pending = ()
    r3 = lambda t: t.reshape(B, T, t.shape[-1])
    ffn_w = wts["ffn_bf16"]

    def ffn(xs, g, key, proj, final_g=None):
        nxt = (key[0], 1) if key[1] == 0 else (key[0] + 1, 0)
        if state is not None:
            return _ffn_streamed(xs, g, *ffn_w[key], proj, final_g, n_chunks=2)
        if nxt[0] >= depth:
            return _ffn(xs, g, *ffn_w[key], proj, final_g, tm=tm)
        xs, *ffn_w[nxt] = _ffn(xs, g, *ffn_w[key], proj, final_g, (wts["w_gu"], wts["w_down"], nxt), tm=tm)
        return xs

    for l in range(depth):
        i = l // 2
        g = wts["norm_g"][l]
        xs = ffn(xs, g[0:1], (l, 0), pending)
        if l % 2 == 0:
            prompt = state is None
            keep = min(BAND_CHUNKS * CHUNK, T)
            qa, ka, va, gate, la, bq, bk, bv, *kv32 = _even_in(
                xs, g[1:2], wts["even_main"][i], wts["even_r"][i], wts["alpha_up"][i], wts["b_alpha"][i],
                tm=tm, a_qk=a_qk, a_v=a_v, b_w=b_w, dk_a=dk_a, d_b=d_b, kv32=not prompt)
            nk, nv = _band_tail(xs, g[1:2], wts["even_kvt"][i], B=B, T=T, keep=keep) if prompt else kv32
            s0 = jnp.zeros((B, h_a, dk_a, dv_a), F32) if prompt else state[l][0]
            o_a, s_fin = _gla(r3(qa), r3(ka), r3(va), r3(gate), r3(la), s0, wts["gla_g"][i],
                              bb=gla_bb, L=min(CHUNK, T))
            if prompt:
                o_b = _band_prompt(r3(bq), r3(bk), r3(bv), wts["band_bias"][i], nh=h_b, d=d_b)
                nk, nv = (t.reshape(B, h_b, d_b, keep).transpose(0, 3, 1, 2) for t in (nk, nv))
            else:
                ck, cv = state[l][1], state[l][2]
                o_b = _band_sample(r3(bq), ck.transpose(0, 2, 3, 1), cv.transpose(0, 2, 3, 1), r3(bk), r3(bv),
                                   wts["band_bias"][i], nh=h_b, d=d_b)
                nk, nv = (t.reshape(B, T, h_b, d_b) for t in (nk, nv))
            new_states.append((s_fin, nk, nv))
            pending = ((o_a.reshape(n, a_v), wts["even_out_a"][i]), (o_b.reshape(n, b_w), wts["even_out_b"][i]))
        elif state is None:
            q, kt, vt, lft = _odd_in_t(xs, g[1:2], wts["odd_q"][i], wts["odd_kvt"][i], wts["odd_ft"][i],
                                       wts["b_f_col"][i], B=B, T=T, tm=tm, nh=h_c, d_c=d_c)
            ft = _cumsum_lanes(lft.reshape(B * h_c, T), suffix=False).reshape(B, h_c, T)
            o = _fox_prompt(r3(q), kt, vt, ft, tq=512, tk=512)
            new_states.append((kt.transpose(0, 3, 1, 2), vt.transpose(0, 3, 1, 2), lft.transpose(0, 2, 1)))
            pending = ((o.reshape(n, c_w), wts["odd_out"][i]),)
        else:
            q, k, v, k32, v32, lf = _odd_in(xs, g[1:2], wts["odd_main"][i], wts["odd_f"][i], wts["b_f"][i],
                                             tm=tm, c_w=c_w, d_c=d_c, h_c=h_c)
            ck, cv, clf = state[l]
            lc = ck.shape[1]
            hg = 4
            ct = _cumsum_lanes(clf.astype(F32).transpose(0, 2, 1).reshape(B * h_c, lc), suffix=True)
            grp = lambda t: t.reshape(B, T, h_c // hg, hg * d_c).transpose(0, 2, 1, 3)
            o = _fox_sample(grp(q), ck.transpose(0, 2, 3, 1), cv.transpose(0, 2, 3, 1), ct.reshape(B, h_c, lc),
                            grp(k), grp(v), r3(lf), hg=hg)
            o = o.transpose(0, 2, 1, 3).reshape(n, c_w)
            new_states.append((r3(k32).reshape(B, T, h_c, d_c), r3(v32).reshape(B, T, h_c, d_c), r3(lf)))
            pending = ((o, wts["odd_out"][i]),)
        last = l == depth - 1
        xs = ffn(xs, g[2:3], (l, 1), pending, wts["final_g"] if last else None)
        pending = ()
    return xs.reshape(B, T, D), new_states


def kernel(x_prompt, x_sample, state_gla, cache_band_k, cache_band_v, cache_fox_k, cache_fox_v, cache_fox_logf,
           norm_g, ffn_w_gu, ffn_w_down, even_w_in, gla_w_alpha_up, gla_b_alpha, gla_norm_g, band_rel_bias,
           even_w_out, odd_w_in, fox_b_f, odd_w_out, final_norm_g):
    depth = norm_g.shape[0]
    _, _, h_a, dk_a, dv_a = state_gla.shape
    h_b, d_b = cache_band_k.shape[-2:]
    h_c, d_c = cache_fox_k.shape[-2:]
    r_a = gla_w_alpha_up.shape[1]
    a_qk, a_v, b_w, c_w = h_a * dk_a, h_a * dv_a, h_b * d_b, h_c * d_c
    assert cache_band_k.shape[2] == BAND_CHUNKS * CHUNK and band_rel_bias.shape[-1] == 2 * MAX_REL + 1

    r0 = 2 * a_qk + 2 * a_v
    n_rel = 384
    rel_padded = jnp.pad(band_rel_bias, ((0, 0), (0, 0), (0, n_rel - band_rel_bias.shape[-1])))
    win = (BAND_CHUNKS + 1) * CHUNK
    odd_t = odd_w_in.transpose(0, 2, 1)
    wts = {
        "norm_g": norm_g,
        "w_gu": ffn_w_gu,
        "w_down": ffn_w_down,
        "ffn_bf16": {(0, 0): [ffn_w_gu[0, 0].astype(BF16), ffn_w_down[0, 0].astype(BF16)]},
        "even_main": jnp.concatenate([even_w_in[:, :, :r0], even_w_in[:, :, r0 + r_a:]], axis=-1).astype(BF16),
        "even_r": even_w_in[:, :, r0:r0 + r_a].astype(BF16),
        "even_kvt": even_w_in.transpose(0, 2, 1)[:, r0 + r_a + b_w:].astype(BF16),
        "alpha_up": gla_w_alpha_up.astype(BF16),
        "b_alpha": gla_b_alpha[:, None, :],
        "gla_g": gla_norm_g[:, None, :],
        "band_bias": [_bias_table(rel_padded[i], chunk=CHUNK, win=win, rows=2 * CHUNK)
                      for i in range(rel_padded.shape[0])],
        "even_out_a": even_w_out[:, :a_v].astype(BF16),
        "even_out_b": even_w_out[:, a_v:].astype(BF16),
        "odd_main": odd_w_in[:, :, :3 * c_w].astype(BF16),
        "odd_f": odd_w_in[:, :, 3 * c_w:].astype(BF16),
        "odd_q": odd_w_in[:, :, :c_w].astype(BF16),
        "odd_kvt": odd_t[:, c_w:3 * c_w].astype(BF16),
        "odd_ft": odd_t[:, 3 * c_w:].astype(BF16),
        "b_f": fox_b_f[:, None, :],
        "b_f_col": fox_b_f[:, :, None],
        "odd_out": odd_w_out.astype(BF16),
        "final_g": final_norm_g[None, :],
    }
    dims = (h_a, dk_a, dv_a, h_b, d_b, h_c, d_c)

    sample_states = []
    for l in range(depth):
        i = l // 2
        if l % 2 == 0:
            sample_states.append((state_gla[i], cache_band_k[i], cache_band_v[i]))
        else:
            sample_states.append((cache_fox_k[i], cache_fox_v[i], cache_fox_logf[i]))

    y_prompt, ns_p = _trunk(x_prompt, None, wts, dims, tm=512, gla_bb=8)
    y_sample, ns_s = _trunk(x_sample, sample_states, wts, dims, tm=256, gla_bb=8)

    ev = range(0, depth, 2)
    od = range(1, depth, 2)
    stack = lambda ns, layers, j: jnp.stack([ns[l][j] for l in layers])
    return (y_prompt, y_sample,
            stack(ns_p, ev, 0), stack(ns_p, ev, 1), stack(ns_p, ev, 2),
            stack(ns_p, od, 0), stack(ns_p, od, 1), stack(ns_p, od, 2),
            stack(ns_s, ev, 0), stack(ns_s, ev, 1), stack(ns_s, ev, 2),
            stack(ns_s, od, 0), stack(ns_s, od, 1), stack(ns_s, od, 2))
```

```python
import functools

import jax
import jax.numpy as jnp
from jax import lax
from jax.experimental import pallas as pl
from jax.experimental.pallas import tpu as pltpu

F32 = jnp.float32
BF16 = jnp.bfloat16
HIGHEST = lax.Precision.HIGHEST
NT = (((1,), (1,)), ((), ()))
TN = (((0,), (0,)), ((), ()))

EPS = 1e-6
CHUNK = 64
BAND_CHUNKS = 8
MAX_REL = 128
GATE_NORM_A = 16.0
LOG2E = 1.4426950408889634
VMEM_LIMIT_BYTES = 56 * 1024 * 1024


def _params(*sem):
    return pltpu.CompilerParams(dimension_semantics=sem, vmem_limit_bytes=VMEM_LIMIT_BYTES)


def _resident(shape):
    nd = len(shape)
    return pl.BlockSpec(shape, lambda *_: (0,) * nd, pipeline_mode=pl.Buffered(1))


def _rows(tm, width):
    return pl.BlockSpec((tm, width), lambda i: (i, 0))


def _rms(x, g):
    ms = jnp.mean(x * x, axis=-1, keepdims=True)
    return x * lax.rsqrt(ms + EPS) * g


def _log_sigmoid(x):
    return jnp.minimum(x, 0.0) - jnp.log1p(jnp.exp(-jnp.abs(x)))


def _dot(a, b):
    return jnp.dot(a, b, preferred_element_type=F32)


def _dot_nt(a, b):
    return lax.dot_general(a, b, NT, preferred_element_type=F32)


def _ffn_kernel(*refs, n_proj, has_final, casts, d_ff):
    it = iter(refs)
    x_ref = next(it)
    proj = [(next(it), next(it)) for _ in range(n_proj)]
    g_ref, wgu_ref, wd_ref = next(it), next(it), next(it)
    gf_ref = next(it) if has_final else None
    cast_src = [next(it) for _ in range(casts)]
    out_ref = next(it)
    for src in cast_src:
        next(it)[...] = src[...].astype(BF16)

    x = x_ref[...]
    for o_ref, w_ref in proj:
        x = x + _dot(o_ref[...], w_ref[...])
    xn = _rms(x, g_ref[...]).astype(BF16)
    h = _dot(xn, wgu_ref[...])
    a = (jax.nn.silu(h[:, :d_ff]) * h[:, d_ff:]).astype(BF16)
    y = x + 0.5 * _dot(a, wd_ref[...])
    if has_final:
        y = _rms(y, gf_ref[...])
    out_ref[...] = y


def _ffn(x, g, wgu, wd, proj=(), final_g=None, cast_next=None, *, tm):
    n, d = x.shape
    d_ff = wd.shape[0]
    steps = n // tm
    args, specs = [x], [_rows(tm, d)]
    for o, w in proj:
        args += [o, w]
        specs += [_rows(tm, o.shape[1]), _resident(w.shape)]
    args += [g, wgu, wd]
    specs += [_resident(g.shape), _resident(wgu.shape), _resident(wd.shape)]
    if final_g is not None:
        args.append(final_g)
        specs.append(_resident(final_g.shape))
    out_specs, out_shapes = [_rows(tm, d)], [jax.ShapeDtypeStruct((n, d), F32)]
    if cast_next is not None:
        src_gu, src_d, idx = cast_next
        for src, rows, per in ((src_gu, d, 1), (src_d, d_ff, 2)):
            slab = rows * per // steps
            assert rows * per % steps == 0 and slab % 16 == 0
            args.append(src)
            specs.append(pl.BlockSpec((None,) * len(idx) + (slab, src.shape[-1]),
                                      lambda i, per=per: tuple(idx) + (i // per, 0)))
            out_specs.append(pl.BlockSpec((slab, src.shape[-1]), lambda i, per=per: (i // per, 0)))
            out_shapes.append(jax.ShapeDtypeStruct((rows, src.shape[-1]), BF16))
    outs = pl.pallas_call(
        functools.partial(_ffn_kernel, n_proj=len(proj), has_final=final_g is not None,
                          casts=0 if cast_next is None else 2, d_ff=d_ff),
        grid=(steps,),
        in_specs=specs,
        out_specs=out_specs,
        out_shape=out_shapes,
        compiler_params=_params("arbitrary" if cast_next is not None else "parallel"),
        name="ffn",
    )(*args)
    return outs[0] if cast_next is None else outs


def _ffn_streamed_kernel(*refs, n_proj, has_final):
    it = iter(refs)
    x_ref = next(it)
    proj = [(next(it), next(it)) for _ in range(n_proj)]
    g_ref, wg_ref, wu_ref, wd_ref = next(it), next(it), next(it), next(it)
    gf_ref = next(it) if has_final else None
    out_ref, x_scr, xn_scr, acc_scr = next(it), next(it), next(it), next(it)
    c = pl.program_id(0)

    @pl.when(c == 0)
    def _():
        x = x_ref[...]
        for o_ref, w_ref in proj:
            x = x + _dot(o_ref[...], w_ref[...])
        x_scr[...] = x
        xn_scr[...] = _rms(x, g_ref[...]).astype(BF16)
        acc_scr[...] = jnp.zeros(acc_scr.shape, F32)

    xn = xn_scr[...]
    a = (jax.nn.silu(_dot(xn, wg_ref[...])) * _dot(xn, wu_ref[...])).astype(BF16)
    acc_scr[...] += _dot(a, wd_ref[...])

    @pl.when(c == pl.num_programs(0) - 1)
    def _():
        y = x_scr[...] + 0.5 * acc_scr[...]
        if has_final:
            y = _rms(y, gf_ref[...])
        out_ref[...] = y


def _ffn_streamed(x, g, wgu, wd, proj=(), final_g=None, *, n_chunks):
    n, d = x.shape
    d_ff = wd.shape[0]
    ck = d_ff // n_chunks
    assert d_ff % n_chunks == 0 and ck % 128 == 0
    whole = lambda a: pl.BlockSpec(a.shape, lambda c: (0,) * a.ndim)
    args, specs = [x], [whole(x)]
    for o, w in proj:
        args += [o, w]
        specs += [whole(o), whole(w)]
    args += [g, wgu, wgu, wd]
    specs += [whole(g), pl.BlockSpec((d, ck), lambda c: (0, c)), pl.BlockSpec((d, ck), lambda c: (0, n_chunks + c)),
              pl.BlockSpec((ck, d), lambda c: (c, 0))]
    if final_g is not None:
        args.append(final_g)
        specs.append(whole(final_g))
    return pl.pallas_call(
        functools.partial(_ffn_streamed_kernel, n_proj=len(proj), has_final=final_g is not None),
        grid=(n_chunks,),
        in_specs=specs,
        out_specs=pl.BlockSpec((n, d), lambda c: (0, 0)),
        out_shape=jax.ShapeDtypeStruct((n, d), F32),
        scratch_shapes=[pltpu.VMEM((n, d), F32), pltpu.VMEM((n, d), BF16), pltpu.VMEM((n, d), F32)],
        compiler_params=_params("arbitrary"),
        name="ffn_streamed",
    )(*args)


def _even_in_kernel(x_ref, g_ref, w_ref, wr_ref, wup_ref, ba_ref,
                    qa_ref, kat_ref, va_ref, gate_ref, la_ref, lat_ref, bq_ref, bk_ref, bv_ref, *kv32_refs,
                    a_qk, a_v, b_w, dk_a, d_b):
    xn = _rms(x_ref[...], g_ref[...]).astype(BF16)
    h = _dot(xn, w_ref[...])
    o = 0
    qa_ref[...] = h[:, o:o + a_qk] * (dk_a ** -0.5)
    o += a_qk
    kat_ref[...] = h[:, o:o + a_qk].T
    o += a_qk
    va_ref[...] = h[:, o:o + a_v].astype(BF16)
    o += a_v
    gate_ref[...] = jax.nn.silu(h[:, o:o + a_v])
    o += a_v
    bq_ref[...] = (h[:, o:o + b_w] * (LOG2E * d_b ** -0.5)).astype(BF16)
    o += b_w
    bk = h[:, o:o + b_w]
    o += b_w
    bv = h[:, o:o + b_w]
    bk_ref[...] = bk.astype(BF16)
    bv_ref[...] = bv.astype(BF16)
    if kv32_refs:
        kv32_refs[0][...] = bk
        kv32_refs[1][...] = bv
    r = _dot(xn, wr_ref[...]).astype(BF16)
    la = _log_sigmoid(_dot(r, wup_ref[...]) + ba_ref[...]) / GATE_NORM_A
    la_ref[...] = la
    lat_ref[...] = la.T


def _even_in(x, g, w, *, tm, t_split, a_qk, a_v, b_w, dk_a, d_b, kv32):
    n, d = x.shape
    bo, to = t_split
    nb = to // tm
    assert bo * to == n and to % tm == 0
    tmin = (pl.BlockSpec((None, a_qk, tm), lambda i: (i // nb, 0, i % nb)), jax.ShapeDtypeStruct((bo, a_qk, to), F32))
    tok = lambda width, dt: (_rows(tm, width), jax.ShapeDtypeStruct((n, width), dt))
    outs = [tok(a_qk, F32), tmin, tok(a_v, BF16), tok(a_v, F32), tok(a_qk, F32), tmin,
            tok(b_w, BF16), tok(b_w, BF16), tok(b_w, BF16)]
    if kv32:
        outs += [tok(b_w, F32), tok(b_w, F32)]
    args = [x, g] + [w[k] for k in ("main", "r", "up", "b_alpha")]
    return pl.pallas_call(
        functools.partial(_even_in_kernel, a_qk=a_qk, a_v=a_v, b_w=b_w, dk_a=dk_a, d_b=d_b),
        grid=(n // tm,),
        in_specs=[_rows(tm, d)] + [_resident(a.shape) for a in args[1:]],
        out_specs=[s for s, _ in outs],
        out_shape=[t for _, t in outs],
        compiler_params=_params("parallel"),
        name="even_in",
    )(*args)


def _band_tail_kernel(x_ref, g_ref, wkvt_ref, kt_ref, vt_ref):
    xn = _rms(x_ref[...], g_ref[...]).astype(BF16)
    kvt = _dot_nt(wkvt_ref[...], xn)
    b_w = kt_ref.shape[1]
    kt_ref[0] = kvt[:b_w]
    vt_ref[0] = kvt[b_w:]


def _band_tail(x, g, w_kvt, *, B, T, keep):
    n, d = x.shape
    b_w = w_kvt.shape[0] // 2
    assert T % keep == 0
    nb = T // keep
    out = pl.BlockSpec((1, b_w, keep), lambda b: (b, 0, 0))
    return pl.pallas_call(
        _band_tail_kernel,
        grid=(B,),
        in_specs=[pl.BlockSpec((keep, d), lambda b: (b * nb + nb - 1, 0)), _resident(g.shape), _resident(w_kvt.shape)],
        out_specs=[out, out],
        out_shape=[jax.ShapeDtypeStruct((B, b_w, keep), F32)] * 2,
        compiler_params=_params("parallel"),
        name="band_tail",
    )(x, g, w_kvt)


def _odd_in_kernel(x_ref, g_ref, w_ref, wf_ref, bf_ref, q_ref, k_ref, v_ref, k32_ref, v32_ref, lf_ref, *, c_w, d_c):
    xn = _rms(x_ref[...], g_ref[...]).astype(BF16)
    h = _dot(xn, w_ref[...])
    q_ref[...] = (h[:, :c_w] * (d_c ** -0.5)).astype(BF16)
    k = h[:, c_w:2 * c_w]
    v = h[:, 2 * c_w:]
    k32_ref[...] = k
    v32_ref[...] = v
    k_ref[...] = k.astype(BF16)
    v_ref[...] = v.astype(BF16)
    lf_ref[...] = _log_sigmoid(_dot(xn, wf_ref[...]) + bf_ref[...])


def _odd_in(x, g, w_main, w_f, b_f, *, tm, c_w, d_c, h_c):
    n, d = x.shape
    outs = [(c_w, BF16), (c_w, BF16), (c_w, BF16), (c_w, F32), (c_w, F32), (h_c, F32)]
    return pl.pallas_call(
        functools.partial(_odd_in_kernel, c_w=c_w, d_c=d_c),
        grid=(n // tm,),
        in_specs=[_rows(tm, d), _resident(g.shape), _resident(w_main.shape), _resident(w_f.shape),
                  _resident(b_f.shape)],
        out_specs=[_rows(tm, w) for w, _ in outs],
        out_shape=[jax.ShapeDtypeStruct((n, w), dt) for w, dt in outs],
        compiler_params=_params("parallel"),
        name="odd_in",
    )(x, g, w_main, w_f, b_f)


def _odd_in_t_kernel(x_ref, g_ref, wq_ref, wkvt_ref, wft_ref, bf_ref, q_ref, kt_ref, vt_ref, lft_ref, *, nh, d_c):
    xn = _rms(x_ref[...], g_ref[...]).astype(BF16)
    tm = xn.shape[0]
    c_w = nh * d_c
    q_ref[...] = (_dot(xn, wq_ref[...]) * (LOG2E * d_c ** -0.5)).astype(BF16)
    ht = _dot_nt(wkvt_ref[...], xn)
    kt_ref[0] = ht[:c_w].reshape(nh, d_c, tm)
    vt_ref[0] = ht[c_w:].reshape(nh, d_c, tm)
    lft_ref[0] = _log_sigmoid(_dot_nt(wft_ref[...], xn) + bf_ref[...])


def _odd_in_t(x, g, wq, wkvt, wft, b_f_col, *, B, T, tm, nh, d_c):
    n, d = x.shape
    nb = T // tm
    c_w = nh * d_c
    tok = lambda w: pl.BlockSpec((tm, w), lambda b, i: (b * nb + i, 0))
    kv = pl.BlockSpec((1, nh, d_c, tm), lambda b, i: (b, 0, 0, i))
    return pl.pallas_call(
        functools.partial(_odd_in_t_kernel, nh=nh, d_c=d_c),
        grid=(B, nb),
        in_specs=[tok(d), _resident(g.shape), _resident(wq.shape), _resident(wkvt.shape), _resident(wft.shape),
                  _resident(b_f_col.shape)],
        out_specs=[tok(c_w), kv, kv, pl.BlockSpec((1, nh, tm), lambda b, i: (b, 0, i))],
        out_shape=[jax.ShapeDtypeStruct((n, c_w), BF16), jax.ShapeDtypeStruct((B, nh, d_c, T), F32),
                   jax.ShapeDtypeStruct((B, nh, d_c, T), F32), jax.ShapeDtypeStruct((B, nh, T), F32)],
        compiler_params=_params("parallel", "parallel"),
        name="odd_in_t",
    )(x, g, wq, wkvt, wft, b_f_col)


def _gla_kernel(q_ref, kt_ref, v_ref, gate_ref, la_ref, lat_ref, s0_ref, g_ref, o_ref, sfin_ref, s_scr,
                *, bb, L, nh, dk, dv):
    c = pl.program_id(1)

    @pl.when(c == 0)
    def _():
        s_scr[...] = s0_ref[...]

    tw = q_ref.shape[1]
    nsub = tw // L
    w = nh * dk
    shift = L.bit_length() - 1
    assert L == 1 << shift and nh % 2 == 0
    tri = (lax.broadcasted_iota(jnp.int32, (L, L), 0) >= lax.broadcasted_iota(jnp.int32, (L, L), 1)).astype(F32)
    t0 = lax.broadcasted_iota(jnp.int32, (tw, tw), 0)
    t1 = lax.broadcasted_iota(jnp.int32, (tw, tw), 1)
    tri_t = (((t0 >> shift) == (t1 >> shift)) & (t0 <= t1)).astype(F32)
    lane_t = lax.broadcasted_iota(jnp.int32, (1, tw), 1)
    row_i = lax.broadcasted_iota(jnp.int32, (L, tw), 0)
    _, mine = _pair_masks(dk)
    g = g_ref[...]
    bs = range(bb)
    zero = jnp.zeros((), BF16)

    bt_all = jnp.dot(jnp.concatenate([lat_ref[b] for b in bs], axis=0), tri_t, precision=HIGHEST,
                     preferred_element_type=F32)
    k_in, k_st, a_col = [], [], []
    for b in bs:
        bt = bt_all[b * w:(b + 1) * w]
        kt = kt_ref[b]
        last = [bt[:, (cc + 1) * L - 1:(cc + 1) * L] for cc in range(nsub)]
        last_sel = last[0]
        for cc in range(1, nsub):
            last_sel = jnp.where(lane_t >= cc * L, last[cc], last_sel)
        k_in.append((kt * jnp.exp(-bt)).astype(BF16))
        k_st.append((kt * jnp.exp(last_sel - bt)).astype(BF16))
        a_col.append([jnp.exp(x) for x in last])
    q_in = {}
    for cc in range(nsub):
        rows = slice(cc * L, (cc + 1) * L)
        bc = jnp.dot(tri, jnp.concatenate([la_ref[b, rows] for b in bs], axis=1), precision=HIGHEST,
                     preferred_element_type=F32)
        for b in bs:
            q_in[b, cc] = (q_ref[b, rows] * jnp.exp(bc[:, b * w:(b + 1) * w])).astype(BF16)

    ccs = range(nsub)
    pairs = [(b, p) for b in bs for p in range(nh // 2)]
    heads = [(b, p, hh) for b, p in pairs for hh in range(2)]
    pair = lambda p: slice(p * 2 * dk, (p + 1) * 2 * dk)
    in_blk = [(lane_t >= cc * L) & (lane_t < (cc + 1) * L) for cc in ccs]
    causal = [in_blk[cc] & (lane_t - cc * L <= row_i) for cc in ccs]
    qm = {(b, p, hh, cc): jnp.where(mine[hh], q_in[b, cc][:, pair(p)], zero) for b, p, hh in heads for cc in ccs}
    sc = {(b, p): _dot(jnp.concatenate([qm[b, p, hh, cc] for cc in ccs for hh in range(2)], axis=0), k_in[b][pair(p)])
          for b, p in pairs}
    att = {(b, p, hh, cc): jnp.where(causal[cc], sc[b, p][(2 * cc + hh) * L:(2 * cc + hh + 1) * L], 0.0).astype(BF16)
           for b, p, hh in heads for cc in ccs}
    inc, o_in = {}, {}
    for b, p, hh in heads:
        h = 2 * p + hh
        ks = [jnp.where(in_blk[cc], k_st[b][h * dk:(h + 1) * dk], zero) for cc in ccs]
        r = _dot(jnp.concatenate([att[b, p, hh, cc] for cc in ccs] + ks, axis=0), v_ref[b, :, h * dv:(h + 1) * dv])
        for cc in ccs:
            o_in[b, h, cc] = r[cc * L:(cc + 1) * L]
            inc[b, h, cc] = r[nsub * L + cc * dk:nsub * L + (cc + 1) * dk]
    state = {(b, h): s_scr[b, h] for b in bs for h in range(nh)}
    for cc in ccs:
        rows = slice(cc * L, (cc + 1) * L)
        o = {}
        for b, p in pairs:
            s_pair = jnp.concatenate([state[b, 2 * p], state[b, 2 * p + 1]], axis=0).astype(BF16)
            carried = _dot(jnp.concatenate([qm[b, p, 0, cc], qm[b, p, 1, cc]], axis=0), s_pair)
            for hh in range(2):
                h = 2 * p + hh
                o[b, h] = o_in[b, h, cc] + carried[hh * L:(hh + 1) * L]
                state[b, h] = a_col[b][cc][h * dk:(h + 1) * dk] * state[b, h] + inc[b, h, cc]
        o = {k: x * lax.rsqrt(jnp.mean(x * x, axis=-1, keepdims=True) + EPS) * g for k, x in o.items()}
        for b in bs:
            o_ref[b, rows] = (jnp.concatenate([o[b, h] for h in range(nh)], axis=-1) * gate_ref[b, rows]).astype(BF16)
    for (b, h), x in state.items():
        s_scr[b, h] = x

    @pl.when(c == pl.num_programs(1) - 1)
    def _():
        sfin_ref[...] = s_scr[...]


def _gla(q, kt, v, gate, la, lat, s0, g, *, bb, L, nsub):
    B, T, a_qk = q.shape
    a_v = v.shape[-1]
    _, nh, dk, dv = s0.shape
    tw = L * nsub
    assert T % tw == 0

    def tok(w):
        return pl.BlockSpec((bb, tw, w), lambda i, c: (i, c, 0))

    tmin = pl.BlockSpec((bb, a_qk, tw), lambda i, c: (i, 0, c))
    st = pl.BlockSpec((bb, nh, dk, dv), lambda i, c: (i, 0, 0, 0))
    return pl.pallas_call(
        functools.partial(_gla_kernel, bb=bb, L=L, nh=nh, dk=dk, dv=dv),
        grid=(B // bb, T // tw),
        in_specs=[tok(a_qk), tmin, tok(a_v), tok(a_v), tok(a_qk), tmin, st,
                  pl.BlockSpec(g.shape, lambda i, c: (0, 0))],
        out_specs=[tok(a_v), st],
        out_shape=[jax.ShapeDtypeStruct((B, T, a_v), BF16), jax.ShapeDtypeStruct(s0.shape, F32)],
        scratch_shapes=[pltpu.VMEM((bb, nh, dk, dv), F32)],
        compiler_params=_params("parallel", "arbitrary"),
        name="gla",
    )(q, kt, v, gate, la, lat, s0, g)


def _bias_table_kernel(rel_ref, out_ref, *, chunk, win, n_rel):
    n_var, nh, rows, width = out_ref.shape
    back = win - chunk
    wide = width + back + rows
    assert wide % 128 == 0 and chunk & (chunk - 1) == 0
    u = lax.broadcasted_iota(jnp.int32, (n_rel, wide), 1)
    want = jnp.clip(back + rows - 1 - u, -MAX_REL, MAX_REL) + MAX_REL
    onehot = (lax.broadcasted_iota(jnp.int32, (n_rel, wide), 0) == want).astype(F32)
    vec = jnp.dot(rel_ref[...], onehot, precision=HIGHEST, preferred_element_type=F32)
    i = lax.broadcasted_iota(jnp.int32, (rows, width), 0)
    j = lax.broadcasted_iota(jnp.int32, (rows, width), 1)
    first = i - (i & (chunk - 1))
    for t in range(n_var):
        shift = back - t * rows if t < n_var - 1 else 0
        ok = (j + shift >= first) & (j + shift < first + win)
        for h in range(nh):
            x = jnp.broadcast_to(vec[h:h + 1, :], (rows, wide))
            x = pltpu.roll(x, (wide - (shift + rows - 1)) % wide, 1, stride=1, stride_axis=0)
            out_ref[t, h] = jnp.where(ok, x[:, :width] * LOG2E, -jnp.inf)


def _bias_table(rel_padded, *, chunk, win, rows):
    nh, n_rel = rel_padded.shape
    n_var = (win - chunk) // rows + 1
    return pl.pallas_call(
        functools.partial(_bias_table_kernel, chunk=chunk, win=win, n_rel=n_rel),
        out_shape=jax.ShapeDtypeStruct((n_var, nh, rows, win + rows - chunk), F32),
        compiler_params=_params(),
        name="band_bias_table",
    )(rel_padded)


def _pair_masks(d):
    lane = lax.broadcasted_iota(jnp.int32, (1, 2 * d), 1)
    return lane, [lane < d, lane >= d]


def _band_prompt_kernel(q_ref, k_ref, v_ref, bias_ref, o_ref, *, nh, d, back):
    n_var, _, tq, width = bias_ref.shape
    nsub = q_ref.shape[1] // tq
    lane, mine = _pair_masks(d)
    ones_lane = [d, 0]
    cols = [slice(p * 2 * d, (p + 1) * 2 * d) for p in range(nh // 2)]
    zero = jnp.zeros((), BF16)
    q, kw, vw, var = [], [], [], []
    for sb in range(nsub):
        c = pl.program_id(1) * nsub + sb
        start = pl.multiple_of(jnp.maximum(c * tq - back, 0), tq)
        var.append(jnp.minimum(c, n_var - 1))
        q.append(q_ref[0, sb * tq:(sb + 1) * tq, :])
        kw.append(k_ref[0, pl.ds(start, width), :])
        vw.append(v_ref[0, pl.ds(start, width), :])
    units = [(sb, p, hh) for sb in range(nsub) for p in range(nh // 2) for hh in range(2)]
    s = [_dot_nt(jnp.where(mine[hh], q[sb][:, cols[p]], zero), kw[sb][:, cols[p]]) + bias_ref[var[sb], 2 * p + hh]
         for sb, p, hh in units]
    m = [jnp.max(x, axis=-1, keepdims=True) for x in s]
    e = [jnp.exp2(x - mx).astype(BF16) for x, mx in zip(s, m)]
    r = [_dot(e[u], jnp.where(mine[hh], vw[sb][:, cols[p]], (lane == ones_lane[hh]).astype(BF16)))
         for u, (sb, p, hh) in enumerate(units)]
    r = [x / x[:, ones_lane[hh]:ones_lane[hh] + 1] for x, (sb, p, hh) in zip(r, units)]
    for sb in range(nsub):
        rs = r[sb * nh:(sb + 1) * nh]
        o_ref[0, sb * tq:(sb + 1) * tq, :] = jnp.concatenate(
            [jnp.where(mine[0], rs[2 * p], rs[2 * p + 1]) for p in range(nh // 2)], axis=-1).astype(BF16)


def _band_prompt(q, k, v, bias, *, nh, d, nsub=4):
    B, T, w = q.shape
    n_var, _, tq, width = bias.shape
    tq *= nsub
    assert T >= width and T % tq == 0
    kv = pl.BlockSpec((1, T, w), lambda b, c: (b, 0, 0))
    return pl.pallas_call(
        functools.partial(_band_prompt_kernel, nh=nh, d=d, back=width - tq // nsub),
        grid=(B, T // tq),
        in_specs=[pl.BlockSpec((1, tq, w), lambda b, c: (b, c, 0)), kv, kv, _resident(bias.shape)],
        out_specs=pl.BlockSpec((1, tq, w), lambda b, c: (b, c, 0)),
        out_shape=jax.ShapeDtypeStruct((B, T, w), BF16),
        compiler_params=_params("parallel", "arbitrary"),
        name="band_prompt",
    )(q, k, v, bias)


def _band_sample_kernel(q_ref, ckt_ref, cvt_ref, kn_ref, vn_ref, bias_ref, o_ref, *, nh, d, s_new):
    lc = ckt_ref.shape[3]
    q = q_ref[0]
    kn = kn_ref[0]
    vn = vn_ref[0]
    hs = range(nh)
    sl = [slice(h * d, (h + 1) * d) for h in hs]
    sc = [_dot(q[:, sl[h]], ckt_ref[0, h].astype(BF16)) + bias_ref[0, h][:s_new, :lc] for h in hs]
    sn = [_dot_nt(q[:, sl[h]], kn[:, sl[h]]) + bias_ref[0, h][:s_new, lc:lc + s_new] for h in hs]
    m = [jnp.maximum(jnp.max(sc[h], axis=-1, keepdims=True), jnp.max(sn[h], axis=-1, keepdims=True)) for h in hs]
    ec = [jnp.exp2(sc[h] - m[h]) for h in hs]
    en = [jnp.exp2(sn[h] - m[h]) for h in hs]
    l = [jnp.sum(ec[h], axis=-1, keepdims=True) + jnp.sum(en[h], axis=-1, keepdims=True) for h in hs]
    o = [(_dot_nt(ec[h].astype(BF16), cvt_ref[0, h].astype(BF16)) + _dot(en[h].astype(BF16), vn[:, sl[h]])) / l[h]
         for h in hs]
    o_ref[0] = jnp.concatenate(o, axis=-1).astype(BF16)


def _band_sample(q, ckt, cvt, kn, vn, bias, *, nh, d):
    B, s_new, w = q.shape
    lc = ckt.shape[3]
    n_var, _, rows, width = bias.shape
    assert s_new <= CHUNK and lc == width - rows
    new = pl.BlockSpec((1, s_new, w), lambda b: (b, 0, 0))
    old = pl.BlockSpec((1, nh, d, lc), lambda b: (b, 0, 0, 0))
    return pl.pallas_call(
        functools.partial(_band_sample_kernel, nh=nh, d=d, s_new=s_new),
        grid=(B,),
        in_specs=[new, old, old, new, new, pl.BlockSpec((1, nh, rows, width), lambda b: (n_var - 1, 0, 0, 0))],
        out_specs=new,
        out_shape=jax.ShapeDtypeStruct((B, s_new, w), BF16),
        compiler_params=_params("parallel"),
        name="band_sample",
    )(q, ckt, cvt, kn, vn, bias)


def _cumsum_lanes_kernel(x_ref, out_ref, *, blk, suffix):
    R, T = x_ref.shape
    src = lax.broadcasted_iota(jnp.int32, (blk, blk), 0)
    dst = lax.broadcasted_iota(jnp.int32, (blk, blk), 1)
    tri = ((src > dst) if suffix else (src <= dst)).astype(F32)
    carry = jnp.zeros((R, 1), F32)
    n = T // blk
    for i in (range(n - 1, -1, -1) if suffix else range(n)):
        xb = x_ref[:, i * blk:(i + 1) * blk]
        out_ref[:, i * blk:(i + 1) * blk] = jnp.dot(xb, tri, precision=HIGHEST, preferred_element_type=F32) + carry
        carry = carry + jnp.sum(xb, axis=1, keepdims=True)


def _cumsum_lanes(x, *, suffix):
    return pl.pallas_call(
        functools.partial(_cumsum_lanes_kernel, blk=256, suffix=suffix),
        out_shape=jax.ShapeDtypeStruct(x.shape, F32),
        compiler_params=_params(),
        name="cumsum_lanes",
    )(x)


def _fox_prompt_kernel(q_ref, kt_ref, vt_ref, ft_ref, o_ref, *, tq, tk, d):
    p = pl.program_id(1)
    T = q_ref.shape[1]
    assert tq == tk and d % 8 == 0
    nq = T // tq
    lane, mine = _pair_masks(d)
    q_fill = [((lane >= d) & (lane < d + 3)).astype(BF16), (lane < 3).astype(BF16)]
    den = [d, 0]
    sub = lax.broadcasted_iota(jnp.int32, (8, tk), 0)
    ones_blk = (sub == 0).astype(F32)
    pad_blk = jnp.zeros((d - 8, tk), F32)

    def stacked(x, blk, hh):
        return jnp.concatenate([x, blk, pad_blk] if hh == 0 else [blk, pad_blk, x], axis=0)

    diag_mask = (lax.broadcasted_iota(jnp.int32, (tk, tq), 0) <= lax.broadcasted_iota(jnp.int32, (tk, tq), 1))
    npair = q_ref.shape[2] // (2 * d)
    heads = [(pr, hh) for pr in range(npair) for hh in range(2)]
    units = [(qi, pr, hh) for qi in range(nq) for pr, hh in heads]
    qs = {(qi, pr, hh): jnp.where(mine[hh], q_ref[0, qi * tq:(qi + 1) * tq, pr * 2 * d:(pr + 1) * 2 * d], q_fill[hh])
          for qi, pr, hh in units}
    m = {u: jnp.full((1, tq), -jnp.inf, F32) for u in units}
    acc = {u: jnp.zeros((2 * d, tq), F32) for u in units}
    for kv in range(nq):
        ks = slice(kv * tk, (kv + 1) * tk)
        k_rows, v_all = {}, {}
        for pr, hh in heads:
            f = ft_ref[0, pl.ds(2 * (p * npair + pr) + hh, 1), ks] * LOG2E
            hi = f.astype(BF16).astype(F32)
            mid = (f - hi).astype(BF16).astype(F32)
            f_blk = jnp.where(sub == 0, -hi, jnp.where(sub == 1, -mid, jnp.where(sub == 2, -(f - hi - mid), 0.0)))
            k_rows[pr, hh] = stacked(kt_ref[0, 2 * pr + hh, :, ks], f_blk, hh).T.astype(BF16)
            v_all[pr, hh] = stacked(vt_ref[0, 2 * pr + hh, :, ks], ones_blk, hh).astype(BF16)
        live = [u for u in units if u[0] >= kv]
        s = {u: _dot_nt(k_rows[u[1:]], qs[u]) for u in live}
        for u in live:
            if u[0] == kv:
                s[u] = jnp.where(diag_mask, s[u], -jnp.inf)
        m_new = {u: jnp.maximum(m[u], jnp.max(s[u], axis=0, keepdims=True)) for u in live}
        e = {u: jnp.exp2(s[u] - m_new[u]).astype(BF16) for u in live}
        for u in live:
            acc[u] = jnp.exp2(m[u] - m_new[u]) * acc[u] + _dot(v_all[u[1:]], e[u])
            m[u] = m_new[u]
        for pr in range(npair):
            out = [(acc[kv, pr, hh] / acc[kv, pr, hh][den[hh]:den[hh] + 1, :]).T for hh in range(2)]
            o_ref[0, kv * tq:(kv + 1) * tq, pr * 2 * d:(pr + 1) * 2 * d] = jnp.where(mine[0], out[0],
                                                                                   out[1]).astype(BF16)


def _fox_prompt(q, kt, vt, ft, *, tq, tk, npair=2):
    B, T, w = q.shape
    _, nh, d, _ = kt.shape
    kv = pl.BlockSpec((1, 2 * npair, d, T), lambda b, p: (b, p, 0, 0))
    qo = pl.BlockSpec((1, T, 2 * npair * d), lambda b, p: (b, 0, p))
    return pl.pallas_call(
        functools.partial(_fox_prompt_kernel, tq=tq, tk=tk, d=d),
        grid=(B, nh // (2 * npair)),
        in_specs=[qo, kv, kv, pl.BlockSpec((1, nh, T), lambda b, p: (b, 0, 0))],
        out_specs=qo,
        out_shape=jax.ShapeDtypeStruct((B, T, w), BF16),
        compiler_params=_params("parallel", "arbitrary"),
        name="fox_prompt",
    )(q, kt, vt, ft)


def _fox_sample_kernel(q_ref, kt_ref, vt_ref, ct_ref, kn_ref, vn_ref, lf_ref, o_ref, fqt_scr, *, hg, d, s_new):
    g = pl.program_id(1)
    nh = lf_ref.shape[2]
    tri = (lax.broadcasted_iota(jnp.int32, (s_new, s_new), 0)
           >= lax.broadcasted_iota(jnp.int32, (s_new, s_new), 1))
    eye = (lax.broadcasted_iota(jnp.int32, (nh, nh), 0) == lax.broadcasted_iota(jnp.int32, (nh, nh), 1)).astype(F32)
    fq = jnp.dot(tri.astype(F32), lf_ref[0], precision=HIGHEST, preferred_element_type=F32)
    fqt_scr[...] = lax.dot_general(eye, fq, NT, precision=HIGHEST, preferred_element_type=F32)
    outs = []
    for hh in range(hg):
        sl = slice(hh * d, (hh + 1) * d)
        head = g * hg + hh
        qh = q_ref[0, 0, :, sl]
        sc = _dot(qh, kt_ref[0, hh].astype(BF16)) + ct_ref[0, pl.ds(head, 1), :]
        sn = _dot_nt(qh, kn_ref[0, 0, :, sl]) - fqt_scr[pl.ds(head, 1), :]
        sn = jnp.where(tri, sn, -jnp.inf)
        m = jnp.maximum(jnp.max(sc, axis=-1, keepdims=True), jnp.max(sn, axis=-1, keepdims=True))
        ec = jnp.exp(sc - m)
        en = jnp.exp(sn - m)
        l = jnp.sum(ec, axis=-1, keepdims=True) + jnp.sum(en, axis=-1, keepdims=True)
        o = _dot_nt(ec.astype(BF16), vt_ref[0, hh].astype(BF16)) + _dot(en.astype(BF16), vn_ref[0, 0, :, sl])
        outs.append(o / l)
    o_ref[0, 0] = jnp.concatenate(outs, axis=-1).astype(BF16)


def _fox_sample(q, kt, vt, ct, kn, vn, lf, *, hg):
    B, ng, s_new, gw = q.shape
    _, nh, d, lc = kt.shape
    new = pl.BlockSpec((1, 1, s_new, gw), lambda b, g: (b, g, 0, 0))
    old = pl.BlockSpec((1, hg, d, lc), lambda b, g: (b, g, 0, 0))
    return pl.pallas_call(
        functools.partial(_fox_sample_kernel, hg=hg, d=d, s_new=s_new),
        grid=(B, ng),
        in_specs=[new, old, old, pl.BlockSpec((1, nh, lc), lambda b, g: (b, 0, 0)), new, new,
                  pl.BlockSpec((1, s_new, nh), lambda b, g: (b, 0, 0))],
        out_specs=new,
        out_shape=jax.ShapeDtypeStruct(q.shape, BF16),
        scratch_shapes=[pltpu.VMEM((nh, s_new), F32)],
        compiler_params=_params("parallel", "arbitrary"),
        name="fox_sample",
    )(q, kt, vt, ct, kn, vn, lf)


def _trunk(x, state, wts, dims, *, tm, gla_bb):
    B, T, D = x.shape
    n = B * T
    depth = wts["norm_g"].shape[0]
    h_a, dk_a, dv_a, h_b, d_b, h_c, d_c = dims
    a_qk, a_v, b_w, c_w = h_a * dk_a, h_a * dv_a, h_b * d_b, h_c * d_c
    xs = x.reshape(n, D)
    new_states = []
    pending = ()
    r3 = lambda t: t.reshape(B, T, t.shape[-1])
    ffn_w = wts["ffn_bf16"]

    def ffn(xs, g, key, proj, final_g=None):
        nxt = (key[0], 1) if key[1] == 0 else (key[0] + 1, 0)
        if state is not None:
            return _ffn_streamed(xs, g, *ffn_w[key], proj, final_g, n_chunks=2)
        if nxt[0] >= depth:
            return _ffn(xs, g, *ffn_w[key], proj, final_g, tm=tm)
        xs, *ffn_w[nxt] = _ffn(xs, g, *ffn_w[key], proj, final_g, (wts["w_gu"], wts["w_down"], nxt), tm=tm)
        return xs

    for l in range(depth):
        i = l // 2
        g = wts["norm_g"][l]
        xs = ffn(xs, g[0:1], (l, 0), pending)
        if l % 2 == 0:
            prompt = state is None
            keep = min(BAND_CHUNKS * CHUNK, T)
            qa, kat, va, gate, la, lat, bq, bk, bv, *kv32 = _even_in(
                xs, g[1:2], {k: v[i] for k, v in wts["even"].items()}, tm=tm, t_split=(B, T) if prompt else (1, n),
                a_qk=a_qk, a_v=a_v, b_w=b_w, dk_a=dk_a, d_b=d_b, kv32=not prompt)
            if not prompt:
                kat, lat = (t.reshape(a_qk, B, T).transpose(1, 0, 2) for t in (kat, lat))
            nk, nv = _band_tail(xs, g[1:2], wts["even_kvt"][i], B=B, T=T, keep=keep) if prompt else kv32
            s0 = jnp.zeros((B, h_a, dk_a, dv_a), F32) if prompt else state[l][0]
            blk = min(CHUNK, T)
            o_a, s_fin = _gla(r3(qa), kat, r3(va), r3(gate), r3(la), lat, s0, wts["gla_g"][i],
                              bb=gla_bb, L=blk, nsub=2 if T % (2 * blk) == 0 else 1)
            if prompt:
                o_b = _band_prompt(r3(bq), r3(bk), r3(bv), wts["band_bias"][i], nh=h_b, d=d_b)
                nk, nv = (t.reshape(B, h_b, d_b, keep).transpose(0, 3, 1, 2) for t in (nk, nv))
            else:
                ck, cv = state[l][1], state[l][2]
                o_b = _band_sample(r3(bq), ck.transpose(0, 2, 3, 1), cv.transpose(0, 2, 3, 1), r3(bk), r3(bv),
                                   wts["band_bias"][i], nh=h_b, d=d_b)
                nk, nv = (t.reshape(B, T, h_b, d_b) for t in (nk, nv))
            new_states.append((s_fin, nk, nv))
            pending = ((o_a.reshape(n, a_v), wts["even_out_a"][i]), (o_b.reshape(n, b_w), wts["even_out_b"][i]))
        elif state is None:
            q, kt, vt, lft = _odd_in_t(xs, g[1:2], wts["odd_q"][i], wts["odd_kvt"][i], wts["odd_ft"][i],
                                       wts["b_f_col"][i], B=B, T=T, tm=tm, nh=h_c, d_c=d_c)
            ft = _cumsum_lanes(lft.reshape(B * h_c, T), suffix=False).reshape(B, h_c, T)
            o = _fox_prompt(r3(q), kt, vt, ft, tq=512, tk=512)
            new_states.append((kt.transpose(0, 3, 1, 2), vt.transpose(0, 3, 1, 2), lft.transpose(0, 2, 1)))
            pending = ((o.reshape(n, c_w), wts["odd_out"][i]),)
        else:
            q, k, v, k32, v32, lf = _odd_in(xs, g[1:2], wts["odd_main"][i], wts["odd_f"][i], wts["b_f"][i],
                                             tm=tm, c_w=c_w, d_c=d_c, h_c=h_c)
            ck, cv, clf = state[l]
            lc = ck.shape[1]
            hg = 8
            ct = _cumsum_lanes(clf.astype(F32).transpose(0, 2, 1).reshape(B * h_c, lc), suffix=True)
            grp = lambda t: t.reshape(B, T, h_c // hg, hg * d_c).transpose(0, 2, 1, 3)
            o = _fox_sample(grp(q), ck.transpose(0, 2, 3, 1), cv.transpose(0, 2, 3, 1), ct.reshape(B, h_c, lc),
                            grp(k), grp(v), r3(lf), hg=hg)
            o = o.transpose(0, 2, 1, 3).reshape(n, c_w)
            new_states.append((r3(k32).reshape(B, T, h_c, d_c), r3(v32).reshape(B, T, h_c, d_c), r3(lf)))
            pending = ((o, wts["odd_out"][i]),)
        last = l == depth - 1
        xs = ffn(xs, g[2:3], (l, 1), pending, wts["final_g"] if last else None)
        pending = ()
    return xs.reshape(B, T, D), new_states


def kernel(x_prompt, x_sample, state_gla, cache_band_k, cache_band_v, cache_fox_k, cache_fox_v, cache_fox_logf,
           norm_g, ffn_w_gu, ffn_w_down, even_w_in, gla_w_alpha_up, gla_b_alpha, gla_norm_g, band_rel_bias,
           even_w_out, odd_w_in, fox_b_f, odd_w_out, final_norm_g):
    depth = norm_g.shape[0]
    _, _, h_a, dk_a, dv_a = state_gla.shape
    h_b, d_b = cache_band_k.shape[-2:]
    h_c, d_c = cache_fox_k.shape[-2:]
    r_a = gla_w_alpha_up.shape[1]
    a_qk, a_v, b_w, c_w = h_a * dk_a, h_a * dv_a, h_b * d_b, h_c * d_c
    assert cache_band_k.shape[2] == BAND_CHUNKS * CHUNK and band_rel_bias.shape[-1] == 2 * MAX_REL + 1

    r0 = 2 * a_qk + 2 * a_v
    n_rel = 384
    rel_padded = jnp.pad(band_rel_bias, ((0, 0), (0, 0), (0, n_rel - band_rel_bias.shape[-1])))
    win = (BAND_CHUNKS + 1) * CHUNK
    odd_t = odd_w_in.transpose(0, 2, 1)
    even_t = even_w_in.transpose(0, 2, 1)
    wts = {
        "norm_g": norm_g,
        "w_gu": ffn_w_gu,
        "w_down": ffn_w_down,
        "ffn_bf16": {(0, 0): [ffn_w_gu[0, 0].astype(BF16), ffn_w_down[0, 0].astype(BF16)]},
        "even": {
            "main": jnp.concatenate([even_w_in[:, :, :r0], even_w_in[:, :, r0 + r_a:]], axis=-1).astype(BF16),
            "r": even_w_in[:, :, r0:r0 + r_a].astype(BF16),
            "up": gla_w_alpha_up.astype(BF16),
            "b_alpha": gla_b_alpha[:, None, :],
        },
        "even_kvt": even_t[:, r0 + r_a + b_w:].astype(BF16),
        "gla_g": gla_norm_g[:, None, :],
        "band_bias": [_bias_table(rel_padded[i], chunk=CHUNK, win=win, rows=2 * CHUNK)
                      for i in range(rel_padded.shape[0])],
        "even_out_a": even_w_out[:, :a_v].astype(BF16),
        "even_out_b": even_w_out[:, a_v:].astype(BF16),
        "odd_main": odd_w_in[:, :, :3 * c_w].astype(BF16),
        "odd_f": odd_w_in[:, :, 3 * c_w:].astype(BF16),
        "odd_q": odd_w_in[:, :, :c_w].astype(BF16),
        "odd_kvt": odd_t[:, c_w:3 * c_w].astype(BF16),
        "odd_ft": odd_t[:, 3 * c_w:].astype(BF16),
        "b_f": fox_b_f[:, None, :],
        "b_f_col": fox_b_f[:, :, None],
        "odd_out": odd_w_out.astype(BF16),
        "final_g": final_norm_g[None, :],
    }
    dims = (h_a, dk_a, dv_a, h_b, d_b, h_c, d_c)

    sample_states = []
    for l in range(depth):
        i = l // 2
        if l % 2 == 0:
            sample_states.append((state_gla[i], cache_band_k[i], cache_band_v[i]))
        else:
            sample_states.append((cache_fox_k[i], cache_fox_v[i], cache_fox_logf[i]))

    y_prompt, ns_p = _trunk(x_prompt, None, wts, dims, tm=512, gla_bb=8)
    y_sample, ns_s = _trunk(x_sample, sample_states, wts, dims, tm=256, gla_bb=8)

    ev = range(0, depth, 2)
    od = range(1, depth, 2)
    stack = lambda ns, layers, j: jnp.stack([ns[l][j] for l in layers])
    return (y_prompt, y_sample,
            stack(ns_p, ev, 0), stack(ns_p, ev, 1), stack(ns_p, ev, 2),
            stack(ns_p, od, 0), stack(ns_p, od, 1), stack(ns_p, od, 2),
            stack(ns_s, ev, 0), stack(ns_s, ev, 1), stack(ns_s, ev, 2),
            stack(ns_s, od, 0), stack(ns_s, od, 1), stack(ns_s, od, 2))
```

```python
import functools

import jax
import jax.numpy as jnp
from jax import lax
from jax.experimental import pallas as pl
from jax.experimental.pallas import tpu as pltpu

F32 = jnp.float32
BF16 = jnp.bfloat16
HIGHEST = lax.Precision.HIGHEST
NT = (((1,), (1,)), ((), ()))
TN = (((0,), (0,)), ((), ()))

EPS = 1e-6
CHUNK = 64
BAND_CHUNKS = 8
MAX_REL = 128
GATE_NORM_A = 16.0
LOG2E = 1.4426950408889634
VMEM_LIMIT_BYTES = 56 * 1024 * 1024


def _params(*sem):
    return pltpu.CompilerParams(dimension_semantics=sem, vmem_limit_bytes=VMEM_LIMIT_BYTES)


def _resident(shape):
    nd = len(shape)
    return pl.BlockSpec(shape, lambda *_: (0,) * nd, pipeline_mode=pl.Buffered(1))


def _rows(tm, width):
    return pl.BlockSpec((tm, width), lambda i: (i, 0))


def _rms(x, g):
    ms = jnp.mean(x * x, axis=-1, keepdims=True)
    return x * lax.rsqrt(ms + EPS) * g


def _log_sigmoid(x):
    return jnp.minimum(x, 0.0) - jnp.log1p(jnp.exp(-jnp.abs(x)))


def _dot(a, b):
    return jnp.dot(a, b, preferred_element_type=F32)


def _dot_nt(a, b):
    return lax.dot_general(a, b, NT, preferred_element_type=F32)


def _ffn_kernel(*refs, n_proj, has_final, casts, d_ff):
    it = iter(refs)
    x_ref = next(it)
    proj = [(next(it), next(it)) for _ in range(n_proj)]
    g_ref, wgu_ref, wd_ref = next(it), next(it), next(it)
    gf_ref = next(it) if has_final else None
    cast_src = [next(it) for _ in range(casts)]
    out_ref = next(it)
    for src in cast_src:
        next(it)[...] = src[...].astype(BF16)

    x = x_ref[...]
    for o_ref, w_ref in proj:
        x = x + _dot(o_ref[...], w_ref[...])
    xn = _rms(x, g_ref[...]).astype(BF16)
    h = _dot(xn, wgu_ref[...])
    a = (jax.nn.silu(h[:, :d_ff]) * h[:, d_ff:]).astype(BF16)
    y = x + 0.5 * _dot(a, wd_ref[...])
    if has_final:
        y = _rms(y, gf_ref[...])
    out_ref[...] = y


def _slab_cast(src, lead, rows, steps, per=1, first=0):
    slab = rows * per // steps
    width = src.shape[-1]
    assert rows * per % steps == 0 and slab % 16 == 0
    return (src,
            pl.BlockSpec((None,) * len(lead) + (slab, width), lambda i: tuple(lead) + (first + i // per, 0)),
            pl.BlockSpec((slab, width), lambda i: (i // per, 0)),
            jax.ShapeDtypeStruct((rows, width), BF16))


def _ffn(x, g, wgu, wd, proj=(), final_g=None, casts=(), *, tm):
    n, d = x.shape
    d_ff = wd.shape[0]
    steps = n // tm
    args, specs = [x], [_rows(tm, d)]
    for o, w in proj:
        args += [o, w]
        specs += [_rows(tm, o.shape[1]), _resident(w.shape)]
    args += [g, wgu, wd]
    specs += [_resident(g.shape), _resident(wgu.shape), _resident(wd.shape)]
    if final_g is not None:
        args.append(final_g)
        specs.append(_resident(final_g.shape))
    out_specs, out_shapes = [_rows(tm, d)], [jax.ShapeDtypeStruct((n, d), F32)]
    for src, in_spec, out_spec, out_shape in casts:
        args.append(src)
        specs.append(in_spec)
        out_specs.append(out_spec)
        out_shapes.append(out_shape)
    return pl.pallas_call(
        functools.partial(_ffn_kernel, n_proj=len(proj), has_final=final_g is not None, casts=len(casts), d_ff=d_ff),
        grid=(steps,),
        in_specs=specs,
        out_specs=out_specs,
        out_shape=out_shapes,
        compiler_params=_params("arbitrary" if casts else "parallel"),
        name="ffn",
    )(*args)


def _ffn_streamed_kernel(*refs, n_proj, has_final):
    it = iter(refs)
    x_ref = next(it)
    proj = [(next(it), next(it)) for _ in range(n_proj)]
    g_ref, wg_ref, wu_ref, wd_ref = next(it), next(it), next(it), next(it)
    gf_ref = next(it) if has_final else None
    out_ref, x_scr, xn_scr, acc_scr = next(it), next(it), next(it), next(it)
    c = pl.program_id(0)

    @pl.when(c == 0)
    def _():
        x = x_ref[...]
        for o_ref, w_ref in proj:
            x = x + _dot(o_ref[...], w_ref[...])
        x_scr[...] = x
        xn_scr[...] = _rms(x, g_ref[...]).astype(BF16)
        acc_scr[...] = jnp.zeros(acc_scr.shape, F32)

    xn = xn_scr[...]
    a = (jax.nn.silu(_dot(xn, wg_ref[...])) * _dot(xn, wu_ref[...])).astype(BF16)
    acc_scr[...] += _dot(a, wd_ref[...])

    @pl.when(c == pl.num_programs(0) - 1)
    def _():
        y = x_scr[...] + 0.5 * acc_scr[...]
        if has_final:
            y = _rms(y, gf_ref[...])
        out_ref[...] = y


def _ffn_streamed(x, g, wgu, wd, proj=(), final_g=None, *, n_chunks):
    n, d = x.shape
    d_ff = wd.shape[0]
    ck = d_ff // n_chunks
    assert d_ff % n_chunks == 0 and ck % 128 == 0
    whole = lambda a: pl.BlockSpec(a.shape, lambda c: (0,) * a.ndim)
    args, specs = [x], [whole(x)]
    for o, w in proj:
        args += [o, w]
        specs += [whole(o), whole(w)]
    args += [g, wgu, wgu, wd]
    specs += [whole(g), pl.BlockSpec((d, ck), lambda c: (0, c)), pl.BlockSpec((d, ck), lambda c: (0, n_chunks + c)),
              pl.BlockSpec((ck, d), lambda c: (c, 0))]
    if final_g is not None:
        args.append(final_g)
        specs.append(whole(final_g))
    return pl.pallas_call(
        functools.partial(_ffn_streamed_kernel, n_proj=len(proj), has_final=final_g is not None),
        grid=(n_chunks,),
        in_specs=specs,
        out_specs=pl.BlockSpec((n, d), lambda c: (0, 0)),
        out_shape=jax.ShapeDtypeStruct((n, d), F32),
        scratch_shapes=[pltpu.VMEM((n, d), F32), pltpu.VMEM((n, d), BF16), pltpu.VMEM((n, d), F32)],
        compiler_params=_params("arbitrary"),
        name="ffn_streamed",
    )(*args)


def _even_in_kernel(x_ref, g_ref, w_ref, wr_ref, wup_ref, ba_ref,
                    qa_ref, kat_ref, va_ref, gate_ref, la_ref, lat_ref, bq_ref, bk_ref, bv_ref, *kv32_refs,
                    a_qk, a_v, b_w, dk_a, d_b):
    xn = _rms(x_ref[...], g_ref[...]).astype(BF16)
    h = _dot(xn, w_ref[...])
    o = 0
    qa_ref[...] = h[:, o:o + a_qk] * (dk_a ** -0.5)
    o += a_qk
    kat_ref[...] = h[:, o:o + a_qk].T
    o += a_qk
    va_ref[...] = h[:, o:o + a_v].astype(BF16)
    o += a_v
    gate_ref[...] = jax.nn.silu(h[:, o:o + a_v])
    o += a_v
    bq_ref[...] = (h[:, o:o + b_w] * (LOG2E * d_b ** -0.5)).astype(BF16)
    o += b_w
    bk = h[:, o:o + b_w]
    o += b_w
    bv = h[:, o:o + b_w]
    bk_ref[...] = bk.astype(BF16)
    bv_ref[...] = bv.astype(BF16)
    if kv32_refs:
        kv32_refs[0][...] = bk
        kv32_refs[1][...] = bv
    r = _dot(xn, wr_ref[...]).astype(BF16)
    la = _log_sigmoid(_dot(r, wup_ref[...]) + ba_ref[...]) / GATE_NORM_A
    la_ref[...] = la
    lat_ref[...] = la.T


def _even_in(x, g, w, *, tm, t_split, a_qk, a_v, b_w, dk_a, d_b, kv32):
    n, d = x.shape
    bo, to = t_split
    nb = to // tm
    assert bo * to == n and to % tm == 0
    tmin = (pl.BlockSpec((None, a_qk, tm), lambda i: (i // nb, 0, i % nb)), jax.ShapeDtypeStruct((bo, a_qk, to), F32))
    tok = lambda width, dt: (_rows(tm, width), jax.ShapeDtypeStruct((n, width), dt))
    outs = [tok(a_qk, F32), tmin, tok(a_v, BF16), tok(a_v, F32), tok(a_qk, F32), tmin,
            tok(b_w, BF16), tok(b_w, BF16), tok(b_w, BF16)]
    if kv32:
        outs += [tok(b_w, F32), tok(b_w, F32)]
    args = [x, g] + [w[k] for k in ("main", "r", "up", "b_alpha")]
    return pl.pallas_call(
        functools.partial(_even_in_kernel, a_qk=a_qk, a_v=a_v, b_w=b_w, dk_a=dk_a, d_b=d_b),
        grid=(n // tm,),
        in_specs=[_rows(tm, d)] + [_resident(a.shape) for a in args[1:]],
        out_specs=[s for s, _ in outs],
        out_shape=[t for _, t in outs],
        compiler_params=_params("parallel"),
        name="even_in",
    )(*args)


def _band_tail_kernel(x_ref, g_ref, wkvt_ref, kt_ref, vt_ref):
    xn = _rms(x_ref[...], g_ref[...]).astype(BF16)
    kvt = _dot_nt(wkvt_ref[...], xn)
    b_w = kt_ref.shape[1]
    kt_ref[0] = kvt[:b_w]
    vt_ref[0] = kvt[b_w:]


def _band_tail(x, g, w_kvt, *, B, T, keep):
    n, d = x.shape
    b_w = w_kvt.shape[0] // 2
    assert T % keep == 0
    nb = T // keep
    out = pl.BlockSpec((1, b_w, keep), lambda b: (b, 0, 0))
    return pl.pallas_call(
        _band_tail_kernel,
        grid=(B,),
        in_specs=[pl.BlockSpec((keep, d), lambda b: (b * nb + nb - 1, 0)), _resident(g.shape), _resident(w_kvt.shape)],
        out_specs=[out, out],
        out_shape=[jax.ShapeDtypeStruct((B, b_w, keep), F32)] * 2,
        compiler_params=_params("parallel"),
        name="band_tail",
    )(x, g, w_kvt)


def _odd_in_kernel(x_ref, g_ref, w_ref, wf_ref, bf_ref, q_ref, k_ref, v_ref, k32_ref, v32_ref, lf_ref, *, c_w, d_c):
    xn = _rms(x_ref[...], g_ref[...]).astype(BF16)
    h = _dot(xn, w_ref[...])
    q_ref[...] = (h[:, :c_w] * (d_c ** -0.5)).astype(BF16)
    k = h[:, c_w:2 * c_w]
    v = h[:, 2 * c_w:]
    k32_ref[...] = k
    v32_ref[...] = v
    k_ref[...] = k.astype(BF16)
    v_ref[...] = v.astype(BF16)
    lf_ref[...] = _log_sigmoid(_dot(xn, wf_ref[...]) + bf_ref[...])


def _odd_in(x, g, w_main, w_f, b_f, *, tm, c_w, d_c, h_c):
    n, d = x.shape
    outs = [(c_w, BF16), (c_w, BF16), (c_w, BF16), (c_w, F32), (c_w, F32), (h_c, F32)]
    return pl.pallas_call(
        functools.partial(_odd_in_kernel, c_w=c_w, d_c=d_c),
        grid=(n // tm,),
        in_specs=[_rows(tm, d), _resident(g.shape), _resident(w_main.shape), _resident(w_f.shape),
                  _resident(b_f.shape)],
        out_specs=[_rows(tm, w) for w, _ in outs],
        out_shape=[jax.ShapeDtypeStruct((n, w), dt) for w, dt in outs],
        compiler_params=_params("parallel"),
        name="odd_in",
    )(x, g, w_main, w_f, b_f)


def _odd_in_t_kernel(x_ref, g_ref, wq_ref, wkvt_ref, wft_ref, bf_ref, q_ref, kt_ref, vt_ref, lft_ref, *, nh, d_c):
    xn = _rms(x_ref[...], g_ref[...]).astype(BF16)
    tm = xn.shape[0]
    c_w = nh * d_c
    q_ref[...] = (_dot(xn, wq_ref[...]) * (LOG2E * d_c ** -0.5)).astype(BF16)
    ht = _dot_nt(wkvt_ref[...], xn)
    kt_ref[0] = ht[:c_w].reshape(nh, d_c, tm)
    vt_ref[0] = ht[c_w:].reshape(nh, d_c, tm)
    lft_ref[0] = _log_sigmoid(_dot_nt(wft_ref[...], xn) + bf_ref[...])


def _odd_in_t(x, g, wq, wkvt, wft, b_f_col, *, B, T, tm, nh, d_c):
    n, d = x.shape
    nb = T // tm
    c_w = nh * d_c
    tok = lambda w: pl.BlockSpec((tm, w), lambda b, i: (b * nb + i, 0))
    kv = pl.BlockSpec((1, nh, d_c, tm), lambda b, i: (b, 0, 0, i))
    return pl.pallas_call(
        functools.partial(_odd_in_t_kernel, nh=nh, d_c=d_c),
        grid=(B, nb),
        in_specs=[tok(d), _resident(g.shape), _resident(wq.shape), _resident(wkvt.shape), _resident(wft.shape),
                  _resident(b_f_col.shape)],
        out_specs=[tok(c_w), kv, kv, pl.BlockSpec((1, nh, tm), lambda b, i: (b, 0, i))],
        out_shape=[jax.ShapeDtypeStruct((n, c_w), BF16), jax.ShapeDtypeStruct((B, nh, d_c, T), F32),
                   jax.ShapeDtypeStruct((B, nh, d_c, T), F32), jax.ShapeDtypeStruct((B, nh, T), F32)],
        compiler_params=_params("parallel", "parallel"),
        name="odd_in_t",
    )(x, g, wq, wkvt, wft, b_f_col)


def _gla_kernel(q_ref, kt_ref, v_ref, gate_ref, la_ref, lat_ref, s0_ref, g_ref, o_ref, sfin_ref, s_scr,
                *, bb, L, nh, dk, dv):
    c = pl.program_id(1)

    @pl.when(c == 0)
    def _():
        s_scr[...] = s0_ref[...]

    tw = q_ref.shape[1]
    nsub = tw // L
    w = nh * dk
    shift = L.bit_length() - 1
    assert L == 1 << shift and nh % 2 == 0
    tri = (lax.broadcasted_iota(jnp.int32, (L, L), 0) >= lax.broadcasted_iota(jnp.int32, (L, L), 1)).astype(F32)
    t0 = lax.broadcasted_iota(jnp.int32, (tw, tw), 0)
    t1 = lax.broadcasted_iota(jnp.int32, (tw, tw), 1)
    tri_t = (((t0 >> shift) == (t1 >> shift)) & (t0 <= t1)).astype(F32)
    lane_t = lax.broadcasted_iota(jnp.int32, (1, tw), 1)
    row_i = lax.broadcasted_iota(jnp.int32, (L, tw), 0)
    _, mine = _pair_masks(dk)
    g = g_ref[...]
    bs = range(bb)
    zero = jnp.zeros((), BF16)

    bt_all = jnp.dot(jnp.concatenate([lat_ref[b] for b in bs], axis=0), tri_t, precision=HIGHEST,
                     preferred_element_type=F32)
    k_in, k_st, a_col = [], [], []
    for b in bs:
        bt = bt_all[b * w:(b + 1) * w]
        kt = kt_ref[b]
        last = [bt[:, (cc + 1) * L - 1:(cc + 1) * L] for cc in range(nsub)]
        last_sel = last[0]
        for cc in range(1, nsub):
            last_sel = jnp.where(lane_t >= cc * L, last[cc], last_sel)
        k_in.append((kt * jnp.exp(-bt)).astype(BF16))
        k_st.append((kt * jnp.exp(last_sel - bt)).astype(BF16))
        a_col.append([jnp.exp(x) for x in last])
    q_in = {}
    for cc in range(nsub):
        rows = slice(cc * L, (cc + 1) * L)
        bc = jnp.dot(tri, jnp.concatenate([la_ref[b, rows] for b in bs], axis=1), precision=HIGHEST,
                     preferred_element_type=F32)
        for b in bs:
            q_in[b, cc] = (q_ref[b, rows] * jnp.exp(bc[:, b * w:(b + 1) * w])).astype(BF16)

    ccs = range(nsub)
    pairs = [(b, p) for b in bs for p in range(nh // 2)]
    heads = [(b, p, hh) for b, p in pairs for hh in range(2)]
    pair = lambda p: slice(p * 2 * dk, (p + 1) * 2 * dk)
    in_blk = [(lane_t >= cc * L) & (lane_t < (cc + 1) * L) for cc in ccs]
    causal = [in_blk[cc] & (lane_t - cc * L <= row_i) for cc in ccs]
    qm = {(b, p, hh, cc): jnp.where(mine[hh], q_in[b, cc][:, pair(p)], zero) for b, p, hh in heads for cc in ccs}
    sc = {(b, p): _dot(jnp.concatenate([qm[b, p, hh, cc] for cc in ccs for hh in range(2)], axis=0), k_in[b][pair(p)])
          for b, p in pairs}
    att = {(b, p, hh, cc): jnp.where(causal[cc], sc[b, p][(2 * cc + hh) * L:(2 * cc + hh + 1) * L], 0.0).astype(BF16)
           for b, p, hh in heads for cc in ccs}
    inc, o_in = {}, {}
    for b, p, hh in heads:
        h = 2 * p + hh
        ks = [jnp.where(in_blk[cc], k_st[b][h * dk:(h + 1) * dk], zero) for cc in ccs]
        r = _dot(jnp.concatenate([att[b, p, hh, cc] for cc in ccs] + ks, axis=0), v_ref[b, :, h * dv:(h + 1) * dv])
        for cc in ccs:
            o_in[b, h, cc] = r[cc * L:(cc + 1) * L]
            inc[b, h, cc] = r[nsub * L + cc * dk:nsub * L + (cc + 1) * dk]
    state = {(b, h): s_scr[b, h] for b in bs for h in range(nh)}
    for cc in ccs:
        rows = slice(cc * L, (cc + 1) * L)
        o = {}
        for b, p in pairs:
            s_pair = jnp.concatenate([state[b, 2 * p], state[b, 2 * p + 1]], axis=0).astype(BF16)
            carried = _dot(jnp.concatenate([qm[b, p, 0, cc], qm[b, p, 1, cc]], axis=0), s_pair)
            for hh in range(2):
                h = 2 * p + hh
                o[b, h] = o_in[b, h, cc] + carried[hh * L:(hh + 1) * L]
                state[b, h] = a_col[b][cc][h * dk:(h + 1) * dk] * state[b, h] + inc[b, h, cc]
        o = {k: x * lax.rsqrt(jnp.mean(x * x, axis=-1, keepdims=True) + EPS) * g for k, x in o.items()}
        for b in bs:
            o_ref[b, rows] = (jnp.concatenate([o[b, h] for h in range(nh)], axis=-1) * gate_ref[b, rows]).astype(BF16)
    for (b, h), x in state.items():
        s_scr[b, h] = x

    @pl.when(c == pl.num_programs(1) - 1)
    def _():
        sfin_ref[...] = s_scr[...]


def _gla(q, kt, v, gate, la, lat, s0, g, *, bb, L, nsub):
    B, T, a_qk = q.shape
    a_v = v.shape[-1]
    _, nh, dk, dv = s0.shape
    tw = L * nsub
    assert T % tw == 0

    def tok(w):
        return pl.BlockSpec((bb, tw, w), lambda i, c: (i, c, 0))

    tmin = pl.BlockSpec((bb, a_qk, tw), lambda i, c: (i, 0, c))
    st = pl.BlockSpec((bb, nh, dk, dv), lambda i, c: (i, 0, 0, 0))
    return pl.pallas_call(
        functools.partial(_gla_kernel, bb=bb, L=L, nh=nh, dk=dk, dv=dv),
        grid=(B // bb, T // tw),
        in_specs=[tok(a_qk), tmin, tok(a_v), tok(a_v), tok(a_qk), tmin, st,
                  pl.BlockSpec(g.shape, lambda i, c: (0, 0))],
        out_specs=[tok(a_v), st],
        out_shape=[jax.ShapeDtypeStruct((B, T, a_v), BF16), jax.ShapeDtypeStruct(s0.shape, F32)],
        scratch_shapes=[pltpu.VMEM((bb, nh, dk, dv), F32)],
        compiler_params=_params("parallel", "arbitrary"),
        name="gla",
    )(q, kt, v, gate, la, lat, s0, g)


def _bias_table_kernel(rel_ref, out_ref, *, chunk, win, n_rel):
    n_var, nh, rows, width = out_ref.shape
    back = win - chunk
    wide = width + back + rows
    assert wide % 128 == 0 and chunk & (chunk - 1) == 0
    u = lax.broadcasted_iota(jnp.int32, (n_rel, wide), 1)
    want = jnp.clip(back + rows - 1 - u, -MAX_REL, MAX_REL) + MAX_REL
    onehot = (lax.broadcasted_iota(jnp.int32, (n_rel, wide), 0) == want).astype(F32)
    vec = jnp.dot(rel_ref[...], onehot, precision=HIGHEST, preferred_element_type=F32)
    i = lax.broadcasted_iota(jnp.int32, (rows, width), 0)
    j = lax.broadcasted_iota(jnp.int32, (rows, width), 1)
    first = i - (i & (chunk - 1))
    for t in range(n_var):
        shift = back - t * rows if t < n_var - 1 else 0
        ok = (j + shift >= first) & (j + shift < first + win)
        for h in range(nh):
            x = jnp.broadcast_to(vec[h:h + 1, :], (rows, wide))
            x = pltpu.roll(x, (wide - (shift + rows - 1)) % wide, 1, stride=1, stride_axis=0)
            out_ref[t, h] = jnp.where(ok, x[:, :width] * LOG2E, -jnp.inf)


def _bias_table(rel_padded, *, chunk, win, rows):
    nh, n_rel = rel_padded.shape
    n_var = (win - chunk) // rows + 1
    return pl.pallas_call(
        functools.partial(_bias_table_kernel, chunk=chunk, win=win, n_rel=n_rel),
        out_shape=jax.ShapeDtypeStruct((n_var, nh, rows, win + rows - chunk), F32),
        compiler_params=_params(),
        name="band_bias_table",
    )(rel_padded)


def _pair_masks(d):
    lane = lax.broadcasted_iota(jnp.int32, (1, 2 * d), 1)
    return lane, [lane < d, lane >= d]


def _band_prompt_kernel(q_ref, k_ref, v_ref, bias_ref, o_ref, *, nh, d, back):
    n_var, _, tq, width = bias_ref.shape
    nsub = q_ref.shape[1] // tq
    lane, mine = _pair_masks(d)
    ones_lane = [d, 0]
    cols = [slice(p * 2 * d, (p + 1) * 2 * d) for p in range(nh // 2)]
    zero = jnp.zeros((), BF16)
    q, kw, vw, var = [], [], [], []
    for sb in range(nsub):
        c = pl.program_id(1) * nsub + sb
        start = pl.multiple_of(jnp.maximum(c * tq - back, 0), tq)
        var.append(jnp.minimum(c, n_var - 1))
        q.append(q_ref[0, sb * tq:(sb + 1) * tq, :])
        kw.append(k_ref[0, pl.ds(start, width), :])
        vw.append(v_ref[0, pl.ds(start, width), :])
    units = [(sb, p, hh) for sb in range(nsub) for p in range(nh // 2) for hh in range(2)]
    s = [_dot_nt(jnp.where(mine[hh], q[sb][:, cols[p]], zero), kw[sb][:, cols[p]]) + bias_ref[var[sb], 2 * p + hh]
         for sb, p, hh in units]
    m = [jnp.max(x, axis=-1, keepdims=True) for x in s]
    e = [jnp.exp2(x - mx).astype(BF16) for x, mx in zip(s, m)]
    r = [_dot(e[u], jnp.where(mine[hh], vw[sb][:, cols[p]], (lane == ones_lane[hh]).astype(BF16)))
         for u, (sb, p, hh) in enumerate(units)]
    r = [x / x[:, ones_lane[hh]:ones_lane[hh] + 1] for x, (sb, p, hh) in zip(r, units)]
    for sb in range(nsub):
        rs = r[sb * nh:(sb + 1) * nh]
        o_ref[0, sb * tq:(sb + 1) * tq, :] = jnp.concatenate(
            [jnp.where(mine[0], rs[2 * p], rs[2 * p + 1]) for p in range(nh // 2)], axis=-1).astype(BF16)


def _band_prompt(q, k, v, bias, *, nh, d, nsub=4):
    B, T, w = q.shape
    n_var, _, tq, width = bias.shape
    tq *= nsub
    assert T >= width and T % tq == 0
    kv = pl.BlockSpec((1, T, w), lambda b, c: (b, 0, 0))
    return pl.pallas_call(
        functools.partial(_band_prompt_kernel, nh=nh, d=d, back=width - tq // nsub),
        grid=(B, T // tq),
        in_specs=[pl.BlockSpec((1, tq, w), lambda b, c: (b, c, 0)), kv, kv, _resident(bias.shape)],
        out_specs=pl.BlockSpec((1, tq, w), lambda b, c: (b, c, 0)),
        out_shape=jax.ShapeDtypeStruct((B, T, w), BF16),
        compiler_params=_params("parallel", "arbitrary"),
        name="band_prompt",
    )(q, k, v, bias)


def _band_sample_kernel(q_ref, ckt_ref, cvt_ref, kn_ref, vn_ref, bias_ref, o_ref, *, nh, d, s_new):
    lc = ckt_ref.shape[3]
    q = q_ref[0]
    kn = kn_ref[0]
    vn = vn_ref[0]
    hs = range(nh)
    sl = [slice(h * d, (h + 1) * d) for h in hs]
    sc = [_dot(q[:, sl[h]], ckt_ref[0, h].astype(BF16)) + bias_ref[0, h][:s_new, :lc] for h in hs]
    sn = [_dot_nt(q[:, sl[h]], kn[:, sl[h]]) + bias_ref[0, h][:s_new, lc:lc + s_new] for h in hs]
    m = [jnp.maximum(jnp.max(sc[h], axis=-1, keepdims=True), jnp.max(sn[h], axis=-1, keepdims=True)) for h in hs]
    ec = [jnp.exp2(sc[h] - m[h]) for h in hs]
    en = [jnp.exp2(sn[h] - m[h]) for h in hs]
    l = [jnp.sum(ec[h], axis=-1, keepdims=True) + jnp.sum(en[h], axis=-1, keepdims=True) for h in hs]
    o = [(_dot_nt(ec[h].astype(BF16), cvt_ref[0, h].astype(BF16)) + _dot(en[h].astype(BF16), vn[:, sl[h]])) / l[h]
         for h in hs]
    o_ref[0] = jnp.concatenate(o, axis=-1).astype(BF16)


def _band_sample(q, ckt, cvt, kn, vn, bias, *, nh, d):
    B, s_new, w = q.shape
    lc = ckt.shape[3]
    n_var, _, rows, width = bias.shape
    assert s_new <= CHUNK and lc == width - rows
    new = pl.BlockSpec((1, s_new, w), lambda b: (b, 0, 0))
    old = pl.BlockSpec((1, nh, d, lc), lambda b: (b, 0, 0, 0))
    return pl.pallas_call(
        functools.partial(_band_sample_kernel, nh=nh, d=d, s_new=s_new),
        grid=(B,),
        in_specs=[new, old, old, new, new, pl.BlockSpec((1, nh, rows, width), lambda b: (n_var - 1, 0, 0, 0))],
        out_specs=new,
        out_shape=jax.ShapeDtypeStruct((B, s_new, w), BF16),
        compiler_params=_params("parallel"),
        name="band_sample",
    )(q, ckt, cvt, kn, vn, bias)


def _cumsum_lanes_kernel(x_ref, out_ref, *, blk, suffix):
    R, T = x_ref.shape
    src = lax.broadcasted_iota(jnp.int32, (blk, blk), 0)
    dst = lax.broadcasted_iota(jnp.int32, (blk, blk), 1)
    tri = ((src > dst) if suffix else (src <= dst)).astype(F32)
    carry = jnp.zeros((R, 1), F32)
    n = T // blk
    for i in (range(n - 1, -1, -1) if suffix else range(n)):
        xb = x_ref[:, i * blk:(i + 1) * blk]
        out_ref[:, i * blk:(i + 1) * blk] = jnp.dot(xb, tri, precision=HIGHEST, preferred_element_type=F32) + carry
        carry = carry + jnp.sum(xb, axis=1, keepdims=True)


def _cumsum_lanes(x, *, suffix):
    return pl.pallas_call(
        functools.partial(_cumsum_lanes_kernel, blk=256, suffix=suffix),
        out_shape=jax.ShapeDtypeStruct(x.shape, F32),
        compiler_params=_params(),
        name="cumsum_lanes",
    )(x)


def _fox_prompt_kernel(q_ref, kt_ref, vt_ref, ft_ref, o_ref, *, tq, tk, d):
    p = pl.program_id(1)
    T = q_ref.shape[1]
    assert tq == tk and d % 8 == 0
    nq = T // tq
    lane, mine = _pair_masks(d)
    q_fill = [((lane >= d) & (lane < d + 3)).astype(BF16), (lane < 3).astype(BF16)]
    den = [d, 0]
    sub = lax.broadcasted_iota(jnp.int32, (8, tk), 0)
    ones_blk = (sub == 0).astype(F32)
    pad_blk = jnp.zeros((d - 8, tk), F32)

    def stacked(x, blk, hh):
        return jnp.concatenate([x, blk, pad_blk] if hh == 0 else [blk, pad_blk, x], axis=0)

    diag_mask = (lax.broadcasted_iota(jnp.int32, (tk, tq), 0) <= lax.broadcasted_iota(jnp.int32, (tk, tq), 1))
    npair = q_ref.shape[2] // (2 * d)
    heads = [(pr, hh) for pr in range(npair) for hh in range(2)]
    units = [(qi, pr, hh) for qi in range(nq) for pr, hh in heads]
    qs = {(qi, pr, hh): jnp.where(mine[hh], q_ref[0, qi * tq:(qi + 1) * tq, pr * 2 * d:(pr + 1) * 2 * d], q_fill[hh])
          for qi, pr, hh in units}
    m = {u: jnp.full((1, tq), -jnp.inf, F32) for u in units}
    acc = {u: jnp.zeros((2 * d, tq), F32) for u in units}
    for kv in range(nq):
        ks = slice(kv * tk, (kv + 1) * tk)
        k_rows, v_all = {}, {}
        for pr, hh in heads:
            f = ft_ref[0, pl.ds(2 * (p * npair + pr) + hh, 1), ks] * LOG2E
            hi = f.astype(BF16).astype(F32)
            mid = (f - hi).astype(BF16).astype(F32)
            f_blk = jnp.where(sub == 0, -hi, jnp.where(sub == 1, -mid, jnp.where(sub == 2, -(f - hi - mid), 0.0)))
            k_rows[pr, hh] = stacked(kt_ref[0, 2 * pr + hh, :, ks], f_blk, hh).T.astype(BF16)
            v_all[pr, hh] = stacked(vt_ref[0, 2 * pr + hh, :, ks], ones_blk, hh).astype(BF16)
        live = [u for u in units if u[0] >= kv]
        s = {u: _dot_nt(k_rows[u[1:]], qs[u]) for u in live}
        for u in live:
            if u[0] == kv:
                s[u] = jnp.where(diag_mask, s[u], -jnp.inf)
        m_new = {u: jnp.maximum(m[u], jnp.max(s[u], axis=0, keepdims=True)) for u in live}
        e = {u: jnp.exp2(s[u] - m_new[u]).astype(BF16) for u in live}
        for u in live:
            acc[u] = jnp.exp2(m[u] - m_new[u]) * acc[u] + _dot(v_all[u[1:]], e[u])
            m[u] = m_new[u]
        for pr in range(npair):
            out = [(acc[kv, pr, hh] / acc[kv, pr, hh][den[hh]:den[hh] + 1, :]).T for hh in range(2)]
            o_ref[0, kv * tq:(kv + 1) * tq, pr * 2 * d:(pr + 1) * 2 * d] = jnp.where(mine[0], out[0],
                                                                                   out[1]).astype(BF16)


def _fox_prompt(q, kt, vt, ft, *, tq, tk, npair=2):
    B, T, w = q.shape
    _, nh, d, _ = kt.shape
    kv = pl.BlockSpec((1, 2 * npair, d, T), lambda b, p: (b, p, 0, 0))
    qo = pl.BlockSpec((1, T, 2 * npair * d), lambda b, p: (b, 0, p))
    return pl.pallas_call(
        functools.partial(_fox_prompt_kernel, tq=tq, tk=tk, d=d),
        grid=(B, nh // (2 * npair)),
        in_specs=[qo, kv, kv, pl.BlockSpec((1, nh, T), lambda b, p: (b, 0, 0))],
        out_specs=qo,
        out_shape=jax.ShapeDtypeStruct((B, T, w), BF16),
        compiler_params=_params("parallel", "arbitrary"),
        name="fox_prompt",
    )(q, kt, vt, ft)


def _fox_sample_kernel(q_ref, kt_ref, vt_ref, ct_ref, kn_ref, vn_ref, lf_ref, o_ref, fqt_scr, *, hg, d, s_new):
    g = pl.program_id(1)
    nh = lf_ref.shape[2]
    tri = (lax.broadcasted_iota(jnp.int32, (s_new, s_new), 0)
           >= lax.broadcasted_iota(jnp.int32, (s_new, s_new), 1))
    eye = (lax.broadcasted_iota(jnp.int32, (nh, nh), 0) == lax.broadcasted_iota(jnp.int32, (nh, nh), 1)).astype(F32)
    fq = jnp.dot(tri.astype(F32), lf_ref[0], precision=HIGHEST, preferred_element_type=F32)
    fqt_scr[...] = lax.dot_general(eye, fq, NT, precision=HIGHEST, preferred_element_type=F32)
    hs = range(hg)
    sl = [slice(hh * d, (hh + 1) * d) for hh in hs]
    qh = [q_ref[0, 0, :, sl[hh]] for hh in hs]
    sc = [_dot(qh[hh], kt_ref[0, hh].astype(BF16)) + ct_ref[0, pl.ds(g * hg + hh, 1), :] for hh in hs]
    sn = [jnp.where(tri, _dot_nt(qh[hh], kn_ref[0, 0, :, sl[hh]]) - fqt_scr[pl.ds(g * hg + hh, 1), :], -jnp.inf)
          for hh in hs]
    m = [jnp.maximum(jnp.max(sc[hh], axis=-1, keepdims=True), jnp.max(sn[hh], axis=-1, keepdims=True)) for hh in hs]
    ec = [jnp.exp(sc[hh] - m[hh]) for hh in hs]
    en = [jnp.exp(sn[hh] - m[hh]) for hh in hs]
    l = [jnp.sum(ec[hh], axis=-1, keepdims=True) + jnp.sum(en[hh], axis=-1, keepdims=True) for hh in hs]
    o = [(_dot_nt(ec[hh].astype(BF16), vt_ref[0, hh].astype(BF16)) + _dot(en[hh].astype(BF16), vn_ref[0, 0, :, sl[hh]]))
         / l[hh] for hh in hs]
    o_ref[0, 0] = jnp.concatenate(o, axis=-1).astype(BF16)


def _fox_sample(q, kt, vt, ct, kn, vn, lf, *, hg, b0):
    _, ng, s_new, gw = q.shape
    nb, nh, d, lc = kt.shape
    new = pl.BlockSpec((1, 1, s_new, gw), lambda b, g: (b0 + b, g, 0, 0))
    old = pl.BlockSpec((1, hg, d, lc), lambda b, g: (b, g, 0, 0))
    return pl.pallas_call(
        functools.partial(_fox_sample_kernel, hg=hg, d=d, s_new=s_new),
        grid=(nb, ng),
        in_specs=[new, old, old, pl.BlockSpec((1, nh, lc), lambda b, g: (b0 + b, 0, 0)), new, new,
                  pl.BlockSpec((1, s_new, nh), lambda b, g: (b0 + b, 0, 0))],
        out_specs=pl.BlockSpec((1, 1, s_new, gw), lambda b, g: (b, g, 0, 0)),
        out_shape=jax.ShapeDtypeStruct((nb,) + q.shape[1:], BF16),
        scratch_shapes=[pltpu.VMEM((nh, s_new), F32)],
        compiler_params=_params("parallel", "arbitrary"),
        name="fox_sample",
    )(q, kt, vt, ct, kn, vn, lf)


def _trunk(x, state, wts, dims, *, tm, gla_bb):
    B, T, D = x.shape
    n = B * T
    depth = wts["norm_g"].shape[0]
    h_a, dk_a, dv_a, h_b, d_b, h_c, d_c = dims
    a_qk, a_v, b_w, c_w = h_a * dk_a, h_a * dv_a, h_b * d_b, h_c * d_c
    xs = x.reshape(n, D)
    new_states = []
    pending = ()
    r3 = lambda t: t.reshape(B, T, t.shape[-1])
    ffn_w = wts["ffn_bf16"]

    def ffn(xs, g, key, proj, final_g=None):
        if state is not None:
            return _ffn_streamed(xs, g, *ffn_w[key], proj, final_g, n_chunks=2)
        steps = n // tm
        nxt = (key[0], 1) if key[1] == 0 else (key[0] + 1, 0)
        jobs = []
        if nxt[0] < depth:
            jobs += [_slab_cast(wts["w_gu"], nxt, D, steps), _slab_cast(wts["w_down"], nxt, ffn_w[key][1].shape[0], steps, 2)]
        for src in wts["hosted_src"]:
            share = src.shape[0] // (2 * depth)
            jobs.append(_slab_cast(src, (), share, steps, first=(2 * key[0] + key[1]) * steps))
        xs, *outs = _ffn(xs, g, *ffn_w[key], proj, final_g, jobs, tm=tm)
        if nxt[0] < depth:
            ffn_w[nxt], outs = outs[:2], outs[2:]
        wts["hosted_out"].append(outs)
        return xs

    for l in range(depth):
        i = l // 2
        g = wts["norm_g"][l]
        xs = ffn(xs, g[0:1], (l, 0), pending)
        if l % 2 == 0:
            prompt = state is None
            keep = min(BAND_CHUNKS * CHUNK, T)
            qa, kat, va, gate, la, lat, bq, bk, bv, *kv32 = _even_in(
                xs, g[1:2], {k: v[i] for k, v in wts["even"].items()}, tm=tm, t_split=(B, T) if prompt else (1, n),
                a_qk=a_qk, a_v=a_v, b_w=b_w, dk_a=dk_a, d_b=d_b, kv32=not prompt)
            if not prompt:
                kat, lat = (t.reshape(a_qk, B, T).transpose(1, 0, 2) for t in (kat, lat))
            nk, nv = _band_tail(xs, g[1:2], wts["even_kvt"][i], B=B, T=T, keep=keep) if prompt else kv32
            s0 = jnp.zeros((B, h_a, dk_a, dv_a), F32) if prompt else state[l][0]
            blk = min(CHUNK, T)
            o_a, s_fin = _gla(r3(qa), kat, r3(va), r3(gate), r3(la), lat, s0, wts["gla_g"][i],
                              bb=gla_bb, L=blk, nsub=2 if T % (2 * blk) == 0 else 1)
            if prompt:
                o_b = _band_prompt(r3(bq), r3(bk), r3(bv), wts["band_bias"][i], nh=h_b, d=d_b)
                nk, nv = (t.reshape(B, h_b, d_b, keep).transpose(0, 3, 1, 2) for t in (nk, nv))
            else:
                ck, cv = state[l][1], state[l][2]
                o_b = _band_sample(r3(bq), ck.transpose(0, 2, 3, 1), cv.transpose(0, 2, 3, 1), r3(bk), r3(bv),
                                   wts["band_bias"][i], nh=h_b, d=d_b)
                nk, nv = (t.reshape(B, T, h_b, d_b) for t in (nk, nv))
            new_states.append((s_fin, nk, nv))
            pending = ((o_a.reshape(n, a_v), wts["even_out_a"][i]), (o_b.reshape(n, b_w), wts["even_out_b"][i]))
        elif state is None:
            q, kt, vt, lft = _odd_in_t(xs, g[1:2], wts["odd_q"][i], wts["odd_kvt"][i], wts["odd_ft"][i],
                                       wts["b_f_col"][i], B=B, T=T, tm=tm, nh=h_c, d_c=d_c)
            ft = _cumsum_lanes(lft.reshape(B * h_c, T), suffix=False).reshape(B, h_c, T)
            o = _fox_prompt(r3(q), kt, vt, ft, tq=512, tk=512)
            new_states.append((kt.transpose(0, 3, 1, 2), vt.transpose(0, 3, 1, 2), lft.transpose(0, 2, 1)))
            pending = ((o.reshape(n, c_w), wts["odd_out"][i]),)
        else:
            q, k, v, k32, v32, lf = _odd_in(xs, g[1:2], wts["odd_main"][i], wts["odd_f"][i], wts["b_f"][i],
                                             tm=tm, c_w=c_w, d_c=d_c, h_c=h_c)
            clf = state[l][2]
            lc = clf.shape[1]
            hg = 8
            ct = _cumsum_lanes(clf.astype(F32).transpose(0, 2, 1).reshape(B * h_c, lc), suffix=True)
            grp = lambda t: t.reshape(B, T, h_c // hg, hg * d_c).transpose(0, 2, 1, 3)
            hosted = wts["hosted_out"]
            per = B // len(hosted)
            o = jnp.concatenate([
                _fox_sample(grp(q), parts[2 * i].reshape(per, h_c, d_c, lc), parts[2 * i + 1].reshape(per, h_c, d_c, lc),
                            ct.reshape(B, h_c, lc), grp(k), grp(v), r3(lf), hg=hg, b0=j * per)
                for j, parts in enumerate(hosted)], axis=0)
            o = o.transpose(0, 2, 1, 3).reshape(n, c_w)
            new_states.append((r3(k32).reshape(B, T, h_c, d_c), r3(v32).reshape(B, T, h_c, d_c), r3(lf)))
            pending = ((o, wts["odd_out"][i]),)
        last = l == depth - 1
        xs = ffn(xs, g[2:3], (l, 1), pending, wts["final_g"] if last else None)
        pending = ()
    return xs.reshape(B, T, D), new_states


def kernel(x_prompt, x_sample, state_gla, cache_band_k, cache_band_v, cache_fox_k, cache_fox_v, cache_fox_logf,
           norm_g, ffn_w_gu, ffn_w_down, even_w_in, gla_w_alpha_up, gla_b_alpha, gla_norm_g, band_rel_bias,
           even_w_out, odd_w_in, fox_b_f, odd_w_out, final_norm_g):
    depth = norm_g.shape[0]
    _, _, h_a, dk_a, dv_a = state_gla.shape
    h_b, d_b = cache_band_k.shape[-2:]
    h_c, d_c = cache_fox_k.shape[-2:]
    r_a = gla_w_alpha_up.shape[1]
    a_qk, a_v, b_w, c_w = h_a * dk_a, h_a * dv_a, h_b * d_b, h_c * d_c
    assert cache_band_k.shape[2] == BAND_CHUNKS * CHUNK and band_rel_bias.shape[-1] == 2 * MAX_REL + 1

    r0 = 2 * a_qk + 2 * a_v
    n_rel = 384
    rel_padded = jnp.pad(band_rel_bias, ((0, 0), (0, 0), (0, n_rel - band_rel_bias.shape[-1])))
    win = (BAND_CHUNKS + 1) * CHUNK
    odd_t = odd_w_in.transpose(0, 2, 1)
    even_t = even_w_in.transpose(0, 2, 1)
    wts = {
        "norm_g": norm_g,
        "w_gu": ffn_w_gu,
        "w_down": ffn_w_down,
        "ffn_bf16": {(0, 0): [ffn_w_gu[0, 0].astype(BF16), ffn_w_down[0, 0].astype(BF16)]},
        "even": {
            "main": jnp.concatenate([even_w_in[:, :, :r0], even_w_in[:, :, r0 + r_a:]], axis=-1).astype(BF16),
            "r": even_w_in[:, :, r0:r0 + r_a].astype(BF16),
            "up": gla_w_alpha_up.astype(BF16),
            "b_alpha": gla_b_alpha[:, None, :],
        },
        "even_kvt": even_t[:, r0 + r_a + b_w:].astype(BF16),
        "gla_g": gla_norm_g[:, None, :],
        "band_bias": [_bias_table(rel_padded[i], chunk=CHUNK, win=win, rows=2 * CHUNK)
                      for i in range(rel_padded.shape[0])],
        "even_out_a": even_w_out[:, :a_v].astype(BF16),
        "even_out_b": even_w_out[:, a_v:].astype(BF16),
        "odd_main": odd_w_in[:, :, :3 * c_w].astype(BF16),
        "odd_f": odd_w_in[:, :, 3 * c_w:].astype(BF16),
        "odd_q": odd_w_in[:, :, :c_w].astype(BF16),
        "odd_kvt": odd_t[:, c_w:3 * c_w].astype(BF16),
        "odd_ft": odd_t[:, 3 * c_w:].astype(BF16),
        "b_f": fox_b_f[:, None, :],
        "b_f_col": fox_b_f[:, :, None],
        "odd_out": odd_w_out.astype(BF16),
        "final_g": final_norm_g[None, :],
    }
    dims = (h_a, dk_a, dv_a, h_b, d_b, h_c, d_c)

    sample_states = []
    wts["hosted_src"], wts["hosted_out"] = [], []
    for l in range(depth):
        i = l // 2
        if l % 2 == 0:
            sample_states.append((state_gla[i], cache_band_k[i], cache_band_v[i]))
        else:
            sample_states.append((cache_fox_k[i], cache_fox_v[i], cache_fox_logf[i]))
            wts["hosted_src"] += [c[i].transpose(0, 2, 3, 1).reshape(-1, c.shape[2]) for c in (cache_fox_k, cache_fox_v)]
    assert len(wts["hosted_src"]) == 2, "the sample's forgetting attention expects one cached odd layer"

    y_prompt, ns_p = _trunk(x_prompt, None, wts, dims, tm=512, gla_bb=8)
    y_sample, ns_s = _trunk(x_sample, sample_states, wts, dims, tm=256, gla_bb=8)

    ev = range(0, depth, 2)
    od = range(1, depth, 2)
    stack = lambda ns, layers, j: jnp.stack([ns[l][j] for l in layers])
    return (y_prompt, y_sample,
            stack(ns_p, ev, 0), stack(ns_p, ev, 1), stack(ns_p, ev, 2),
            stack(ns_p, od, 0), stack(ns_p, od, 1), stack(ns_p, od, 2),
            stack(ns_s, ev, 0), stack(ns_s, ev, 1), stack(ns_s, ev, 2),
            stack(ns_s, od, 0), stack(ns_s, od, 1), stack(ns_s, od, 2))
```

```python
import functools

import jax
import jax.numpy as jnp
from jax import lax
from jax.experimental import pallas as pl
from jax.experimental.pallas import tpu as pltpu

F32 = jnp.float32
BF16 = jnp.bfloat16
HIGHEST = lax.Precision.HIGHEST
NT = (((1,), (1,)), ((), ()))

EPS = 1e-6
CHUNK = 64
BAND_CHUNKS = 8
MAX_REL = 128
GATE_NORM_A = 16.0
LOG2E = 1.4426950408889634

V7X_VMEM_BYTES = 64 * 1024 * 1024
VMEM_LIMIT_BYTES = V7X_VMEM_BYTES * 7 // 8
LANES = 128

PROMPT_ROWS = 512
GLA_BATCH = 8
GLA_BLOCKS = 2
BAND_BLOCK_CHUNKS = 2
BAND_BLOCKS = 4
FOX_BLOCK = 256
FOX_PAIRS = 2
FOX_SAMPLE_HEADS = 8
CUMSUM_BLOCK = 256


def _params(*sem):
    return pltpu.CompilerParams(dimension_semantics=sem, vmem_limit_bytes=VMEM_LIMIT_BYTES)


def _resident(shape):
    nd = len(shape)
    return pl.BlockSpec(shape, lambda *_: (0,) * nd, pipeline_mode=pl.Buffered(1))


def _rows(tm, width):
    return pl.BlockSpec((tm, width), lambda i: (i, 0))


def _rms(x, g):
    ms = jnp.mean(x * x, axis=-1, keepdims=True)
    return x * lax.rsqrt(ms + EPS) * g


def _log_sigmoid(x):
    return jnp.minimum(x, 0.0) - jnp.log1p(jnp.exp(-jnp.abs(x)))


def _dot(a, b):
    return jnp.dot(a, b, preferred_element_type=F32)


def _dot_nt(a, b):
    return lax.dot_general(a, b, NT, preferred_element_type=F32)


def _ffn_kernel(*refs, n_proj, has_final, casts, d_ff):
    it = iter(refs)
    x_ref = next(it)
    proj = [(next(it), next(it)) for _ in range(n_proj)]
    g_ref, wgu_ref, wd_ref = next(it), next(it), next(it)
    gf_ref = next(it) if has_final else None
    cast_src = [next(it) for _ in range(casts)]
    out_ref = next(it)
    for src in cast_src:
        next(it)[...] = src[...].astype(BF16)

    x = x_ref[...]
    for o_ref, w_ref in proj:
        x = x + _dot(o_ref[...], w_ref[...])
    xn = _rms(x, g_ref[...]).astype(BF16)
    h = _dot(xn, wgu_ref[...])
    a = (jax.nn.silu(h[:, :d_ff]) * h[:, d_ff:]).astype(BF16)
    y = x + 0.5 * _dot(a, wd_ref[...])
    if has_final:
        y = _rms(y, gf_ref[...])
    out_ref[...] = y


def _slab_cast(src, lead, rows, steps, per=1, first=0):
    slab = rows * per // steps
    width = src.shape[-1]
    assert rows * per % steps == 0 and slab % 16 == 0
    return (src,
            pl.BlockSpec((None,) * len(lead) + (slab, width), lambda i: tuple(lead) + (first + i // per, 0)),
            pl.BlockSpec((slab, width), lambda i: (i // per, 0)),
            jax.ShapeDtypeStruct((rows, width), BF16))


def _ffn(x, g, wgu, wd, proj=(), final_g=None, casts=(), *, tm):
    n, d = x.shape
    d_ff = wd.shape[0]
    steps = n // tm
    args, specs = [x], [_rows(tm, d)]
    for o, w in proj:
        args += [o, w]
        specs += [_rows(tm, o.shape[1]), _resident(w.shape)]
    args += [g, wgu, wd]
    specs += [_resident(g.shape), _resident(wgu.shape), _resident(wd.shape)]
    if final_g is not None:
        args.append(final_g)
        specs.append(_resident(final_g.shape))
    out_specs, out_shapes = [_rows(tm, d)], [jax.ShapeDtypeStruct((n, d), F32)]
    for src, in_spec, out_spec, out_shape in casts:
        args.append(src)
        specs.append(in_spec)
        out_specs.append(out_spec)
        out_shapes.append(out_shape)
    return pl.pallas_call(
        functools.partial(_ffn_kernel, n_proj=len(proj), has_final=final_g is not None, casts=len(casts), d_ff=d_ff),
        grid=(steps,),
        in_specs=specs,
        out_specs=out_specs,
        out_shape=out_shapes,
        compiler_params=_params("arbitrary" if casts else "parallel"),
        name="ffn",
    )(*args)


def _even_in_kernel(x_ref, g_ref, w_ref, wr_ref, wup_ref, ba_ref,
                    qa_ref, kat_ref, va_ref, gate_ref, la_ref, lat_ref, bq_ref, bk_ref, bv_ref, *kv32_refs,
                    a_qk, a_v, b_w, dk_a, d_b):
    xn = _rms(x_ref[...], g_ref[...]).astype(BF16)
    h = _dot(xn, w_ref[...])
    o = 0
    qa_ref[...] = h[:, o:o + a_qk] * (dk_a ** -0.5)
    o += a_qk
    kat_ref[...] = h[:, o:o + a_qk].T
    o += a_qk
    va_ref[...] = h[:, o:o + a_v].astype(BF16)
    o += a_v
    gate_ref[...] = jax.nn.silu(h[:, o:o + a_v])
    o += a_v
    bq_ref[...] = (h[:, o:o + b_w] * (LOG2E * d_b ** -0.5)).astype(BF16)
    o += b_w
    bk = h[:, o:o + b_w]
    o += b_w
    bv = h[:, o:o + b_w]
    bk_ref[...] = bk.astype(BF16)
    bv_ref[...] = bv.astype(BF16)
    if kv32_refs:
        kv32_refs[0][...] = bk
        kv32_refs[1][...] = bv
    r = _dot(xn, wr_ref[...]).astype(BF16)
    la = _log_sigmoid(_dot(r, wup_ref[...]) + ba_ref[...]) / GATE_NORM_A
    la_ref[...] = la
    lat_ref[...] = la.T


def _even_in(x, g, w, *, tm, t_split, a_qk, a_v, b_w, dk_a, d_b, kv32):
    n, d = x.shape
    bo, to = t_split
    nb = to // tm
    assert bo * to == n and to % tm == 0
    tmin = (pl.BlockSpec((None, a_qk, tm), lambda i: (i // nb, 0, i % nb)), jax.ShapeDtypeStruct((bo, a_qk, to), F32))
    tok = lambda width, dt: (_rows(tm, width), jax.ShapeDtypeStruct((n, width), dt))
    outs = [tok(a_qk, F32), tmin, tok(a_v, BF16), tok(a_v, F32), tok(a_qk, F32), tmin,
            tok(b_w, BF16), tok(b_w, BF16), tok(b_w, BF16)]
    if kv32:
        outs += [tok(b_w, F32), tok(b_w, F32)]
    args = [x, g] + [w[k] for k in ("main", "r", "up", "b_alpha")]
    return pl.pallas_call(
        functools.partial(_even_in_kernel, a_qk=a_qk, a_v=a_v, b_w=b_w, dk_a=dk_a, d_b=d_b),
        grid=(n // tm,),
        in_specs=[_rows(tm, d)] + [_resident(a.shape) for a in args[1:]],
        out_specs=[s for s, _ in outs],
        out_shape=[t for _, t in outs],
        compiler_params=_params("parallel"),
        name="even_in",
    )(*args)


def _band_tail_kernel(x_ref, g_ref, wkvt_ref, kt_ref, vt_ref):
    xn = _rms(x_ref[...], g_ref[...]).astype(BF16)
    kvt = _dot_nt(wkvt_ref[...], xn)
    b_w = kt_ref.shape[1]
    kt_ref[0] = kvt[:b_w]
    vt_ref[0] = kvt[b_w:]


def _band_tail(x, g, w_kvt, *, B, T, keep):
    n, d = x.shape
    b_w = w_kvt.shape[0] // 2
    assert T % keep == 0
    nb = T // keep
    out = pl.BlockSpec((1, b_w, keep), lambda b: (b, 0, 0))
    return pl.pallas_call(
        _band_tail_kernel,
        grid=(B,),
        in_specs=[pl.BlockSpec((keep, d), lambda b: (b * nb + nb - 1, 0)), _resident(g.shape), _resident(w_kvt.shape)],
        out_specs=[out, out],
        out_shape=[jax.ShapeDtypeStruct((B, b_w, keep), F32)] * 2,
        compiler_params=_params("parallel"),
        name="band_tail",
    )(x, g, w_kvt)


def _odd_in_kernel(x_ref, g_ref, w_ref, wf_ref, bf_ref, q_ref, k_ref, v_ref, k32_ref, v32_ref, lf_ref, *, c_w, d_c):
    xn = _rms(x_ref[...], g_ref[...]).astype(BF16)
    h = _dot(xn, w_ref[...])
    q_ref[...] = (h[:, :c_w] * (d_c ** -0.5)).astype(BF16)
    k = h[:, c_w:2 * c_w]
    v = h[:, 2 * c_w:]
    k32_ref[...] = k
    v32_ref[...] = v
    k_ref[...] = k.astype(BF16)
    v_ref[...] = v.astype(BF16)
    lf_ref[...] = _log_sigmoid(_dot(xn, wf_ref[...]) + bf_ref[...])


def _odd_in(x, g, w_main, w_f, b_f, *, tm, c_w, d_c, h_c):
    n, d = x.shape
    outs = [(c_w, BF16), (c_w, BF16), (c_w, BF16), (c_w, F32), (c_w, F32), (h_c, F32)]
    return pl.pallas_call(
        functools.partial(_odd_in_kernel, c_w=c_w, d_c=d_c),
        grid=(n // tm,),
        in_specs=[_rows(tm, d), _resident(g.shape), _resident(w_main.shape), _resident(w_f.shape),
                  _resident(b_f.shape)],
        out_specs=[_rows(tm, w) for w, _ in outs],
        out_shape=[jax.ShapeDtypeStruct((n, w), dt) for w, dt in outs],
        compiler_params=_params("parallel"),
        name="odd_in",
    )(x, g, w_main, w_f, b_f)


def _odd_in_t_kernel(x_ref, g_ref, wq_ref, wkvt_ref, wft_ref, bf_ref, q_ref, kt_ref, vt_ref, lft_ref, *, nh, d_c):
    xn = _rms(x_ref[...], g_ref[...]).astype(BF16)
    tm = xn.shape[0]
    c_w = nh * d_c
    q_ref[...] = (_dot(xn, wq_ref[...]) * (LOG2E * d_c ** -0.5)).astype(BF16)
    ht = _dot_nt(wkvt_ref[...], xn)
    kt_ref[0] = ht[:c_w].reshape(nh, d_c, tm)
    vt_ref[0] = ht[c_w:].reshape(nh, d_c, tm)
    lft_ref[0] = _log_sigmoid(_dot_nt(wft_ref[...], xn) + bf_ref[...])


def _odd_in_t(x, g, wq, wkvt, wft, b_f_col, *, B, T, tm, nh, d_c):
    n, d = x.shape
    nb = T // tm
    c_w = nh * d_c
    tok = lambda w: pl.BlockSpec((tm, w), lambda b, i: (b * nb + i, 0))
    kv = pl.BlockSpec((1, nh, d_c, tm), lambda b, i: (b, 0, 0, i))
    return pl.pallas_call(
        functools.partial(_odd_in_t_kernel, nh=nh, d_c=d_c),
        grid=(B, nb),
        in_specs=[tok(d), _resident(g.shape), _resident(wq.shape), _resident(wkvt.shape), _resident(wft.shape),
                  _resident(b_f_col.shape)],
        out_specs=[tok(c_w), kv, kv, pl.BlockSpec((1, nh, tm), lambda b, i: (b, 0, i))],
        out_shape=[jax.ShapeDtypeStruct((n, c_w), BF16), jax.ShapeDtypeStruct((B, nh, d_c, T), F32),
                   jax.ShapeDtypeStruct((B, nh, d_c, T), F32), jax.ShapeDtypeStruct((B, nh, T), F32)],
        compiler_params=_params("parallel", "parallel"),
        name="odd_in_t",
    )(x, g, wq, wkvt, wft, b_f_col)


def _gla_kernel(q_ref, kt_ref, v_ref, gate_ref, la_ref, lat_ref, s0_ref, g_ref, o_ref, sfin_ref, s_scr,
                *, bb, L, nh, dk, dv):
    c = pl.program_id(1)

    @pl.when(c == 0)
    def _():
        s_scr[...] = s0_ref[...]

    tw = q_ref.shape[1]
    nsub = tw // L
    w = nh * dk
    shift = L.bit_length() - 1
    assert L == 1 << shift and nh % 2 == 0
    tri = (lax.broadcasted_iota(jnp.int32, (L, L), 0) >= lax.broadcasted_iota(jnp.int32, (L, L), 1)).astype(F32)
    t0 = lax.broadcasted_iota(jnp.int32, (tw, tw), 0)
    t1 = lax.broadcasted_iota(jnp.int32, (tw, tw), 1)
    tri_t = (((t0 >> shift) == (t1 >> shift)) & (t0 <= t1)).astype(F32)
    lane_t = lax.broadcasted_iota(jnp.int32, (1, tw), 1)
    row_i = lax.broadcasted_iota(jnp.int32, (L, tw), 0)
    _, mine = _pair_masks(dk)
    g = g_ref[...]
    bs = range(bb)
    zero = jnp.zeros((), BF16)

    bt_all = jnp.dot(jnp.concatenate([lat_ref[b] for b in bs], axis=0), tri_t, precision=HIGHEST,
                     preferred_element_type=F32)
    k_in, k_st, a_col = [], [], []
    for b in bs:
        bt = bt_all[b * w:(b + 1) * w]
        kt = kt_ref[b]
        last = [bt[:, (cc + 1) * L - 1:(cc + 1) * L] for cc in range(nsub)]
        last_sel = last[0]
        for cc in range(1, nsub):
            last_sel = jnp.where(lane_t >= cc * L, last[cc], last_sel)
        k_in.append((kt * jnp.exp(-bt)).astype(BF16))
        k_st.append((kt * jnp.exp(last_sel - bt)).astype(BF16))
        a_col.append([jnp.exp(x) for x in last])
    q_in = {}
    for cc in range(nsub):
        rows = slice(cc * L, (cc + 1) * L)
        bc = jnp.dot(tri, jnp.concatenate([la_ref[b, rows] for b in bs], axis=1), precision=HIGHEST,
                     preferred_element_type=F32)
        for b in bs:
            q_in[b, cc] = (q_ref[b, rows] * jnp.exp(bc[:, b * w:(b + 1) * w])).astype(BF16)

    ccs = range(nsub)
    pairs = [(b, p) for b in bs for p in range(nh // 2)]
    heads = [(b, p, hh) for b, p in pairs for hh in range(2)]
    pair = lambda p: slice(p * 2 * dk, (p + 1) * 2 * dk)
    in_blk = [(lane_t >= cc * L) & (lane_t < (cc + 1) * L) for cc in ccs]
    causal = [in_blk[cc] & (lane_t - cc * L <= row_i) for cc in ccs]
    qm = {(b, p, hh, cc): jnp.where(mine[hh], q_in[b, cc][:, pair(p)], zero) for b, p, hh in heads for cc in ccs}
    sc = {(b, p): _dot(jnp.concatenate([qm[b, p, hh, cc] for cc in ccs for hh in range(2)], axis=0), k_in[b][pair(p)])
          for b, p in pairs}
    att = {(b, p, hh, cc): jnp.where(causal[cc], sc[b, p][(2 * cc + hh) * L:(2 * cc + hh + 1) * L], 0.0).astype(BF16)
           for b, p, hh in heads for cc in ccs}
    inc, o_in = {}, {}
    for b, p, hh in heads:
        h = 2 * p + hh
        ks = [jnp.where(in_blk[cc], k_st[b][h * dk:(h + 1) * dk], zero) for cc in ccs]
        r = _dot(jnp.concatenate([att[b, p, hh, cc] for cc in ccs] + ks, axis=0), v_ref[b, :, h * dv:(h + 1) * dv])
        for cc in ccs:
            o_in[b, h, cc] = r[cc * L:(cc + 1) * L]
            inc[b, h, cc] = r[nsub * L + cc * dk:nsub * L + (cc + 1) * dk]
    state = {(b, h): s_scr[b, h] for b in bs for h in range(nh)}
    for cc in ccs:
        rows = slice(cc * L, (cc + 1) * L)
        o = {}
        for b, p in pairs:
            s_pair = jnp.concatenate([state[b, 2 * p], state[b, 2 * p + 1]], axis=0).astype(BF16)
            carried = _dot(jnp.concatenate([qm[b, p, 0, cc], qm[b, p, 1, cc]], axis=0), s_pair)
            for hh in range(2):
                h = 2 * p + hh
                o[b, h] = o_in[b, h, cc] + carried[hh * L:(hh + 1) * L]
                state[b, h] = a_col[b][cc][h * dk:(h + 1) * dk] * state[b, h] + inc[b, h, cc]
        o = {k: x * lax.rsqrt(jnp.mean(x * x, axis=-1, keepdims=True) + EPS) * g for k, x in o.items()}
        for b in bs:
            o_ref[b, rows] = (jnp.concatenate([o[b, h] for h in range(nh)], axis=-1) * gate_ref[b, rows]).astype(BF16)
    for (b, h), x in state.items():
        s_scr[b, h] = x

    @pl.when(c == pl.num_programs(1) - 1)
    def _():
        sfin_ref[...] = s_scr[...]


def _gla(q, kt, v, gate, la, lat, s0, g, *, bb, L, nsub):
    B, T, a_qk = q.shape
    a_v = v.shape[-1]
    _, nh, dk, dv = s0.shape
    tw = L * nsub
    assert T % tw == 0

    def tok(w):
        return pl.BlockSpec((bb, tw, w), lambda i, c: (i, c, 0))

    tmin = pl.BlockSpec((bb, a_qk, tw), lambda i, c: (i, 0, c))
    st = pl.BlockSpec((bb, nh, dk, dv), lambda i, c: (i, 0, 0, 0))
    return pl.pallas_call(
        functools.partial(_gla_kernel, bb=bb, L=L, nh=nh, dk=dk, dv=dv),
        grid=(B // bb, T // tw),
        in_specs=[tok(a_qk), tmin, tok(a_v), tok(a_v), tok(a_qk), tmin, st,
                  pl.BlockSpec(g.shape, lambda i, c: (0, 0))],
        out_specs=[tok(a_v), st],
        out_shape=[jax.ShapeDtypeStruct((B, T, a_v), BF16), jax.ShapeDtypeStruct(s0.shape, F32)],
        scratch_shapes=[pltpu.VMEM((bb, nh, dk, dv), F32)],
        compiler_params=_params("parallel", "arbitrary"),
        name="gla",
    )(q, kt, v, gate, la, lat, s0, g)


def _bias_table_kernel(rel_ref, out_ref, *, chunk, win, n_rel):
    n_var, nh, rows, width = out_ref.shape
    back = win - chunk
    wide = width + back + rows
    assert wide % LANES == 0 and chunk & (chunk - 1) == 0
    u = lax.broadcasted_iota(jnp.int32, (n_rel, wide), 1)
    want = jnp.clip(back + rows - 1 - u, -MAX_REL, MAX_REL) + MAX_REL
    onehot = (lax.broadcasted_iota(jnp.int32, (n_rel, wide), 0) == want).astype(F32)
    vec = jnp.dot(rel_ref[...], onehot, precision=HIGHEST, preferred_element_type=F32)
    i = lax.broadcasted_iota(jnp.int32, (rows, width), 0)
    j = lax.broadcasted_iota(jnp.int32, (rows, width), 1)
    first = i - (i & (chunk - 1))
    for t in range(n_var):
        shift = back - t * rows if t < n_var - 1 else 0
        ok = (j + shift >= first) & (j + shift < first + win)
        for h in range(nh):
            x = jnp.broadcast_to(vec[h:h + 1, :], (rows, wide))
            x = pltpu.roll(x, (wide - (shift + rows - 1)) % wide, 1, stride=1, stride_axis=0)
            out_ref[t, h] = jnp.where(ok, x[:, :width] * LOG2E, -jnp.inf)


def _bias_table(rel_padded, *, chunk, win, rows):
    nh, n_rel = rel_padded.shape
    n_var = (win - chunk) // rows + 1
    return pl.pallas_call(
        functools.partial(_bias_table_kernel, chunk=chunk, win=win, n_rel=n_rel),
        out_shape=jax.ShapeDtypeStruct((n_var, nh, rows, win + rows - chunk), F32),
        compiler_params=_params(),
        name="band_bias_table",
    )(rel_padded)


def _pair_masks(d):
    lane = lax.broadcasted_iota(jnp.int32, (1, 2 * d), 1)
    return lane, [lane < d, lane >= d]


def _band_prompt_kernel(q_ref, k_ref, v_ref, bias_ref, o_ref, vx_scr, *, nh, d, back):
    n_var, _, tq, width = bias_ref.shape
    nsub = q_ref.shape[1] // tq
    lane, mine = _pair_masks(d)
    ones_lane = [d, 0]
    cols = [slice(p * 2 * d, (p + 1) * 2 * d) for p in range(nh // 2)]
    zero = jnp.zeros((), BF16)

    @pl.when(pl.program_id(1) == 0)
    def _():
        in_pair = lax.broadcasted_iota(jnp.int32, (1, nh * d), 1) & (2 * d - 1)
        v = v_ref[0]
        vx_scr[0] = jnp.where(in_pair < d, v, (in_pair == ones_lane[0]).astype(BF16))
        vx_scr[1] = jnp.where(in_pair >= d, v, (in_pair == ones_lane[1]).astype(BF16))

    q, kw, vw, var = [], [], [], []
    for sb in range(nsub):
        c = pl.program_id(1) * nsub + sb
        start = pl.multiple_of(jnp.maximum(c * tq - back, 0), tq)
        var.append(jnp.minimum(c, n_var - 1))
        q.append(q_ref[0, sb * tq:(sb + 1) * tq, :])
        kw.append(k_ref[0, pl.ds(start, width), :])
        vw.append([vx_scr[hh, pl.ds(start, width), :] for hh in range(2)])
    units = [(sb, p, hh) for sb in range(nsub) for p in range(nh // 2) for hh in range(2)]
    s = [_dot_nt(jnp.where(mine[hh], q[sb][:, cols[p]], zero), kw[sb][:, cols[p]]) + bias_ref[var[sb], 2 * p + hh]
         for sb, p, hh in units]
    m = [jnp.max(x, axis=-1, keepdims=True) for x in s]
    e = [jnp.exp2(x - mx).astype(BF16) for x, mx in zip(s, m)]
    r = [_dot(e[u], vw[sb][hh][:, cols[p]]) for u, (sb, p, hh) in enumerate(units)]
    r = [x / x[:, ones_lane[hh]:ones_lane[hh] + 1] for x, (sb, p, hh) in zip(r, units)]
    for sb in range(nsub):
        rs = r[sb * nh:(sb + 1) * nh]
        o_ref[0, sb * tq:(sb + 1) * tq, :] = jnp.concatenate(
            [jnp.where(mine[0], rs[2 * p], rs[2 * p + 1]) for p in range(nh // 2)], axis=-1).astype(BF16)


def _band_prompt(q, k, v, bias, *, nh, d, nsub):
    B, T, w = q.shape
    n_var, _, tq, width = bias.shape
    tq *= nsub
    assert T >= width and T % tq == 0
    kv = pl.BlockSpec((1, T, w), lambda b, c: (b, 0, 0))
    return pl.pallas_call(
        functools.partial(_band_prompt_kernel, nh=nh, d=d, back=width - tq // nsub),
        grid=(B, T // tq),
        in_specs=[pl.BlockSpec((1, tq, w), lambda b, c: (b, c, 0)), kv, kv, _resident(bias.shape)],
        out_specs=pl.BlockSpec((1, tq, w), lambda b, c: (b, c, 0)),
        out_shape=jax.ShapeDtypeStruct((B, T, w), BF16),
        scratch_shapes=[pltpu.VMEM((2, T, w), BF16)],
        compiler_params=_params("parallel", "arbitrary"),
        name="band_prompt",
    )(q, k, v, bias)


def _band_sample_kernel(q_ref, ckt_ref, cvt_ref, kn_ref, vn_ref, bias_ref, o_ref, *, nh, d, s_new):
    lc = ckt_ref.shape[3]
    q = q_ref[0]
    kn = kn_ref[0]
    vn = vn_ref[0]
    hs = range(nh)
    sl = [slice(h * d, (h + 1) * d) for h in hs]
    sc = [_dot(q[:, sl[h]], ckt_ref[0, h].astype(BF16)) + bias_ref[0, h][:s_new, :lc] for h in hs]
    sn = [_dot_nt(q[:, sl[h]], kn[:, sl[h]]) + bias_ref[0, h][:s_new, lc:lc + s_new] for h in hs]
    m = [jnp.maximum(jnp.max(sc[h], axis=-1, keepdims=True), jnp.max(sn[h], axis=-1, keepdims=True)) for h in hs]
    ec = [jnp.exp2(sc[h] - m[h]) for h in hs]
    en = [jnp.exp2(sn[h] - m[h]) for h in hs]
    l = [jnp.sum(ec[h], axis=-1, keepdims=True) + jnp.sum(en[h], axis=-1, keepdims=True) for h in hs]
    o = [(_dot_nt(ec[h].astype(BF16), cvt_ref[0, h].astype(BF16)) + _dot(en[h].astype(BF16), vn[:, sl[h]])) / l[h]
         for h in hs]
    o_ref[0] = jnp.concatenate(o, axis=-1).astype(BF16)


def _band_sample(q, ckt, cvt, kn, vn, bias, *, nh, d):
    B, s_new, w = q.shape
    lc = ckt.shape[3]
    n_var, _, rows, width = bias.shape
    assert s_new <= CHUNK and lc == width - rows
    new = pl.BlockSpec((1, s_new, w), lambda b: (b, 0, 0))
    old = pl.BlockSpec((1, nh, d, lc), lambda b: (b, 0, 0, 0))
    return pl.pallas_call(
        functools.partial(_band_sample_kernel, nh=nh, d=d, s_new=s_new),
        grid=(B,),
        in_specs=[new, old, old, new, new, pl.BlockSpec((1, nh, rows, width), lambda b: (n_var - 1, 0, 0, 0))],
        out_specs=new,
        out_shape=jax.ShapeDtypeStruct((B, s_new, w), BF16),
        compiler_params=_params("parallel"),
        name="band_sample",
    )(q, ckt, cvt, kn, vn, bias)


def _cumsum_lanes_kernel(x_ref, out_ref, *, blk, suffix):
    R, T = x_ref.shape
    src = lax.broadcasted_iota(jnp.int32, (blk, blk), 0)
    dst = lax.broadcasted_iota(jnp.int32, (blk, blk), 1)
    tri = ((src > dst) if suffix else (src <= dst)).astype(F32)
    carry = jnp.zeros((R, 1), F32)
    n = T // blk
    for i in (range(n - 1, -1, -1) if suffix else range(n)):
        xb = x_ref[:, i * blk:(i + 1) * blk]
        out_ref[:, i * blk:(i + 1) * blk] = jnp.dot(xb, tri, precision=HIGHEST, preferred_element_type=F32) + carry
        carry = carry + jnp.sum(xb, axis=1, keepdims=True)


def _cumsum_lanes(x, *, suffix):
    return pl.pallas_call(
        functools.partial(_cumsum_lanes_kernel, blk=CUMSUM_BLOCK, suffix=suffix),
        out_shape=jax.ShapeDtypeStruct(x.shape, F32),
        compiler_params=_params(),
        name="cumsum_lanes",
    )(x)


def _fox_prompt_kernel(q_ref, kt_ref, vt_ref, ft_ref, o_ref, *, tq, tk, d):
    p = pl.program_id(1)
    T = q_ref.shape[1]
    assert tq == tk and d % 8 == 0
    nq = T // tq
    lane, mine = _pair_masks(d)
    q_fill = [((lane >= d) & (lane < d + 3)).astype(BF16), (lane < 3).astype(BF16)]
    den = [d, 0]
    sub = lax.broadcasted_iota(jnp.int32, (8, tk), 0)
    ones_blk = (sub == 0).astype(F32)
    pad_blk = jnp.zeros((d - 8, tk), F32)

    def stacked(x, blk, hh):
        return jnp.concatenate([x, blk, pad_blk] if hh == 0 else [blk, pad_blk, x], axis=0)

    diag_mask = (lax.broadcasted_iota(jnp.int32, (tk, tq), 0) <= lax.broadcasted_iota(jnp.int32, (tk, tq), 1))
    npair = q_ref.shape[2] // (2 * d)
    heads = [(pr, hh) for pr in range(npair) for hh in range(2)]
    units = [(qi, pr, hh) for qi in range(nq) for pr, hh in heads]
    qs = {(qi, pr, hh): jnp.where(mine[hh], q_ref[0, qi * tq:(qi + 1) * tq, pr * 2 * d:(pr + 1) * 2 * d], q_fill[hh])
          for qi, pr, hh in units}
    m = {u: jnp.full((1, tq), -jnp.inf, F32) for u in units}
    acc = {u: jnp.zeros((2 * d, tq), F32) for u in units}
    for kv in range(nq):
        ks = slice(kv * tk, (kv + 1) * tk)
        k_rows, v_all = {}, {}
        for pr, hh in heads:
            f = ft_ref[0, pl.ds(2 * (p * npair + pr) + hh, 1), ks] * LOG2E
            hi = f.astype(BF16).astype(F32)
            mid = (f - hi).astype(BF16).astype(F32)
            f_blk = jnp.where(sub == 0, -hi, jnp.where(sub == 1, -mid, jnp.where(sub == 2, -(f - hi - mid), 0.0)))
            k_rows[pr, hh] = stacked(kt_ref[0, 2 * pr + hh, :, ks], f_blk, hh).T.astype(BF16)
            v_all[pr, hh] = stacked(vt_ref[0, 2 * pr + hh, :, ks], ones_blk, hh).astype(BF16)
        live = [u for u in units if u[0] >= kv]
        s = {u: _dot_nt(k_rows[u[1:]], qs[u]) for u in live}
        for u in live:
            if u[0] == kv:
                s[u] = jnp.where(diag_mask, s[u], -jnp.inf)
        m_new = {u: jnp.maximum(m[u], jnp.max(s[u], axis=0, keepdims=True)) for u in live}
        e = {u: jnp.exp2(s[u] - m_new[u]).astype(BF16) for u in live}
        for u in live:
            acc[u] = jnp.exp2(m[u] - m_new[u]) * acc[u] + _dot(v_all[u[1:]], e[u])
            m[u] = m_new[u]
        for pr in range(npair):
            out = [(acc[kv, pr, hh] / acc[kv, pr, hh][den[hh]:den[hh] + 1, :]).T for hh in range(2)]
            o_ref[0, kv * tq:(kv + 1) * tq, pr * 2 * d:(pr + 1) * 2 * d] = jnp.where(mine[0], out[0],
                                                                                   out[1]).astype(BF16)


def _fox_prompt(q, kt, vt, ft, *, tq, tk, npair):
    B, T, w = q.shape
    _, nh, d, _ = kt.shape
    kv = pl.BlockSpec((1, 2 * npair, d, T), lambda b, p: (b, p, 0, 0))
    qo = pl.BlockSpec((1, T, 2 * npair * d), lambda b, p: (b, 0, p))
    return pl.pallas_call(
        functools.partial(_fox_prompt_kernel, tq=tq, tk=tk, d=d),
        grid=(B, nh // (2 * npair)),
        in_specs=[qo, kv, kv, pl.BlockSpec((1, nh, T), lambda b, p: (b, 0, 0))],
        out_specs=qo,
        out_shape=jax.ShapeDtypeStruct((B, T, w), BF16),
        compiler_params=_params("parallel", "arbitrary"),
        name="fox_prompt",
    )(q, kt, vt, ft)


def _fox_sample_kernel(q_ref, kt_ref, vt_ref, ct_ref, kn_ref, vn_ref, lf_ref, o_ref, fqt_scr, *, hg, d, s_new):
    g = pl.program_id(1)
    nh = lf_ref.shape[2]
    tri = (lax.broadcasted_iota(jnp.int32, (s_new, s_new), 0)
           >= lax.broadcasted_iota(jnp.int32, (s_new, s_new), 1))
    eye = (lax.broadcasted_iota(jnp.int32, (nh, nh), 0) == lax.broadcasted_iota(jnp.int32, (nh, nh), 1)).astype(F32)
    fq = jnp.dot(tri.astype(F32), lf_ref[0], precision=HIGHEST, preferred_element_type=F32)
    fqt_scr[...] = lax.dot_general(eye, fq, NT, precision=HIGHEST, preferred_element_type=F32)
    hs = range(hg)
    sl = [slice(hh * d, (hh + 1) * d) for hh in hs]
    qh = [q_ref[0, 0, :, sl[hh]] for hh in hs]
    sc = [_dot(qh[hh], kt_ref[0, hh].astype(BF16)) + ct_ref[0, pl.ds(g * hg + hh, 1), :] for hh in hs]
    sn = [jnp.where(tri, _dot_nt(qh[hh], kn_ref[0, 0, :, sl[hh]]) - fqt_scr[pl.ds(g * hg + hh, 1), :], -jnp.inf)
          for hh in hs]
    m = [jnp.maximum(jnp.max(sc[hh], axis=-1, keepdims=True), jnp.max(sn[hh], axis=-1, keepdims=True)) for hh in hs]
    ec = [jnp.exp(sc[hh] - m[hh]) for hh in hs]
    en = [jnp.exp(sn[hh] - m[hh]) for hh in hs]
    l = [jnp.sum(ec[hh], axis=-1, keepdims=True) + jnp.sum(en[hh], axis=-1, keepdims=True) for hh in hs]
    o = [(_dot_nt(ec[hh].astype(BF16), vt_ref[0, hh].astype(BF16)) + _dot(en[hh].astype(BF16), vn_ref[0, 0, :, sl[hh]]))
         / l[hh] for hh in hs]
    o_ref[0, 0] = jnp.concatenate(o, axis=-1).astype(BF16)


def _fox_sample(q, kt, vt, ct, kn, vn, lf, *, hg, b0):
    _, ng, s_new, gw = q.shape
    nb, nh, d, lc = kt.shape
    new = pl.BlockSpec((1, 1, s_new, gw), lambda b, g: (b0 + b, g, 0, 0))
    old = pl.BlockSpec((1, hg, d, lc), lambda b, g: (b, g, 0, 0))
    return pl.pallas_call(
        functools.partial(_fox_sample_kernel, hg=hg, d=d, s_new=s_new),
        grid=(nb, ng),
        in_specs=[new, old, old, pl.BlockSpec((1, nh, lc), lambda b, g: (b0 + b, 0, 0)), new, new,
                  pl.BlockSpec((1, s_new, nh), lambda b, g: (b0 + b, 0, 0))],
        out_specs=pl.BlockSpec((1, 1, s_new, gw), lambda b, g: (b, g, 0, 0)),
        out_shape=jax.ShapeDtypeStruct((nb,) + q.shape[1:], BF16),
        scratch_shapes=[pltpu.VMEM((nh, s_new), F32)],
        compiler_params=_params("parallel", "arbitrary"),
        name="fox_sample",
    )(q, kt, vt, ct, kn, vn, lf)


def _trunk(x, state, wts, dims, *, tm):
    B, T, D = x.shape
    n = B * T
    depth = wts["norm_g"].shape[0]
    h_a, dk_a, dv_a, h_b, d_b, h_c, d_c = dims
    a_qk, a_v, b_w, c_w = h_a * dk_a, h_a * dv_a, h_b * d_b, h_c * d_c
    xs = x.reshape(n, D)
    new_states = []
    pending = ()
    r3 = lambda t: t.reshape(B, T, t.shape[-1])
    ffn_w = wts["ffn_bf16"]

    def ffn(xs, g, key, proj, final_g=None):
        if state is not None:
            return _ffn(xs, g, *ffn_w[key], proj, final_g, tm=tm)[0]
        steps = n // tm
        nxt = (key[0], 1) if key[1] == 0 else (key[0] + 1, 0)
        jobs = []
        if nxt[0] < depth:
            jobs += [_slab_cast(wts["w_gu"], nxt, D, steps), _slab_cast(wts["w_down"], nxt, ffn_w[key][1].shape[0], steps, 2)]
        for src in wts["hosted_src"]:
            share = src.shape[0] // (2 * depth)
            jobs.append(_slab_cast(src, (), share, steps, first=(2 * key[0] + key[1]) * steps))
        xs, *outs = _ffn(xs, g, *ffn_w[key], proj, final_g, jobs, tm=tm)
        if nxt[0] < depth:
            ffn_w[nxt], outs = outs[:2], outs[2:]
        wts["hosted_out"].append(outs)
        return xs

    for l in range(depth):
        i = l // 2
        g = wts["norm_g"][l]
        xs = ffn(xs, g[0:1], (l, 0), pending)
        if l % 2 == 0:
            prompt = state is None
            keep = min(BAND_CHUNKS * CHUNK, T)
            qa, kat, va, gate, la, lat, bq, bk, bv, *kv32 = _even_in(
                xs, g[1:2], {k: v[i] for k, v in wts["even"].items()}, tm=tm, t_split=(B, T) if prompt else (1, n),
                a_qk=a_qk, a_v=a_v, b_w=b_w, dk_a=dk_a, d_b=d_b, kv32=not prompt)
            if not prompt:
                kat, lat = (t.reshape(a_qk, B, T).transpose(1, 0, 2) for t in (kat, lat))
            nk, nv = _band_tail(xs, g[1:2], wts["even_kvt"][i], B=B, T=T, keep=keep) if prompt else kv32
            s0 = jnp.zeros((B, h_a, dk_a, dv_a), F32) if prompt else state[l][0]
            blk = min(CHUNK, T)
            o_a, s_fin = _gla(r3(qa), kat, r3(va), r3(gate), r3(la), lat, s0, wts["gla_g"][i],
                              bb=GLA_BATCH, L=blk, nsub=GLA_BLOCKS if T % (GLA_BLOCKS * blk) == 0 else 1)
            if prompt:
                o_b = _band_prompt(r3(bq), r3(bk), r3(bv), wts["band_bias"][i], nh=h_b, d=d_b, nsub=BAND_BLOCKS)
                nk, nv = (t.reshape(B, h_b, d_b, keep).transpose(0, 3, 1, 2) for t in (nk, nv))
            else:
                ck, cv = state[l][1], state[l][2]
                o_b = _band_sample(r3(bq), ck.transpose(0, 2, 3, 1), cv.transpose(0, 2, 3, 1), r3(bk), r3(bv),
                                   wts["band_bias"][i], nh=h_b, d=d_b)
                nk, nv = (t.reshape(B, T, h_b, d_b) for t in (nk, nv))
            new_states.append((s_fin, nk, nv))
            pending = ((o_a.reshape(n, a_v), wts["even_out_a"][i]), (o_b.reshape(n, b_w), wts["even_out_b"][i]))
        elif state is None:
            q, kt, vt, lft = _odd_in_t(xs, g[1:2], wts["odd_q"][i], wts["odd_kvt"][i], wts["odd_ft"][i],
                                       wts["b_f_col"][i], B=B, T=T, tm=tm, nh=h_c, d_c=d_c)
            ft = _cumsum_lanes(lft.reshape(B * h_c, T), suffix=False).reshape(B, h_c, T)
            o = _fox_prompt(r3(q), kt, vt, ft, tq=FOX_BLOCK, tk=FOX_BLOCK, npair=FOX_PAIRS)
            new_states.append((kt.transpose(0, 3, 1, 2), vt.transpose(0, 3, 1, 2), lft.transpose(0, 2, 1)))
            pending = ((o.reshape(n, c_w), wts["odd_out"][i]),)
        else:
            q, k, v, k32, v32, lf = _odd_in(xs, g[1:2], wts["odd_main"][i], wts["odd_f"][i], wts["b_f"][i],
                                             tm=tm, c_w=c_w, d_c=d_c, h_c=h_c)
            clf = state[l][2]
            lc = clf.shape[1]
            hg = FOX_SAMPLE_HEADS
            ct = _cumsum_lanes(clf.astype(F32).transpose(0, 2, 1).reshape(B * h_c, lc), suffix=True)
            grp = lambda t: t.reshape(B, T, h_c // hg, hg * d_c).transpose(0, 2, 1, 3)
            hosted = wts["hosted_out"]
            per = B // len(hosted)
            o = jnp.concatenate([
                _fox_sample(grp(q), parts[2 * i].reshape(per, h_c, d_c, lc), parts[2 * i + 1].reshape(per, h_c, d_c, lc),
                            ct.reshape(B, h_c, lc), grp(k), grp(v), r3(lf), hg=hg, b0=j * per)
                for j, parts in enumerate(hosted)], axis=0)
            o = o.transpose(0, 2, 1, 3).reshape(n, c_w)
            new_states.append((r3(k32).reshape(B, T, h_c, d_c), r3(v32).reshape(B, T, h_c, d_c), r3(lf)))
            pending = ((o, wts["odd_out"][i]),)
        last = l == depth - 1
        xs = ffn(xs, g[2:3], (l, 1), pending, wts["final_g"] if last else None)
        pending = ()
    return xs.reshape(B, T, D), new_states


def kernel(x_prompt, x_sample, state_gla, cache_band_k, cache_band_v, cache_fox_k, cache_fox_v, cache_fox_logf,
           norm_g, ffn_w_gu, ffn_w_down, even_w_in, gla_w_alpha_up, gla_b_alpha, gla_norm_g, band_rel_bias,
           even_w_out, odd_w_in, fox_b_f, odd_w_out, final_norm_g):
    depth = norm_g.shape[0]
    _, _, h_a, dk_a, dv_a = state_gla.shape
    h_b, d_b = cache_band_k.shape[-2:]
    h_c, d_c = cache_fox_k.shape[-2:]
    r_a = gla_w_alpha_up.shape[1]
    a_qk, a_v, b_w, c_w = h_a * dk_a, h_a * dv_a, h_b * d_b, h_c * d_c
    assert cache_band_k.shape[2] == BAND_CHUNKS * CHUNK and band_rel_bias.shape[-1] == 2 * MAX_REL + 1

    r0 = 2 * a_qk + 2 * a_v
    n_rel = -(-band_rel_bias.shape[-1] // LANES) * LANES
    rel_padded = jnp.pad(band_rel_bias, ((0, 0), (0, 0), (0, n_rel - band_rel_bias.shape[-1])))
    win = (BAND_CHUNKS + 1) * CHUNK
    odd_t = odd_w_in.transpose(0, 2, 1)
    even_t = even_w_in.transpose(0, 2, 1)
    wts = {
        "norm_g": norm_g,
        "w_gu": ffn_w_gu,
        "w_down": ffn_w_down,
        "ffn_bf16": {(0, 0): [ffn_w_gu[0, 0].astype(BF16), ffn_w_down[0, 0].astype(BF16)]},
        "even": {
            "main": jnp.concatenate([even_w_in[:, :, :r0], even_w_in[:, :, r0 + r_a:]], axis=-1).astype(BF16),
            "r": even_w_in[:, :, r0:r0 + r_a].astype(BF16),
            "up": gla_w_alpha_up.astype(BF16),
            "b_alpha": gla_b_alpha[:, None, :],
        },
        "even_kvt": even_t[:, r0 + r_a + b_w:].astype(BF16),
        "gla_g": gla_norm_g[:, None, :],
        "band_bias": [_bias_table(rel_padded[i], chunk=CHUNK, win=win, rows=BAND_BLOCK_CHUNKS * CHUNK)
                      for i in range(rel_padded.shape[0])],
        "even_out_a": even_w_out[:, :a_v].astype(BF16),
        "even_out_b": even_w_out[:, a_v:].astype(BF16),
        "odd_main": odd_w_in[:, :, :3 * c_w].astype(BF16),
        "odd_f": odd_w_in[:, :, 3 * c_w:].astype(BF16),
        "odd_q": odd_w_in[:, :, :c_w].astype(BF16),
        "odd_kvt": odd_t[:, c_w:3 * c_w].astype(BF16),
        "odd_ft": odd_t[:, 3 * c_w:].astype(BF16),
        "b_f": fox_b_f[:, None, :],
        "b_f_col": fox_b_f[:, :, None],
        "odd_out": odd_w_out.astype(BF16),
        "final_g": final_norm_g[None, :],
    }
    dims = (h_a, dk_a, dv_a, h_b, d_b, h_c, d_c)

    sample_states = []
    wts["hosted_src"], wts["hosted_out"] = [], []
    for l in range(depth):
        i = l // 2
        if l % 2 == 0:
            sample_states.append((state_gla[i], cache_band_k[i], cache_band_v[i]))
        else:
            sample_states.append((cache_fox_k[i], cache_fox_v[i], cache_fox_logf[i]))
            wts["hosted_src"] += [c[i].transpose(0, 2, 3, 1).reshape(-1, c.shape[2]) for c in (cache_fox_k, cache_fox_v)]
    assert len(wts["hosted_src"]) == 2, "the sample's forgetting attention expects one cached odd layer"

    y_prompt, ns_p = _trunk(x_prompt, None, wts, dims, tm=PROMPT_ROWS)
    y_sample, ns_s = _trunk(x_sample, sample_states, wts, dims, tm=x_sample.shape[0] * x_sample.shape[1])

    ev = range(0, depth, 2)
    od = range(1, depth, 2)
    stack = lambda ns, layers, j: jnp.stack([ns[l][j] for l in layers])
    return (y_prompt, y_sample,
            stack(ns_p, ev, 0), stack(ns_p, ev, 1), stack(ns_p, ev, 2),
            stack(ns_p, od, 0), stack(ns_p, od, 1), stack(ns_p, od, 2),
            stack(ns_s, ev, 0), stack(ns_s, ev, 1), stack(ns_s, ev, 2),
            stack(ns_s, od, 0), stack(ns_s, od, 1), stack(ns_s, od, 2))
```

```python
import functools

import jax
import jax.numpy as jnp
from jax import lax
from jax.experimental import pallas as pl
from jax.experimental.pallas import tpu as pltpu

F32 = jnp.float32
BF16 = jnp.bfloat16
NT = (((1,), (1,)), ((), ()))

EPS = 1e-6
CHUNK = 64
BAND_CHUNKS = 8
MAX_REL = 128
GATE_NORM_A = 16.0
LOG2E = 1.4426950408889634

V7X_VMEM_BYTES = 64 * 1024 * 1024
VMEM_LIMIT_BYTES = V7X_VMEM_BYTES * 7 // 8
LANES = 128

PROMPT_ROWS = 512
GLA_BATCH = 8
GLA_BLOCKS = 2
BAND_BLOCK_CHUNKS = 2
BAND_BLOCKS = 4
FOX_BLOCK = 256
FOX_PAIRS = 2
FOX_SAMPLE_HEADS = 8
CUMSUM_BLOCK = 256


def _params(*sem):
    return pltpu.CompilerParams(dimension_semantics=sem, vmem_limit_bytes=VMEM_LIMIT_BYTES)


def _resident(shape):
    nd = len(shape)
    return pl.BlockSpec(shape, lambda *_: (0,) * nd, pipeline_mode=pl.Buffered(1))


def _rows(tm, width):
    return pl.BlockSpec((tm, width), lambda i: (i, 0))


def _rms(x, g):
    ms = jnp.mean(x * x, axis=-1, keepdims=True)
    return x * lax.rsqrt(ms + EPS) * g


def _log_sigmoid(x):
    return jnp.minimum(x, 0.0) - jnp.log1p(jnp.exp(-jnp.abs(x)))


def _dot(a, b):
    return jnp.dot(a, b, preferred_element_type=F32)


def _dot_nt(a, b):
    return lax.dot_general(a, b, NT, preferred_element_type=F32)


def _select_sum(x, sel, side, dot=_dot):
    sel = sel.astype(BF16)
    hi = x.astype(BF16)
    rest = x - hi.astype(F32)
    mid = rest.astype(BF16)
    lo = (rest - mid.astype(F32)).astype(BF16)
    terms = [dot(sel, p) if side == "l" else dot(p, sel) for p in (hi, mid, lo)]
    return terms[0] + terms[1] + terms[2]


def _ffn_kernel(*refs, n_proj, has_final, casts, d_ff):
    it = iter(refs)
    x_ref = next(it)
    proj = [(next(it), next(it)) for _ in range(n_proj)]
    g_ref, wgu_ref, wd_ref = next(it), next(it), next(it)
    gf_ref = next(it) if has_final else None
    cast_src = [next(it) for _ in range(casts)]
    out_ref = next(it)
    for src in cast_src:
        next(it)[...] = src[...].astype(BF16)

    x = x_ref[...]
    for o_ref, w_ref in proj:
        x = x + _dot(o_ref[...], w_ref[...])
    xn = _rms(x, g_ref[...]).astype(BF16)
    h = _dot(xn, wgu_ref[...])
    a = (jax.nn.silu(h[:, :d_ff]) * h[:, d_ff:]).astype(BF16)
    y = x + 0.5 * _dot(a, wd_ref[...])
    if has_final:
        y = _rms(y, gf_ref[...])
    out_ref[...] = y


def _slab_cast(src, lead, rows, steps, per=1, first=0):
    slab = rows * per // steps
    width = src.shape[-1]
    assert rows * per % steps == 0 and slab % 16 == 0
    return (src,
            pl.BlockSpec((None,) * len(lead) + (slab, width), lambda i: tuple(lead) + (first + i // per, 0)),
            pl.BlockSpec((slab, width), lambda i: (i // per, 0)),
            jax.ShapeDtypeStruct((rows, width), BF16))


def _ffn(x, g, wgu, wd, proj=(), final_g=None, casts=(), *, tm):
    n, d = x.shape
    d_ff = wd.shape[0]
    steps = n // tm
    args, specs = [x], [_rows(tm, d)]
    for o, w in proj:
        args += [o, w]
        specs += [_rows(tm, o.shape[1]), _resident(w.shape)]
    args += [g, wgu, wd]
    specs += [_resident(g.shape), _resident(wgu.shape), _resident(wd.shape)]
    if final_g is not None:
        args.append(final_g)
        specs.append(_resident(final_g.shape))
    out_specs, out_shapes = [_rows(tm, d)], [jax.ShapeDtypeStruct((n, d), F32)]
    for src, in_spec, out_spec, out_shape in casts:
        args.append(src)
        specs.append(in_spec)
        out_specs.append(out_spec)
        out_shapes.append(out_shape)
    return pl.pallas_call(
        functools.partial(_ffn_kernel, n_proj=len(proj), has_final=final_g is not None, casts=len(casts), d_ff=d_ff),
        grid=(steps,),
        in_specs=specs,
        out_specs=out_specs,
        out_shape=out_shapes,
        compiler_params=_params("arbitrary" if casts else "parallel"),
        name="ffn",
    )(*args)


def _even_in_kernel(x_ref, g_ref, w_ref, wr_ref, wup_ref, ba_ref,
                    qa_ref, kat_ref, va_ref, gate_ref, la_ref, lat_ref, bq_ref, bk_ref, bv_ref, *kv32_refs,
                    a_qk, a_v, b_w, dk_a, d_b):
    xn = _rms(x_ref[...], g_ref[...]).astype(BF16)
    h = _dot(xn, w_ref[...])
    o = 0
    qa_ref[...] = h[:, o:o + a_qk] * (dk_a ** -0.5)
    o += a_qk
    kat_ref[...] = h[:, o:o + a_qk].T
    o += a_qk
    va_ref[...] = h[:, o:o + a_v].astype(BF16)
    o += a_v
    gate_ref[...] = jax.nn.silu(h[:, o:o + a_v])
    o += a_v
    bq_ref[...] = (h[:, o:o + b_w] * (LOG2E * d_b ** -0.5)).astype(BF16)
    o += b_w
    bk = h[:, o:o + b_w]
    o += b_w
    bv = h[:, o:o + b_w]
    bk_ref[...] = bk.astype(BF16)
    bv_ref[...] = bv.astype(BF16)
    if kv32_refs:
        kv32_refs[0][...] = bk
        kv32_refs[1][...] = bv
    r = _dot(xn, wr_ref[...]).astype(BF16)
    la = _log_sigmoid(_dot(r, wup_ref[...]) + ba_ref[...]) / GATE_NORM_A
    la_ref[...] = la
    lat_ref[...] = la.T


def _even_in(x, g, w, *, tm, t_split, a_qk, a_v, b_w, dk_a, d_b, kv32):
    n, d = x.shape
    bo, to = t_split
    nb = to // tm
    assert bo * to == n and to % tm == 0
    tmin = (pl.BlockSpec((None, a_qk, tm), lambda i: (i // nb, 0, i % nb)), jax.ShapeDtypeStruct((bo, a_qk, to), F32))
    tok = lambda width, dt: (_rows(tm, width), jax.ShapeDtypeStruct((n, width), dt))
    outs = [tok(a_qk, F32), tmin, tok(a_v, BF16), tok(a_v, F32), tok(a_qk, F32), tmin,
            tok(b_w, BF16), tok(b_w, BF16), tok(b_w, BF16)]
    if kv32:
        outs += [tok(b_w, F32), tok(b_w, F32)]
    args = [x, g] + [w[k] for k in ("main", "r", "up", "b_alpha")]
    return pl.pallas_call(
        functools.partial(_even_in_kernel, a_qk=a_qk, a_v=a_v, b_w=b_w, dk_a=dk_a, d_b=d_b),
        grid=(n // tm,),
        in_specs=[_rows(tm, d)] + [_resident(a.shape) for a in args[1:]],
        out_specs=[s for s, _ in outs],
        out_shape=[t for _, t in outs],
        compiler_params=_params("parallel"),
        name="even_in",
    )(*args)


def _band_tail_kernel(x_ref, g_ref, wkvt_ref, kt_ref, vt_ref):
    xn = _rms(x_ref[...], g_ref[...]).astype(BF16)
    kvt = _dot_nt(wkvt_ref[...], xn)
    b_w = kt_ref.shape[1]
    kt_ref[0] = kvt[:b_w]
    vt_ref[0] = kvt[b_w:]


def _band_tail(x, g, w_kvt, *, B, T, keep):
    n, d = x.shape
    b_w = w_kvt.shape[0] // 2
    assert T % keep == 0
    nb = T // keep
    out = pl.BlockSpec((1, b_w, keep), lambda b: (b, 0, 0))
    return pl.pallas_call(
        _band_tail_kernel,
        grid=(B,),
        in_specs=[pl.BlockSpec((keep, d), lambda b: (b * nb + nb - 1, 0)), _resident(g.shape), _resident(w_kvt.shape)],
        out_specs=[out, out],
        out_shape=[jax.ShapeDtypeStruct((B, b_w, keep), F32)] * 2,
        compiler_params=_params("parallel"),
        name="band_tail",
    )(x, g, w_kvt)


def _odd_in_kernel(x_ref, g_ref, w_ref, wf_ref, bf_ref, q_ref, k_ref, v_ref, k32_ref, v32_ref, lf_ref, *, c_w, d_c):
    xn = _rms(x_ref[...], g_ref[...]).astype(BF16)
    h = _dot(xn, w_ref[...])
    q_ref[...] = (h[:, :c_w] * (d_c ** -0.5)).astype(BF16)
    k = h[:, c_w:2 * c_w]
    v = h[:, 2 * c_w:]
    k32_ref[...] = k
    v32_ref[...] = v
    k_ref[...] = k.astype(BF16)
    v_ref[...] = v.astype(BF16)
    lf_ref[...] = _log_sigmoid(_dot(xn, wf_ref[...]) + bf_ref[...])


def _odd_in(x, g, w_main, w_f, b_f, *, tm, c_w, d_c, h_c):
    n, d = x.shape
    outs = [(c_w, BF16), (c_w, BF16), (c_w, BF16), (c_w, F32), (c_w, F32), (h_c, F32)]
    return pl.pallas_call(
        functools.partial(_odd_in_kernel, c_w=c_w, d_c=d_c),
        grid=(n // tm,),
        in_specs=[_rows(tm, d), _resident(g.shape), _resident(w_main.shape), _resident(w_f.shape),
                  _resident(b_f.shape)],
        out_specs=[_rows(tm, w) for w, _ in outs],
        out_shape=[jax.ShapeDtypeStruct((n, w), dt) for w, dt in outs],
        compiler_params=_params("parallel"),
        name="odd_in",
    )(x, g, w_main, w_f, b_f)


def _odd_in_t_kernel(x_ref, g_ref, wq_ref, wkvt_ref, wft_ref, bf_ref, q_ref, kt_ref, vt_ref, lft_ref, *, nh, d_c):
    xn = _rms(x_ref[...], g_ref[...]).astype(BF16)
    tm = xn.shape[0]
    c_w = nh * d_c
    q_ref[...] = (_dot(xn, wq_ref[...]) * (LOG2E * d_c ** -0.5)).astype(BF16)
    ht = _dot_nt(wkvt_ref[...], xn)
    kt_ref[0] = ht[:c_w].reshape(nh, d_c, tm)
    vt_ref[0] = ht[c_w:].reshape(nh, d_c, tm)
    lft_ref[0] = _log_sigmoid(_dot_nt(wft_ref[...], xn) + bf_ref[...])


def _odd_in_t(x, g, wq, wkvt, wft, b_f_col, *, B, T, tm, nh, d_c):
    n, d = x.shape
    nb = T // tm
    c_w = nh * d_c
    tok = lambda w: pl.BlockSpec((tm, w), lambda b, i: (b * nb + i, 0))
    kv = pl.BlockSpec((1, nh, d_c, tm), lambda b, i: (b, 0, 0, i))
    return pl.pallas_call(
        functools.partial(_odd_in_t_kernel, nh=nh, d_c=d_c),
        grid=(B, nb),
        in_specs=[tok(d), _resident(g.shape), _resident(wq.shape), _resident(wkvt.shape), _resident(wft.shape),
                  _resident(b_f_col.shape)],
        out_specs=[tok(c_w), kv, kv, pl.BlockSpec((1, nh, tm), lambda b, i: (b, 0, i))],
        out_shape=[jax.ShapeDtypeStruct((n, c_w), BF16), jax.ShapeDtypeStruct((B, nh, d_c, T), F32),
                   jax.ShapeDtypeStruct((B, nh, d_c, T), F32), jax.ShapeDtypeStruct((B, nh, T), F32)],
        compiler_params=_params("parallel", "parallel"),
        name="odd_in_t",
    )(x, g, wq, wkvt, wft, b_f_col)


def _gla_kernel(q_ref, kt_ref, v_ref, gate_ref, la_ref, lat_ref, s0_ref, g_ref, o_ref, sfin_ref, s_scr,
                *, bb, L, nh, dk, dv):
    c = pl.program_id(1)

    @pl.when(c == 0)
    def _():
        s_scr[...] = s0_ref[...]

    tw = q_ref.shape[1]
    nsub = tw // L
    w = nh * dk
    shift = L.bit_length() - 1
    assert L == 1 << shift and nh % 2 == 0
    tri = (lax.broadcasted_iota(jnp.int32, (L, L), 0) >= lax.broadcasted_iota(jnp.int32, (L, L), 1)).astype(F32)
    t0 = lax.broadcasted_iota(jnp.int32, (tw, tw), 0)
    t1 = lax.broadcasted_iota(jnp.int32, (tw, tw), 1)
    tri_t = (((t0 >> shift) == (t1 >> shift)) & (t0 <= t1)).astype(F32)
    lane_t = lax.broadcasted_iota(jnp.int32, (1, tw), 1)
    row_i = lax.broadcasted_iota(jnp.int32, (L, tw), 0)
    _, mine = _pair_masks(dk)
    g = g_ref[...]
    bs = range(bb)
    zero = jnp.zeros((), BF16)

    bt_all = _select_sum(jnp.concatenate([lat_ref[b] for b in bs], axis=0), tri_t, "r")
    k_in, k_st, a_col = [], [], []
    for b in bs:
        bt = bt_all[b * w:(b + 1) * w]
        kt = kt_ref[b]
        last = [bt[:, (cc + 1) * L - 1:(cc + 1) * L] for cc in range(nsub)]
        last_sel = last[0]
        for cc in range(1, nsub):
            last_sel = jnp.where(lane_t >= cc * L, last[cc], last_sel)
        k_in.append((kt * jnp.exp(-bt)).astype(BF16))
        k_st.append((kt * jnp.exp(last_sel - bt)).astype(BF16))
        a_col.append([jnp.exp(x) for x in last])
    q_in = {}
    for cc in range(nsub):
        rows = slice(cc * L, (cc + 1) * L)
        bc = _select_sum(jnp.concatenate([la_ref[b, rows] for b in bs], axis=1), tri, "l")
        for b in bs:
            q_in[b, cc] = (q_ref[b, rows] * jnp.exp(bc[:, b * w:(b + 1) * w])).astype(BF16)

    ccs = range(nsub)
    pairs = [(b, p) for b in bs for p in range(nh // 2)]
    heads = [(b, p, hh) for b, p in pairs for hh in range(2)]
    pair = lambda p: slice(p * 2 * dk, (p + 1) * 2 * dk)
    in_blk = [(lane_t >= cc * L) & (lane_t < (cc + 1) * L) for cc in ccs]
    causal = [in_blk[cc] & (lane_t - cc * L <= row_i) for cc in ccs]
    qm = {(b, p, hh, cc): jnp.where(mine[hh], q_in[b, cc][:, pair(p)], zero) for b, p, hh in heads for cc in ccs}
    sc = {(b, p): _dot(jnp.concatenate([qm[b, p, hh, cc] for cc in ccs for hh in range(2)], axis=0), k_in[b][pair(p)])
          for b, p in pairs}
    att = {(b, p, hh, cc): jnp.where(causal[cc], sc[b, p][(2 * cc + hh) * L:(2 * cc + hh + 1) * L], 0.0).astype(BF16)
           for b, p, hh in heads for cc in ccs}
    inc, o_in = {}, {}
    for b, p, hh in heads:
        h = 2 * p + hh
        ks = [jnp.where(in_blk[cc], k_st[b][h * dk:(h + 1) * dk], zero) for cc in ccs]
        r = _dot(jnp.concatenate([att[b, p, hh, cc] for cc in ccs] + ks, axis=0), v_ref[b, :, h * dv:(h + 1) * dv])
        for cc in ccs:
            o_in[b, h, cc] = r[cc * L:(cc + 1) * L]
            inc[b, h, cc] = r[nsub * L + cc * dk:nsub * L + (cc + 1) * dk]
    state = {(b, h): s_scr[b, h] for b in bs for h in range(nh)}
    for cc in ccs:
        rows = slice(cc * L, (cc + 1) * L)
        o = {}
        for b, p in pairs:
            s_pair = jnp.concatenate([state[b, 2 * p], state[b, 2 * p + 1]], axis=0).astype(BF16)
            carried = _dot(jnp.concatenate([qm[b, p, 0, cc], qm[b, p, 1, cc]], axis=0), s_pair)
            for hh in range(2):
                h = 2 * p + hh
                o[b, h] = o_in[b, h, cc] + carried[hh * L:(hh + 1) * L]
                state[b, h] = a_col[b][cc][h * dk:(h + 1) * dk] * state[b, h] + inc[b, h, cc]
        o = {k: x * lax.rsqrt(jnp.mean(x * x, axis=-1, keepdims=True) + EPS) * g for k, x in o.items()}
        for b in bs:
            o_ref[b, rows] = (jnp.concatenate([o[b, h] for h in range(nh)], axis=-1) * gate_ref[b, rows]).astype(BF16)
    for (b, h), x in state.items():
        s_scr[b, h] = x

    @pl.when(c == pl.num_programs(1) - 1)
    def _():
        sfin_ref[...] = s_scr[...]


def _gla(q, kt, v, gate, la, lat, s0, g, *, bb, L, nsub):
    B, T, a_qk = q.shape
    a_v = v.shape[-1]
    _, nh, dk, dv = s0.shape
    tw = L * nsub
    assert T % tw == 0

    def tok(w):
        return pl.BlockSpec((bb, tw, w), lambda i, c: (i, c, 0))

    tmin = pl.BlockSpec((bb, a_qk, tw), lambda i, c: (i, 0, c))
    st = pl.BlockSpec((bb, nh, dk, dv), lambda i, c: (i, 0, 0, 0))
    return pl.pallas_call(
        functools.partial(_gla_kernel, bb=bb, L=L, nh=nh, dk=dk, dv=dv),
        grid=(B // bb, T // tw),
        in_specs=[tok(a_qk), tmin, tok(a_v), tok(a_v), tok(a_qk), tmin, st,
                  pl.BlockSpec(g.shape, lambda i, c: (0, 0))],
        out_specs=[tok(a_v), st],
        out_shape=[jax.ShapeDtypeStruct((B, T, a_v), BF16), jax.ShapeDtypeStruct(s0.shape, F32)],
        scratch_shapes=[pltpu.VMEM((bb, nh, dk, dv), F32)],
        compiler_params=_params("parallel", "arbitrary"),
        name="gla",
    )(q, kt, v, gate, la, lat, s0, g)


def _bias_table_kernel(rel_ref, out_ref, *, chunk, win, n_rel):
    n_var, nh, rows, width = out_ref.shape
    back = win - chunk
    wide = width + back + rows
    assert wide % LANES == 0 and chunk & (chunk - 1) == 0
    u = lax.broadcasted_iota(jnp.int32, (n_rel, wide), 1)
    want = jnp.clip(back + rows - 1 - u, -MAX_REL, MAX_REL) + MAX_REL
    onehot = (lax.broadcasted_iota(jnp.int32, (n_rel, wide), 0) == want).astype(F32)
    vec = _select_sum(rel_ref[...], onehot, "r")
    i = lax.broadcasted_iota(jnp.int32, (rows, width), 0)
    j = lax.broadcasted_iota(jnp.int32, (rows, width), 1)
    first = i - (i & (chunk - 1))
    for t in range(n_var):
        shift = back - t * rows if t < n_var - 1 else 0
        ok = (j + shift >= first) & (j + shift < first + win)
        for h in range(nh):
            x = jnp.broadcast_to(vec[h:h + 1, :], (rows, wide))
            x = pltpu.roll(x, (wide - (shift + rows - 1)) % wide, 1, stride=1, stride_axis=0)
            out_ref[t, h] = jnp.where(ok, x[:, :width] * LOG2E, -jnp.inf)


def _bias_table(rel_padded, *, chunk, win, rows):
    nh, n_rel = rel_padded.shape
    n_var = (win - chunk) // rows + 1
    return pl.pallas_call(
        functools.partial(_bias_table_kernel, chunk=chunk, win=win, n_rel=n_rel),
        out_shape=jax.ShapeDtypeStruct((n_var, nh, rows, win + rows - chunk), F32),
        compiler_params=_params(),
        name="band_bias_table",
    )(rel_padded)


def _pair_masks(d):
    lane = lax.broadcasted_iota(jnp.int32, (1, 2 * d), 1)
    return lane, [lane < d, lane >= d]


def _band_prompt_kernel(q_ref, k_ref, v_ref, bias_ref, o_ref, vx_scr, *, nh, d, back):
    n_var, _, tq, width = bias_ref.shape
    nsub = q_ref.shape[1] // tq
    lane, mine = _pair_masks(d)
    ones_lane = [d, 0]
    cols = [slice(p * 2 * d, (p + 1) * 2 * d) for p in range(nh // 2)]
    zero = jnp.zeros((), BF16)

    @pl.when(pl.program_id(1) == 0)
    def _():
        in_pair = lax.broadcasted_iota(jnp.int32, (1, nh * d), 1) & (2 * d - 1)
        v = v_ref[0]
        vx_scr[0] = jnp.where(in_pair < d, v, (in_pair == ones_lane[0]).astype(BF16))
        vx_scr[1] = jnp.where(in_pair >= d, v, (in_pair == ones_lane[1]).astype(BF16))

    q, kw, vw, var = [], [], [], []
    for sb in range(nsub):
        c = pl.program_id(1) * nsub + sb
        start = pl.multiple_of(jnp.maximum(c * tq - back, 0), tq)
        var.append(jnp.minimum(c, n_var - 1))
        q.append(q_ref[0, sb * tq:(sb + 1) * tq, :])
        kw.append(k_ref[0, pl.ds(start, width), :])
        vw.append([vx_scr[hh, pl.ds(start, width), :] for hh in range(2)])
    units = [(sb, p, hh) for sb in range(nsub) for p in range(nh // 2) for hh in range(2)]
    s = [_dot_nt(jnp.where(mine[hh], q[sb][:, cols[p]], zero), kw[sb][:, cols[p]]) + bias_ref[var[sb], 2 * p + hh]
         for sb, p, hh in units]
    m = [jnp.max(x, axis=-1, keepdims=True) for x in s]
    e = [jnp.exp2(x - mx).astype(BF16) for x, mx in zip(s, m)]
    r = [_dot(e[u], vw[sb][hh][:, cols[p]]) for u, (sb, p, hh) in enumerate(units)]
    r = [x / x[:, ones_lane[hh]:ones_lane[hh] + 1] for x, (sb, p, hh) in zip(r, units)]
    for sb in range(nsub):
        rs = r[sb * nh:(sb + 1) * nh]
        o_ref[0, sb * tq:(sb + 1) * tq, :] = jnp.concatenate(
            [jnp.where(mine[0], rs[2 * p], rs[2 * p + 1]) for p in range(nh // 2)], axis=-1).astype(BF16)


def _band_prompt(q, k, v, bias, *, nh, d, nsub):
    B, T, w = q.shape
    n_var, _, tq, width = bias.shape
    tq *= nsub
    assert T >= width and T % tq == 0
    kv = pl.BlockSpec((1, T, w), lambda b, c: (b, 0, 0))
    return pl.pallas_call(
        functools.partial(_band_prompt_kernel, nh=nh, d=d, back=width - tq // nsub),
        grid=(B, T // tq),
        in_specs=[pl.BlockSpec((1, tq, w), lambda b, c: (b, c, 0)), kv, kv, _resident(bias.shape)],
        out_specs=pl.BlockSpec((1, tq, w), lambda b, c: (b, c, 0)),
        out_shape=jax.ShapeDtypeStruct((B, T, w), BF16),
        scratch_shapes=[pltpu.VMEM((2, T, w), BF16)],
        compiler_params=_params("parallel", "arbitrary"),
        name="band_prompt",
    )(q, k, v, bias)


def _band_sample_kernel(q_ref, ckt_ref, cvt_ref, kn_ref, vn_ref, bias_ref, o_ref, *, nh, d, s_new):
    lc = ckt_ref.shape[3]
    q = q_ref[0]
    kn = kn_ref[0]
    vn = vn_ref[0]
    hs = range(nh)
    sl = [slice(h * d, (h + 1) * d) for h in hs]
    sc = [_dot(q[:, sl[h]], ckt_ref[0, h].astype(BF16)) + bias_ref[0, h][:s_new, :lc] for h in hs]
    sn = [_dot_nt(q[:, sl[h]], kn[:, sl[h]]) + bias_ref[0, h][:s_new, lc:lc + s_new] for h in hs]
    m = [jnp.maximum(jnp.max(sc[h], axis=-1, keepdims=True), jnp.max(sn[h], axis=-1, keepdims=True)) for h in hs]
    ec = [jnp.exp2(sc[h] - m[h]) for h in hs]
    en = [jnp.exp2(sn[h] - m[h]) for h in hs]
    l = [jnp.sum(ec[h], axis=-1, keepdims=True) + jnp.sum(en[h], axis=-1, keepdims=True) for h in hs]
    o = [(_dot_nt(ec[h].astype(BF16), cvt_ref[0, h].astype(BF16)) + _dot(en[h].astype(BF16), vn[:, sl[h]])) / l[h]
         for h in hs]
    o_ref[0] = jnp.concatenate(o, axis=-1).astype(BF16)


def _band_sample(q, ckt, cvt, kn, vn, bias, *, nh, d):
    B, s_new, w = q.shape
    lc = ckt.shape[3]
    n_var, _, rows, width = bias.shape
    assert s_new <= CHUNK and lc == width - rows
    new = pl.BlockSpec((1, s_new, w), lambda b: (b, 0, 0))
    old = pl.BlockSpec((1, nh, d, lc), lambda b: (b, 0, 0, 0))
    return pl.pallas_call(
        functools.partial(_band_sample_kernel, nh=nh, d=d, s_new=s_new),
        grid=(B,),
        in_specs=[new, old, old, new, new, pl.BlockSpec((1, nh, rows, width), lambda b: (n_var - 1, 0, 0, 0))],
        out_specs=new,
        out_shape=jax.ShapeDtypeStruct((B, s_new, w), BF16),
        compiler_params=_params("parallel"),
        name="band_sample",
    )(q, ckt, cvt, kn, vn, bias)


def _cumsum_lanes_kernel(x_ref, out_ref, *, blk, suffix):
    R, T = x_ref.shape
    src = lax.broadcasted_iota(jnp.int32, (blk, blk), 0)
    dst = lax.broadcasted_iota(jnp.int32, (blk, blk), 1)
    tri = ((src > dst) if suffix else (src <= dst)).astype(F32)
    carry = jnp.zeros((R, 1), F32)
    n = T // blk
    for i in (range(n - 1, -1, -1) if suffix else range(n)):
        xb = x_ref[:, i * blk:(i + 1) * blk]
        out_ref[:, i * blk:(i + 1) * blk] = _select_sum(xb, tri, "r") + carry
        carry = carry + jnp.sum(xb, axis=1, keepdims=True)


def _cumsum_lanes(x, *, suffix):
    return pl.pallas_call(
        functools.partial(_cumsum_lanes_kernel, blk=CUMSUM_BLOCK, suffix=suffix),
        out_shape=jax.ShapeDtypeStruct(x.shape, F32),
        compiler_params=_params(),
        name="cumsum_lanes",
    )(x)


def _fox_prompt_kernel(q_ref, kt_ref, vt_ref, ft_ref, o_ref, *, tq, tk, d):
    p = pl.program_id(1)
    T = q_ref.shape[1]
    assert tq == tk and d % 8 == 0
    nq = T // tq
    lane, mine = _pair_masks(d)
    q_fill = [((lane >= d) & (lane < d + 3)).astype(BF16), (lane < 3).astype(BF16)]
    den = [d, 0]
    sub = lax.broadcasted_iota(jnp.int32, (8, tk), 0)
    ones_blk = (sub == 0).astype(F32)
    pad_blk = jnp.zeros((d - 8, tk), F32)

    def stacked(x, blk, hh):
        return jnp.concatenate([x, blk, pad_blk] if hh == 0 else [blk, pad_blk, x], axis=0)

    diag_mask = (lax.broadcasted_iota(jnp.int32, (tk, tq), 0) <= lax.broadcasted_iota(jnp.int32, (tk, tq), 1))
    npair = q_ref.shape[2] // (2 * d)
    heads = [(pr, hh) for pr in range(npair) for hh in range(2)]
    units = [(qi, pr, hh) for qi in range(nq) for pr, hh in heads]
    qs = {(qi, pr, hh): jnp.where(mine[hh], q_ref[0, qi * tq:(qi + 1) * tq, pr * 2 * d:(pr + 1) * 2 * d], q_fill[hh])
          for qi, pr, hh in units}
    m = {u: jnp.full((1, tq), -jnp.inf, F32) for u in units}
    acc = {u: jnp.zeros((2 * d, tq), F32) for u in units}
    for kv in range(nq):
        ks = slice(kv * tk, (kv + 1) * tk)
        k_rows, v_all = {}, {}
        for pr, hh in heads:
            f = ft_ref[0, pl.ds(2 * (p * npair + pr) + hh, 1), ks] * LOG2E
            hi = f.astype(BF16).astype(F32)
            mid = (f - hi).astype(BF16).astype(F32)
            f_blk = jnp.where(sub == 0, -hi, jnp.where(sub == 1, -mid, jnp.where(sub == 2, -(f - hi - mid), 0.0)))
            k_rows[pr, hh] = stacked(kt_ref[0, 2 * pr + hh, :, ks], f_blk, hh).T.astype(BF16)
            v_all[pr, hh] = stacked(vt_ref[0, 2 * pr + hh, :, ks], ones_blk, hh).astype(BF16)
        live = [u for u in units if u[0] >= kv]
        s = {u: _dot_nt(k_rows[u[1:]], qs[u]) for u in live}
        for u in live:
            if u[0] == kv:
                s[u] = jnp.where(diag_mask, s[u], -jnp.inf)
        m_new = {u: jnp.maximum(m[u], jnp.max(s[u], axis=0, keepdims=True)) for u in live}
        e = {u: jnp.exp2(s[u] - m_new[u]).astype(BF16) for u in live}
        for u in live:
            acc[u] = jnp.exp2(m[u] - m_new[u]) * acc[u] + _dot(v_all[u[1:]], e[u])
            m[u] = m_new[u]
        for pr in range(npair):
            out = [(acc[kv, pr, hh] / acc[kv, pr, hh][den[hh]:den[hh] + 1, :]).T for hh in range(2)]
            o_ref[0, kv * tq:(kv + 1) * tq, pr * 2 * d:(pr + 1) * 2 * d] = jnp.where(mine[0], out[0],
                                                                                   out[1]).astype(BF16)


def _fox_prompt(q, kt, vt, ft, *, tq, tk, npair):
    B, T, w = q.shape
    _, nh, d, _ = kt.shape
    kv = pl.BlockSpec((1, 2 * npair, d, T), lambda b, p: (b, p, 0, 0))
    qo = pl.BlockSpec((1, T, 2 * npair * d), lambda b, p: (b, 0, p))
    return pl.pallas_call(
        functools.partial(_fox_prompt_kernel, tq=tq, tk=tk, d=d),
        grid=(B, nh // (2 * npair)),
        in_specs=[qo, kv, kv, pl.BlockSpec((1, nh, T), lambda b, p: (b, 0, 0))],
        out_specs=qo,
        out_shape=jax.ShapeDtypeStruct((B, T, w), BF16),
        compiler_params=_params("parallel", "arbitrary"),
        name="fox_prompt",
    )(q, kt, vt, ft)


def _fox_sample_kernel(q_ref, kt_ref, vt_ref, ct_ref, kn_ref, vn_ref, lf_ref, o_ref, fqt_scr, *, hg, d, s_new):
    g = pl.program_id(1)
    nh = lf_ref.shape[2]
    tri = (lax.broadcasted_iota(jnp.int32, (s_new, s_new), 0)
           >= lax.broadcasted_iota(jnp.int32, (s_new, s_new), 1))
    eye = (lax.broadcasted_iota(jnp.int32, (nh, nh), 0) == lax.broadcasted_iota(jnp.int32, (nh, nh), 1)).astype(F32)
    fq = _select_sum(lf_ref[0], tri, "l")
    fqt_scr[...] = _select_sum(fq, eye, "l", _dot_nt)
    hs = range(hg)
    sl = [slice(hh * d, (hh + 1) * d) for hh in hs]
    qh = [q_ref[0, 0, :, sl[hh]] for hh in hs]
    sc = [_dot(qh[hh], kt_ref[0, hh].astype(BF16)) + ct_ref[0, pl.ds(g * hg + hh, 1), :] for hh in hs]
    sn = [jnp.where(tri, _dot_nt(qh[hh], kn_ref[0, 0, :, sl[hh]]) - fqt_scr[pl.ds(g * hg + hh, 1), :], -jnp.inf)
          for hh in hs]
    m = [jnp.maximum(jnp.max(sc[hh], axis=-1, keepdims=True), jnp.max(sn[hh], axis=-1, keepdims=True)) for hh in hs]
    ec = [jnp.exp(sc[hh] - m[hh]) for hh in hs]
    en = [jnp.exp(sn[hh] - m[hh]) for hh in hs]
    l = [jnp.sum(ec[hh], axis=-1, keepdims=True) + jnp.sum(en[hh], axis=-1, keepdims=True) for hh in hs]
    o = [(_dot_nt(ec[hh].astype(BF16), vt_ref[0, hh].astype(BF16)) + _dot(en[hh].astype(BF16), vn_ref[0, 0, :, sl[hh]]))
         / l[hh] for hh in hs]
    o_ref[0, 0] = jnp.concatenate(o, axis=-1).astype(BF16)


def _fox_sample(q, kt, vt, ct, kn, vn, lf, *, hg, b0):
    _, ng, s_new, gw = q.shape
    nb, nh, d, lc = kt.shape
    new = pl.BlockSpec((1, 1, s_new, gw), lambda b, g: (b0 + b, g, 0, 0))
    old = pl.BlockSpec((1, hg, d, lc), lambda b, g: (b, g, 0, 0))
    return pl.pallas_call(
        functools.partial(_fox_sample_kernel, hg=hg, d=d, s_new=s_new),
        grid=(nb, ng),
        in_specs=[new, old, old, pl.BlockSpec((1, nh, lc), lambda b, g: (b0 + b, 0, 0)), new, new,
                  pl.BlockSpec((1, s_new, nh), lambda b, g: (b0 + b, 0, 0))],
        out_specs=pl.BlockSpec((1, 1, s_new, gw), lambda b, g: (b, g, 0, 0)),
        out_shape=jax.ShapeDtypeStruct((nb,) + q.shape[1:], BF16),
        scratch_shapes=[pltpu.VMEM((nh, s_new), F32)],
        compiler_params=_params("parallel", "arbitrary"),
        name="fox_sample",
    )(q, kt, vt, ct, kn, vn, lf)


def _trunk(x, state, wts, dims, *, tm):
    B, T, D = x.shape
    n = B * T
    depth = wts["norm_g"].shape[0]
    h_a, dk_a, dv_a, h_b, d_b, h_c, d_c = dims
    a_qk, a_v, b_w, c_w = h_a * dk_a, h_a * dv_a, h_b * d_b, h_c * d_c
    xs = x.reshape(n, D)
    new_states = []
    pending = ()
    r3 = lambda t: t.reshape(B, T, t.shape[-1])
    ffn_w = wts["ffn_bf16"]

    def ffn(xs, g, key, proj, final_g=None):
        if state is not None:
            return _ffn(xs, g, *ffn_w[key], proj, final_g, tm=tm)[0]
        steps = n // tm
        nxt = (key[0], 1) if key[1] == 0 else (key[0] + 1, 0)
        jobs = []
        if nxt[0] < depth:
            jobs += [_slab_cast(wts["w_gu"], nxt, D, steps), _slab_cast(wts["w_down"], nxt, ffn_w[key][1].shape[0], steps, 2)]
        for src in wts["hosted_src"]:
            share = src.shape[0] // (2 * depth)
            jobs.append(_slab_cast(src, (), share, steps, first=(2 * key[0] + key[1]) * steps))
        xs, *outs = _ffn(xs, g, *ffn_w[key], proj, final_g, jobs, tm=tm)
        if nxt[0] < depth:
            ffn_w[nxt], outs = outs[:2], outs[2:]
        wts["hosted_out"].append(outs)
        return xs

    for l in range(depth):
        i = l // 2
        g = wts["norm_g"][l]
        xs = ffn(xs, g[0:1], (l, 0), pending)
        if l % 2 == 0:
            prompt = state is None
            keep = min(BAND_CHUNKS * CHUNK, T)
            qa, kat, va, gate, la, lat, bq, bk, bv, *kv32 = _even_in(
                xs, g[1:2], {k: v[i] for k, v in wts["even"].items()}, tm=tm, t_split=(B, T) if prompt else (1, n),
                a_qk=a_qk, a_v=a_v, b_w=b_w, dk_a=dk_a, d_b=d_b, kv32=not prompt)
            if not prompt:
                kat, lat = (t.reshape(a_qk, B, T).transpose(1, 0, 2) for t in (kat, lat))
            nk, nv = _band_tail(xs, g[1:2], wts["even_kvt"][i], B=B, T=T, keep=keep) if prompt else kv32
            s0 = jnp.zeros((B, h_a, dk_a, dv_a), F32) if prompt else state[l][0]
            blk = min(CHUNK, T)
            o_a, s_fin = _gla(r3(qa), kat, r3(va), r3(gate), r3(la), lat, s0, wts["gla_g"][i],
                              bb=GLA_BATCH, L=blk, nsub=GLA_BLOCKS if T % (GLA_BLOCKS * blk) == 0 else 1)
            if prompt:
                o_b = _band_prompt(r3(bq), r3(bk), r3(bv), wts["band_bias"][i], nh=h_b, d=d_b, nsub=BAND_BLOCKS)
                nk, nv = (t.reshape(B, h_b, d_b, keep).transpose(0, 3, 1, 2) for t in (nk, nv))
            else:
                ck, cv = state[l][1], state[l][2]
                o_b = _band_sample(r3(bq), ck.transpose(0, 2, 3, 1), cv.transpose(0, 2, 3, 1), r3(bk), r3(bv),
                                   wts["band_bias"][i], nh=h_b, d=d_b)
                nk, nv = (t.reshape(B, T, h_b, d_b) for t in (nk, nv))
            new_states.append((s_fin, nk, nv))
            pending = ((o_a.reshape(n, a_v), wts["even_out_a"][i]), (o_b.reshape(n, b_w), wts["even_out_b"][i]))
        elif state is None:
            q, kt, vt, lft = _odd_in_t(xs, g[1:2], wts["odd_q"][i], wts["odd_kvt"][i], wts["odd_ft"][i],
                                       wts["b_f_col"][i], B=B, T=T, tm=tm, nh=h_c, d_c=d_c)
            ft = _cumsum_lanes(lft.reshape(B * h_c, T), suffix=False).reshape(B, h_c, T)
            o = _fox_prompt(r3(q), kt, vt, ft, tq=FOX_BLOCK, tk=FOX_BLOCK, npair=FOX_PAIRS)
            new_states.append((kt.transpose(0, 3, 1, 2), vt.transpose(0, 3, 1, 2), lft.transpose(0, 2, 1)))
            pending = ((o.reshape(n, c_w), wts["odd_out"][i]),)
        else:
            q, k, v, k32, v32, lf = _odd_in(xs, g[1:2], wts["odd_main"][i], wts["odd_f"][i], wts["b_f"][i],
                                             tm=tm, c_w=c_w, d_c=d_c, h_c=h_c)
            clf = state[l][2]
            lc = clf.shape[1]
            hg = FOX_SAMPLE_HEADS
            ct = _cumsum_lanes(clf.astype(F32).transpose(0, 2, 1).reshape(B * h_c, lc), suffix=True)
            grp = lambda t: t.reshape(B, T, h_c // hg, hg * d_c).transpose(0, 2, 1, 3)
            hosted = wts["hosted_out"]
            per = B // len(hosted)
            o = jnp.concatenate([
                _fox_sample(grp(q), parts[2 * i].reshape(per, h_c, d_c, lc), parts[2 * i + 1].reshape(per, h_c, d_c, lc),
                            ct.reshape(B, h_c, lc), grp(k), grp(v), r3(lf), hg=hg, b0=j * per)
                for j, parts in enumerate(hosted)], axis=0)
            o = o.transpose(0, 2, 1, 3).reshape(n, c_w)
            new_states.append((r3(k32).reshape(B, T, h_c, d_c), r3(v32).reshape(B, T, h_c, d_c), r3(lf)))
            pending = ((o, wts["odd_out"][i]),)
        last = l == depth - 1
        xs = ffn(xs, g[2:3], (l, 1), pending, wts["final_g"] if last else None)
        pending = ()
    return xs.reshape(B, T, D), new_states


def kernel(x_prompt, x_sample, state_gla, cache_band_k, cache_band_v, cache_fox_k, cache_fox_v, cache_fox_logf,
           norm_g, ffn_w_gu, ffn_w_down, even_w_in, gla_w_alpha_up, gla_b_alpha, gla_norm_g, band_rel_bias,
           even_w_out, odd_w_in, fox_b_f, odd_w_out, final_norm_g):
    depth = norm_g.shape[0]
    _, _, h_a, dk_a, dv_a = state_gla.shape
    h_b, d_b = cache_band_k.shape[-2:]
    h_c, d_c = cache_fox_k.shape[-2:]
    r_a = gla_w_alpha_up.shape[1]
    a_qk, a_v, b_w, c_w = h_a * dk_a, h_a * dv_a, h_b * d_b, h_c * d_c
    assert cache_band_k.shape[2] == BAND_CHUNKS * CHUNK and band_rel_bias.shape[-1] == 2 * MAX_REL + 1

    r0 = 2 * a_qk + 2 * a_v
    n_rel = -(-band_rel_bias.shape[-1] // LANES) * LANES
    rel_padded = jnp.pad(band_rel_bias, ((0, 0), (0, 0), (0, n_rel - band_rel_bias.shape[-1])))
    win = (BAND_CHUNKS + 1) * CHUNK
    odd_t = odd_w_in.transpose(0, 2, 1)
    even_t = even_w_in.transpose(0, 2, 1)
    wts = {
        "norm_g": norm_g,
        "w_gu": ffn_w_gu,
        "w_down": ffn_w_down,
        "ffn_bf16": {(0, 0): [ffn_w_gu[0, 0].astype(BF16), ffn_w_down[0, 0].astype(BF16)]},
        "even": {
            "main": jnp.concatenate([even_w_in[:, :, :r0], even_w_in[:, :, r0 + r_a:]], axis=-1).astype(BF16),
            "r": even_w_in[:, :, r0:r0 + r_a].astype(BF16),
            "up": gla_w_alpha_up.astype(BF16),
            "b_alpha": gla_b_alpha[:, None, :],
        },
        "even_kvt": even_t[:, r0 + r_a + b_w:].astype(BF16),
        "gla_g": gla_norm_g[:, None, :],
        "band_bias": [_bias_table(rel_padded[i], chunk=CHUNK, win=win, rows=BAND_BLOCK_CHUNKS * CHUNK)
                      for i in range(rel_padded.shape[0])],
        "even_out_a": even_w_out[:, :a_v].astype(BF16),
        "even_out_b": even_w_out[:, a_v:].astype(BF16),
        "odd_main": odd_w_in[:, :, :3 * c_w].astype(BF16),
        "odd_f": odd_w_in[:, :, 3 * c_w:].astype(BF16),
        "odd_q": odd_w_in[:, :, :c_w].astype(BF16),
        "odd_kvt": odd_t[:, c_w:3 * c_w].astype(BF16),
        "odd_ft": odd_t[:, 3 * c_w:].astype(BF16),
        "b_f": fox_b_f[:, None, :],
        "b_f_col": fox_b_f[:, :, None],
        "odd_out": odd_w_out.astype(BF16),
        "final_g": final_norm_g[None, :],
    }
    dims = (h_a, dk_a, dv_a, h_b, d_b, h_c, d_c)

    sample_states = []
    wts["hosted_src"], wts["hosted_out"] = [], []
    for l in range(depth):
        i = l // 2
        if l % 2 == 0:
            sample_states.append((state_gla[i], cache_band_k[i], cache_band_v[i]))
        else:
            sample_states.append((cache_fox_k[i], cache_fox_v[i], cache_fox_logf[i]))
            wts["hosted_src"] += [c[i].transpose(0, 2, 3, 1).reshape(-1, c.shape[2]) for c in (cache_fox_k, cache_fox_v)]
    assert len(wts["hosted_src"]) == 2, "the sample's forgetting attention expects one cached odd layer"

    y_prompt, ns_p = _trunk(x_prompt, None, wts, dims, tm=PROMPT_ROWS)
    y_sample, ns_s = _trunk(x_sample, sample_states, wts, dims, tm=x_sample.shape[0] * x_sample.shape[1])

    ev = range(0, depth, 2)
    od = range(1, depth, 2)
    stack = lambda ns, layers, j: jnp.stack([ns[l][j] for l in layers])
    return (y_prompt, y_sample,
            stack(ns_p, ev, 0), stack(ns_p, ev, 1), stack(ns_p, ev, 2),
            stack(ns_p, od, 0), stack(ns_p, od, 1), stack(ns_p, od, 2),
            stack(ns_s, ev, 0), stack(ns_s, ev, 1), stack(ns_s, ev, 2),
            stack(ns_s, od, 0), stack(ns_s, od, 1), stack(ns_s, od, 2))
```

```python
import functools

import jax
import jax.numpy as jnp
from jax import lax
from jax.experimental import pallas as pl
from jax.experimental.pallas import tpu as pltpu

F32 = jnp.float32
BF16 = jnp.bfloat16
NT = (((1,), (1,)), ((), ()))

EPS = 1e-6
CHUNK = 64
BAND_CHUNKS = 8
MAX_REL = 128
GATE_NORM_A = 16.0
LOG2E = 1.4426950408889634

V7X_VMEM_BYTES = 64 * 1024 * 1024
VMEM_LIMIT_BYTES = V7X_VMEM_BYTES * 7 // 8
LANES = 128

PROMPT_ROWS = 512
IN_PROJ_ROWS = 1024
GLA_BATCH = 8
GLA_BLOCKS = 2
BAND_BLOCK_CHUNKS = 2
BAND_BLOCKS = 4
FOX_BLOCK = 256
FOX_PAIRS = 2
FOX_SAMPLE_HEADS = 8
CUMSUM_BLOCK = 256


def _params(*sem):
    return pltpu.CompilerParams(dimension_semantics=sem, vmem_limit_bytes=VMEM_LIMIT_BYTES)


def _resident(shape):
    nd = len(shape)
    return pl.BlockSpec(shape, lambda *_: (0,) * nd, pipeline_mode=pl.Buffered(1))


def _rows(tm, width):
    return pl.BlockSpec((tm, width), lambda i: (i, 0))


def _rms(x, g):
    ms = jnp.mean(x * x, axis=-1, keepdims=True)
    return x * lax.rsqrt(ms + EPS) * g


def _log_sigmoid(x):
    return jnp.minimum(x, 0.0) - jnp.log1p(jnp.exp(-jnp.abs(x)))


def _dot(a, b):
    return jnp.dot(a, b, preferred_element_type=F32)


def _dot_nt(a, b):
    return lax.dot_general(a, b, NT, preferred_element_type=F32)


def _select_sum(x, sel, side, dot=_dot):
    sel = sel.astype(BF16)
    hi = x.astype(BF16)
    rest = x - hi.astype(F32)
    mid = rest.astype(BF16)
    lo = (rest - mid.astype(F32)).astype(BF16)
    terms = [dot(sel, p) if side == "l" else dot(p, sel) for p in (hi, mid, lo)]
    return terms[0] + terms[1] + terms[2]


def _ffn_kernel(*refs, n_proj, has_final, casts, d_ff):
    it = iter(refs)
    x_ref = next(it)
    proj = [(next(it), next(it)) for _ in range(n_proj)]
    g_ref, wgu_ref, wd_ref = next(it), next(it), next(it)
    gf_ref = next(it) if has_final else None
    cast_src = [next(it) for _ in range(casts)]
    out_ref = next(it)
    for src in cast_src:
        next(it)[...] = src[...].astype(BF16)

    x = x_ref[...]
    for o_ref, w_ref in proj:
        x = x + _dot(o_ref[...], w_ref[...])
    xn = _rms(x, g_ref[...]).astype(BF16)
    h = _dot(xn, wgu_ref[...])
    a = (jax.nn.silu(h[:, :d_ff]) * h[:, d_ff:]).astype(BF16)
    y = x + 0.5 * _dot(a, wd_ref[...])
    if has_final:
        y = _rms(y, gf_ref[...])
    out_ref[...] = y


def _slab_cast(src, lead, rows, steps, per=1, first=0):
    slab = rows * per // steps
    width = src.shape[-1]
    assert rows * per % steps == 0 and slab % 16 == 0
    return (src,
            pl.BlockSpec((None,) * len(lead) + (slab, width), lambda i: tuple(lead) + (first + i // per, 0)),
            pl.BlockSpec((slab, width), lambda i: (i // per, 0)),
            jax.ShapeDtypeStruct((rows, width), BF16))


def _ffn(x, g, wgu, wd, proj=(), final_g=None, casts=(), *, tm):
    n, d = x.shape
    d_ff = wd.shape[0]
    steps = n // tm
    args, specs = [x], [_rows(tm, d)]
    for o, w in proj:
        args += [o, w]
        specs += [_rows(tm, o.shape[1]), _resident(w.shape)]
    args += [g, wgu, wd]
    specs += [_resident(g.shape), _resident(wgu.shape), _resident(wd.shape)]
    if final_g is not None:
        args.append(final_g)
        specs.append(_resident(final_g.shape))
    out_specs, out_shapes = [_rows(tm, d)], [jax.ShapeDtypeStruct((n, d), F32)]
    for src, in_spec, out_spec, out_shape in casts:
        args.append(src)
        specs.append(in_spec)
        out_specs.append(out_spec)
        out_shapes.append(out_shape)
    return pl.pallas_call(
        functools.partial(_ffn_kernel, n_proj=len(proj), has_final=final_g is not None, casts=len(casts), d_ff=d_ff),
        grid=(steps,),
        in_specs=specs,
        out_specs=out_specs,
        out_shape=out_shapes,
        compiler_params=_params("arbitrary" if casts else "parallel"),
        name="ffn",
    )(*args)


def _even_in_kernel(x_ref, g_ref, w_ref, wr_ref, wup_ref, ba_ref,
                    qa_ref, kat_ref, va_ref, gate_ref, la_ref, lat_ref, bq_ref, bk_ref, bv_ref, *kv32_refs,
                    a_qk, a_v, b_w, dk_a, d_b):
    xn = _rms(x_ref[...], g_ref[...]).astype(BF16)
    h = _dot(xn, w_ref[...])
    o = 0
    qa_ref[...] = h[:, o:o + a_qk] * (dk_a ** -0.5)
    o += a_qk
    kat_ref[...] = h[:, o:o + a_qk].T
    o += a_qk
    va_ref[...] = h[:, o:o + a_v].astype(BF16)
    o += a_v
    gate_ref[...] = jax.nn.silu(h[:, o:o + a_v])
    o += a_v
    bq_ref[...] = (h[:, o:o + b_w] * (LOG2E * d_b ** -0.5)).astype(BF16)
    o += b_w
    bk = h[:, o:o + b_w]
    o += b_w
    bv = h[:, o:o + b_w]
    bk_ref[...] = bk.astype(BF16)
    bv_ref[...] = bv.astype(BF16)
    if kv32_refs:
        kv32_refs[0][...] = bk
        kv32_refs[1][...] = bv
    r = _dot(xn, wr_ref[...]).astype(BF16)
    la = _log_sigmoid(_dot(r, wup_ref[...]) + ba_ref[...]) / GATE_NORM_A
    la_ref[...] = la
    lat_ref[...] = la.T


def _even_in(x, g, w, *, tm, t_split, a_qk, a_v, b_w, dk_a, d_b, kv32):
    n, d = x.shape
    bo, to = t_split
    nb = to // tm
    assert bo * to == n and to % tm == 0
    tmin = (pl.BlockSpec((None, a_qk, tm), lambda i: (i // nb, 0, i % nb)), jax.ShapeDtypeStruct((bo, a_qk, to), F32))
    tok = lambda width, dt: (_rows(tm, width), jax.ShapeDtypeStruct((n, width), dt))
    outs = [tok(a_qk, F32), tmin, tok(a_v, BF16), tok(a_v, F32), tok(a_qk, F32), tmin,
            tok(b_w, BF16), tok(b_w, BF16), tok(b_w, BF16)]
    if kv32:
        outs += [tok(b_w, F32), tok(b_w, F32)]
    args = [x, g] + [w[k] for k in ("main", "r", "up", "b_alpha")]
    return pl.pallas_call(
        functools.partial(_even_in_kernel, a_qk=a_qk, a_v=a_v, b_w=b_w, dk_a=dk_a, d_b=d_b),
        grid=(n // tm,),
        in_specs=[_rows(tm, d)] + [_resident(a.shape) for a in args[1:]],
        out_specs=[s for s, _ in outs],
        out_shape=[t for _, t in outs],
        compiler_params=_params("parallel"),
        name="even_in",
    )(*args)


def _band_tail_kernel(x_ref, g_ref, wkvt_ref, kt_ref, vt_ref):
    xn = _rms(x_ref[...], g_ref[...]).astype(BF16)
    kvt = _dot_nt(wkvt_ref[...], xn)
    b_w = kt_ref.shape[1]
    kt_ref[0] = kvt[:b_w]
    vt_ref[0] = kvt[b_w:]


def _band_tail(x, g, w_kvt, *, B, T, keep):
    n, d = x.shape
    b_w = w_kvt.shape[0] // 2
    assert T % keep == 0
    nb = T // keep
    out = pl.BlockSpec((1, b_w, keep), lambda b: (b, 0, 0))
    return pl.pallas_call(
        _band_tail_kernel,
        grid=(B,),
        in_specs=[pl.BlockSpec((keep, d), lambda b: (b * nb + nb - 1, 0)), _resident(g.shape), _resident(w_kvt.shape)],
        out_specs=[out, out],
        out_shape=[jax.ShapeDtypeStruct((B, b_w, keep), F32)] * 2,
        compiler_params=_params("parallel"),
        name="band_tail",
    )(x, g, w_kvt)


def _odd_in_kernel(x_ref, g_ref, w_ref, wf_ref, bf_ref, q_ref, k_ref, v_ref, k32_ref, v32_ref, lf_ref, *, c_w, d_c):
    xn = _rms(x_ref[...], g_ref[...]).astype(BF16)
    h = _dot(xn, w_ref[...])
    q_ref[...] = (h[:, :c_w] * (d_c ** -0.5)).astype(BF16)
    k = h[:, c_w:2 * c_w]
    v = h[:, 2 * c_w:]
    k32_ref[...] = k
    v32_ref[...] = v
    k_ref[...] = k.astype(BF16)
    v_ref[...] = v.astype(BF16)
    lf_ref[...] = _log_sigmoid(_dot(xn, wf_ref[...]) + bf_ref[...])


def _odd_in(x, g, w_main, w_f, b_f, *, tm, c_w, d_c, h_c):
    n, d = x.shape
    outs = [(c_w, BF16), (c_w, BF16), (c_w, BF16), (c_w, F32), (c_w, F32), (h_c, F32)]
    return pl.pallas_call(
        functools.partial(_odd_in_kernel, c_w=c_w, d_c=d_c),
        grid=(n // tm,),
        in_specs=[_rows(tm, d), _resident(g.shape), _resident(w_main.shape), _resident(w_f.shape),
                  _resident(b_f.shape)],
        out_specs=[_rows(tm, w) for w, _ in outs],
        out_shape=[jax.ShapeDtypeStruct((n, w), dt) for w, dt in outs],
        compiler_params=_params("parallel"),
        name="odd_in",
    )(x, g, w_main, w_f, b_f)


def _odd_in_t_kernel(x_ref, g_ref, wq_ref, wkvt_ref, wft_ref, bf_ref, q_ref, kt_ref, vt_ref, lft_ref, *, nh, d_c):
    xn = _rms(x_ref[...], g_ref[...]).astype(BF16)
    tm = xn.shape[0]
    c_w = nh * d_c
    q_ref[...] = (_dot(xn, wq_ref[...]) * (LOG2E * d_c ** -0.5)).astype(BF16)
    ht = _dot_nt(wkvt_ref[...], xn)
    kt_ref[0] = ht[:c_w].reshape(nh, d_c, tm)
    vt_ref[0] = ht[c_w:].reshape(nh, d_c, tm)
    lft_ref[0] = _log_sigmoid(_dot_nt(wft_ref[...], xn) + bf_ref[...])


def _odd_in_t(x, g, wq, wkvt, wft, b_f_col, *, B, T, tm, nh, d_c):
    n, d = x.shape
    nb = T // tm
    c_w = nh * d_c
    tok = lambda w: pl.BlockSpec((tm, w), lambda b, i: (b * nb + i, 0))
    kv = pl.BlockSpec((1, nh, d_c, tm), lambda b, i: (b, 0, 0, i))
    return pl.pallas_call(
        functools.partial(_odd_in_t_kernel, nh=nh, d_c=d_c),
        grid=(B, nb),
        in_specs=[tok(d), _resident(g.shape), _resident(wq.shape), _resident(wkvt.shape), _resident(wft.shape),
                  _resident(b_f_col.shape)],
        out_specs=[tok(c_w), kv, kv, pl.BlockSpec((1, nh, tm), lambda b, i: (b, 0, i))],
        out_shape=[jax.ShapeDtypeStruct((n, c_w), BF16), jax.ShapeDtypeStruct((B, nh, d_c, T), F32),
                   jax.ShapeDtypeStruct((B, nh, d_c, T), F32), jax.ShapeDtypeStruct((B, nh, T), F32)],
        compiler_params=_params("parallel", "parallel"),
        name="odd_in_t",
    )(x, g, wq, wkvt, wft, b_f_col)


def _gla_kernel(q_ref, kt_ref, v_ref, gate_ref, la_ref, lat_ref, s0_ref, g_ref, o_ref, sfin_ref, s_scr,
                *, bb, L, nh, dk, dv):
    c = pl.program_id(1)

    @pl.when(c == 0)
    def _():
        s_scr[...] = s0_ref[...]

    tw = q_ref.shape[1]
    nsub = tw // L
    w = nh * dk
    shift = L.bit_length() - 1
    assert L == 1 << shift and nh % 2 == 0
    tri = (lax.broadcasted_iota(jnp.int32, (L, L), 0) >= lax.broadcasted_iota(jnp.int32, (L, L), 1)).astype(F32)
    t0 = lax.broadcasted_iota(jnp.int32, (tw, tw), 0)
    t1 = lax.broadcasted_iota(jnp.int32, (tw, tw), 1)
    tri_t = (((t0 >> shift) == (t1 >> shift)) & (t0 <= t1)).astype(F32)
    lane_t = lax.broadcasted_iota(jnp.int32, (1, tw), 1)
    row_i = lax.broadcasted_iota(jnp.int32, (L, tw), 0)
    _, mine = _pair_masks(dk)
    g = g_ref[...]
    bs = range(bb)
    zero = jnp.zeros((), BF16)

    bt_all = _select_sum(jnp.concatenate([lat_ref[b] for b in bs], axis=0), tri_t, "r")
    k_in, k_st, a_col = [], [], []
    for b in bs:
        bt = bt_all[b * w:(b + 1) * w]
        kt = kt_ref[b]
        last = [bt[:, (cc + 1) * L - 1:(cc + 1) * L] for cc in range(nsub)]
        last_sel = last[0]
        for cc in range(1, nsub):
            last_sel = jnp.where(lane_t >= cc * L, last[cc], last_sel)
        k_in.append((kt * jnp.exp(-bt)).astype(BF16))
        k_st.append((kt * jnp.exp(last_sel - bt)).astype(BF16))
        a_col.append([jnp.exp(x) for x in last])
    q_in = {}
    for cc in range(nsub):
        rows = slice(cc * L, (cc + 1) * L)
        bc = _select_sum(jnp.concatenate([la_ref[b, rows] for b in bs], axis=1), tri, "l")
        for b in bs:
            q_in[b, cc] = (q_ref[b, rows] * jnp.exp(bc[:, b * w:(b + 1) * w])).astype(BF16)

    ccs = range(nsub)
    pairs = [(b, p) for b in bs for p in range(nh // 2)]
    heads = [(b, p, hh) for b, p in pairs for hh in range(2)]
    pair = lambda p: slice(p * 2 * dk, (p + 1) * 2 * dk)
    in_blk = [(lane_t >= cc * L) & (lane_t < (cc + 1) * L) for cc in ccs]
    causal = [in_blk[cc] & (lane_t - cc * L <= row_i) for cc in ccs]
    qm = {(b, p, hh, cc): jnp.where(mine[hh], q_in[b, cc][:, pair(p)], zero) for b, p, hh in heads for cc in ccs}
    sc = {(b, p): _dot(jnp.concatenate([qm[b, p, hh, cc] for cc in ccs for hh in range(2)], axis=0), k_in[b][pair(p)])
          for b, p in pairs}
    att = {(b, p, hh, cc): jnp.where(causal[cc], sc[b, p][(2 * cc + hh) * L:(2 * cc + hh + 1) * L], 0.0).astype(BF16)
           for b, p, hh in heads for cc in ccs}
    inc, o_in = {}, {}
    for b, p, hh in heads:
        h = 2 * p + hh
        ks = [jnp.where(in_blk[cc], k_st[b][h * dk:(h + 1) * dk], zero) for cc in ccs]
        r = _dot(jnp.concatenate([att[b, p, hh, cc] for cc in ccs] + ks, axis=0), v_ref[b, :, h * dv:(h + 1) * dv])
        for cc in ccs:
            o_in[b, h, cc] = r[cc * L:(cc + 1) * L]
            inc[b, h, cc] = r[nsub * L + cc * dk:nsub * L + (cc + 1) * dk]
    state = {(b, h): s_scr[b, h] for b in bs for h in range(nh)}
    for cc in ccs:
        rows = slice(cc * L, (cc + 1) * L)
        o = {}
        for b, p in pairs:
            s_pair = jnp.concatenate([state[b, 2 * p], state[b, 2 * p + 1]], axis=0).astype(BF16)
            carried = _dot(jnp.concatenate([qm[b, p, 0, cc], qm[b, p, 1, cc]], axis=0), s_pair)
            for hh in range(2):
                h = 2 * p + hh
                o[b, h] = o_in[b, h, cc] + carried[hh * L:(hh + 1) * L]
                state[b, h] = a_col[b][cc][h * dk:(h + 1) * dk] * state[b, h] + inc[b, h, cc]
        o = {k: x * lax.rsqrt(jnp.mean(x * x, axis=-1, keepdims=True) + EPS) * g for k, x in o.items()}
        for b in bs:
            o_ref[b, rows] = (jnp.concatenate([o[b, h] for h in range(nh)], axis=-1) * gate_ref[b, rows]).astype(BF16)
    for (b, h), x in state.items():
        s_scr[b, h] = x

    @pl.when(c == pl.num_programs(1) - 1)
    def _():
        sfin_ref[...] = s_scr[...]


def _gla(q, kt, v, gate, la, lat, s0, g, *, bb, L, nsub):
    B, T, a_qk = q.shape
    a_v = v.shape[-1]
    _, nh, dk, dv = s0.shape
    tw = L * nsub
    assert T % tw == 0

    def tok(w):
        return pl.BlockSpec((bb, tw, w), lambda i, c: (i, c, 0))

    tmin = pl.BlockSpec((bb, a_qk, tw), lambda i, c: (i, 0, c))
    st = pl.BlockSpec((bb, nh, dk, dv), lambda i, c: (i, 0, 0, 0))
    return pl.pallas_call(
        functools.partial(_gla_kernel, bb=bb, L=L, nh=nh, dk=dk, dv=dv),
        grid=(B // bb, T // tw),
        in_specs=[tok(a_qk), tmin, tok(a_v), tok(a_v), tok(a_qk), tmin, st,
                  pl.BlockSpec(g.shape, lambda i, c: (0, 0))],
        out_specs=[tok(a_v), st],
        out_shape=[jax.ShapeDtypeStruct((B, T, a_v), BF16), jax.ShapeDtypeStruct(s0.shape, F32)],
        scratch_shapes=[pltpu.VMEM((bb, nh, dk, dv), F32)],
        compiler_params=_params("parallel", "arbitrary"),
        name="gla",
    )(q, kt, v, gate, la, lat, s0, g)


def _bias_table_kernel(rel_ref, out_ref, *, chunk, win, n_rel):
    n_var, nh, rows, width = out_ref.shape
    back = win - chunk
    wide = width + back + rows
    assert wide % LANES == 0 and chunk & (chunk - 1) == 0
    u = lax.broadcasted_iota(jnp.int32, (n_rel, wide), 1)
    want = jnp.clip(back + rows - 1 - u, -MAX_REL, MAX_REL) + MAX_REL
    onehot = (lax.broadcasted_iota(jnp.int32, (n_rel, wide), 0) == want).astype(F32)
    vec = _select_sum(rel_ref[...], onehot, "r")
    i = lax.broadcasted_iota(jnp.int32, (rows, width), 0)
    j = lax.broadcasted_iota(jnp.int32, (rows, width), 1)
    first = i - (i & (chunk - 1))
    for t in range(n_var):
        shift = back - t * rows if t < n_var - 1 else 0
        ok = (j + shift >= first) & (j + shift < first + win)
        for h in range(nh):
            x = jnp.broadcast_to(vec[h:h + 1, :], (rows, wide))
            x = pltpu.roll(x, (wide - (shift + rows - 1)) % wide, 1, stride=1, stride_axis=0)
            out_ref[t, h] = jnp.where(ok, x[:, :width] * LOG2E, -jnp.inf)


def _bias_table(rel_padded, *, chunk, win, rows):
    nh, n_rel = rel_padded.shape
    n_var = (win - chunk) // rows + 1
    return pl.pallas_call(
        functools.partial(_bias_table_kernel, chunk=chunk, win=win, n_rel=n_rel),
        out_shape=jax.ShapeDtypeStruct((n_var, nh, rows, win + rows - chunk), F32),
        compiler_params=_params(),
        name="band_bias_table",
    )(rel_padded)


def _pair_masks(d):
    lane = lax.broadcasted_iota(jnp.int32, (1, 2 * d), 1)
    return lane, [lane < d, lane >= d]


def _band_prompt_kernel(q_ref, k_ref, v_ref, bias_ref, o_ref, vx_scr, *, nh, d, back):
    n_var, _, tq, width = bias_ref.shape
    nsub = q_ref.shape[1] // tq
    lane, mine = _pair_masks(d)
    ones_lane = [d, 0]
    cols = [slice(p * 2 * d, (p + 1) * 2 * d) for p in range(nh // 2)]
    zero = jnp.zeros((), BF16)

    @pl.when(pl.program_id(1) == 0)
    def _():
        in_pair = lax.broadcasted_iota(jnp.int32, (1, nh * d), 1) & (2 * d - 1)
        v = v_ref[0]
        vx_scr[0] = jnp.where(in_pair < d, v, (in_pair == ones_lane[0]).astype(BF16))
        vx_scr[1] = jnp.where(in_pair >= d, v, (in_pair == ones_lane[1]).astype(BF16))

    q, kw, vw, var = [], [], [], []
    for sb in range(nsub):
        c = pl.program_id(1) * nsub + sb
        start = pl.multiple_of(jnp.maximum(c * tq - back, 0), tq)
        var.append(jnp.minimum(c, n_var - 1))
        q.append(q_ref[0, sb * tq:(sb + 1) * tq, :])
        kw.append(k_ref[0, pl.ds(start, width), :])
        vw.append([vx_scr[hh, pl.ds(start, width), :] for hh in range(2)])
    units = [(sb, p, hh) for sb in range(nsub) for p in range(nh // 2) for hh in range(2)]
    s = [_dot_nt(jnp.where(mine[hh], q[sb][:, cols[p]], zero), kw[sb][:, cols[p]]) + bias_ref[var[sb], 2 * p + hh]
         for sb, p, hh in units]
    m = [jnp.max(x, axis=-1, keepdims=True) for x in s]
    e = [jnp.exp2(x - mx).astype(BF16) for x, mx in zip(s, m)]
    r = [_dot(e[u], vw[sb][hh][:, cols[p]]) for u, (sb, p, hh) in enumerate(units)]
    r = [x / x[:, ones_lane[hh]:ones_lane[hh] + 1] for x, (sb, p, hh) in zip(r, units)]
    for sb in range(nsub):
        rs = r[sb * nh:(sb + 1) * nh]
        o_ref[0, sb * tq:(sb + 1) * tq, :] = jnp.concatenate(
            [jnp.where(mine[0], rs[2 * p], rs[2 * p + 1]) for p in range(nh // 2)], axis=-1).astype(BF16)


def _band_prompt(q, k, v, bias, *, nh, d, nsub):
    B, T, w = q.shape
    n_var, _, tq, width = bias.shape
    tq *= nsub
    assert T >= width and T % tq == 0
    kv = pl.BlockSpec((1, T, w), lambda b, c: (b, 0, 0))
    return pl.pallas_call(
        functools.partial(_band_prompt_kernel, nh=nh, d=d, back=width - tq // nsub),
        grid=(B, T // tq),
        in_specs=[pl.BlockSpec((1, tq, w), lambda b, c: (b, c, 0)), kv, kv, _resident(bias.shape)],
        out_specs=pl.BlockSpec((1, tq, w), lambda b, c: (b, c, 0)),
        out_shape=jax.ShapeDtypeStruct((B, T, w), BF16),
        scratch_shapes=[pltpu.VMEM((2, T, w), BF16)],
        compiler_params=_params("parallel", "arbitrary"),
        name="band_prompt",
    )(q, k, v, bias)


def _band_sample_kernel(q_ref, ckt_ref, cvt_ref, kn_ref, vn_ref, bias_ref, o_ref, *, nh, d, s_new):
    lc = ckt_ref.shape[3]
    q = q_ref[0]
    kn = kn_ref[0]
    vn = vn_ref[0]
    hs = range(nh)
    sl = [slice(h * d, (h + 1) * d) for h in hs]
    sc = [_dot(q[:, sl[h]], ckt_ref[0, h].astype(BF16)) + bias_ref[0, h][:s_new, :lc] for h in hs]
    sn = [_dot_nt(q[:, sl[h]], kn[:, sl[h]]) + bias_ref[0, h][:s_new, lc:lc + s_new] for h in hs]
    m = [jnp.maximum(jnp.max(sc[h], axis=-1, keepdims=True), jnp.max(sn[h], axis=-1, keepdims=True)) for h in hs]
    ec = [jnp.exp2(sc[h] - m[h]) for h in hs]
    en = [jnp.exp2(sn[h] - m[h]) for h in hs]
    l = [jnp.sum(ec[h], axis=-1, keepdims=True) + jnp.sum(en[h], axis=-1, keepdims=True) for h in hs]
    o = [(_dot_nt(ec[h].astype(BF16), cvt_ref[0, h].astype(BF16)) + _dot(en[h].astype(BF16), vn[:, sl[h]])) / l[h]
         for h in hs]
    o_ref[0] = jnp.concatenate(o, axis=-1).astype(BF16)


def _band_sample(q, ckt, cvt, kn, vn, bias, *, nh, d):
    B, s_new, w = q.shape
    lc = ckt.shape[3]
    n_var, _, rows, width = bias.shape
    assert s_new <= CHUNK and lc == width - rows
    new = pl.BlockSpec((1, s_new, w), lambda b: (b, 0, 0))
    old = pl.BlockSpec((1, nh, d, lc), lambda b: (b, 0, 0, 0))
    return pl.pallas_call(
        functools.partial(_band_sample_kernel, nh=nh, d=d, s_new=s_new),
        grid=(B,),
        in_specs=[new, old, old, new, new, pl.BlockSpec((1, nh, rows, width), lambda b: (n_var - 1, 0, 0, 0))],
        out_specs=new,
        out_shape=jax.ShapeDtypeStruct((B, s_new, w), BF16),
        compiler_params=_params("parallel"),
        name="band_sample",
    )(q, ckt, cvt, kn, vn, bias)


def _cumsum_lanes_kernel(x_ref, out_ref, *, blk, suffix):
    R, T = x_ref.shape
    src = lax.broadcasted_iota(jnp.int32, (blk, blk), 0)
    dst = lax.broadcasted_iota(jnp.int32, (blk, blk), 1)
    tri = ((src > dst) if suffix else (src <= dst)).astype(F32)
    carry = jnp.zeros((R, 1), F32)
    n = T // blk
    for i in (range(n - 1, -1, -1) if suffix else range(n)):
        xb = x_ref[:, i * blk:(i + 1) * blk]
        out_ref[:, i * blk:(i + 1) * blk] = _select_sum(xb, tri, "r") + carry
        carry = carry + jnp.sum(xb, axis=1, keepdims=True)


def _cumsum_lanes(x, *, suffix):
    return pl.pallas_call(
        functools.partial(_cumsum_lanes_kernel, blk=CUMSUM_BLOCK, suffix=suffix),
        out_shape=jax.ShapeDtypeStruct(x.shape, F32),
        compiler_params=_params(),
        name="cumsum_lanes",
    )(x)


def _fox_prompt_kernel(q_ref, kt_ref, vt_ref, ft_ref, o_ref, *, tq, tk, d):
    p = pl.program_id(1)
    T = q_ref.shape[1]
    assert tq == tk and d % 8 == 0
    nq = T // tq
    lane, mine = _pair_masks(d)
    q_fill = [((lane >= d) & (lane < d + 3)).astype(BF16), (lane < 3).astype(BF16)]
    den = [d, 0]
    sub = lax.broadcasted_iota(jnp.int32, (8, tk), 0)
    ones_blk = (sub == 0).astype(F32)
    pad_blk = jnp.zeros((d - 8, tk), F32)

    def stacked(x, blk, hh):
        return jnp.concatenate([x, blk, pad_blk] if hh == 0 else [blk, pad_blk, x], axis=0)

    diag_mask = (lax.broadcasted_iota(jnp.int32, (tk, tq), 0) <= lax.broadcasted_iota(jnp.int32, (tk, tq), 1))
    npair = q_ref.shape[2] // (2 * d)
    heads = [(pr, hh) for pr in range(npair) for hh in range(2)]
    units = [(qi, pr, hh) for qi in range(nq) for pr, hh in heads]
    def q_operand(u):
        qi, pr, hh = u
        return jnp.where(mine[hh], q_ref[0, qi * tq:(qi + 1) * tq, pr * 2 * d:(pr + 1) * 2 * d], q_fill[hh])

    m = {u: jnp.full((1, tq), -jnp.inf, F32) for u in units}
    acc = {u: jnp.zeros((2 * d, tq), F32) for u in units}
    for kv in range(nq):
        ks = slice(kv * tk, (kv + 1) * tk)
        k_rows, v_all = {}, {}
        for pr, hh in heads:
            f = ft_ref[0, pl.ds(2 * (p * npair + pr) + hh, 1), ks] * LOG2E
            hi = f.astype(BF16).astype(F32)
            mid = (f - hi).astype(BF16).astype(F32)
            f_blk = jnp.where(sub == 0, -hi, jnp.where(sub == 1, -mid, jnp.where(sub == 2, -(f - hi - mid), 0.0)))
            k_rows[pr, hh] = stacked(kt_ref[0, 2 * pr + hh, :, ks], f_blk, hh).T.astype(BF16)
            v_all[pr, hh] = stacked(vt_ref[0, 2 * pr + hh, :, ks], ones_blk, hh).astype(BF16)
        live = [u for u in units if u[0] >= kv]
        s = {u: _dot_nt(k_rows[u[1:]], q_operand(u)) for u in live}
        for u in live:
            if u[0] == kv:
                s[u] = jnp.where(diag_mask, s[u], -jnp.inf)
        m_new = {u: jnp.maximum(m[u], jnp.max(s[u], axis=0, keepdims=True)) for u in live}
        e = {u: jnp.exp2(s[u] - m_new[u]).astype(BF16) for u in live}
        for u in live:
            acc[u] = jnp.exp2(m[u] - m_new[u]) * acc[u] + _dot(v_all[u[1:]], e[u])
            m[u] = m_new[u]
        for pr in range(npair):
            out = [(acc[kv, pr, hh] / acc[kv, pr, hh][den[hh]:den[hh] + 1, :]).T for hh in range(2)]
            o_ref[0, kv * tq:(kv + 1) * tq, pr * 2 * d:(pr + 1) * 2 * d] = jnp.where(mine[0], out[0],
                                                                                   out[1]).astype(BF16)


def _fox_prompt(q, kt, vt, ft, *, tq, tk, npair):
    B, T, w = q.shape
    _, nh, d, _ = kt.shape
    kv = pl.BlockSpec((1, 2 * npair, d, T), lambda b, p: (b, p, 0, 0))
    qo = pl.BlockSpec((1, T, 2 * npair * d), lambda b, p: (b, 0, p))
    return pl.pallas_call(
        functools.partial(_fox_prompt_kernel, tq=tq, tk=tk, d=d),
        grid=(B, nh // (2 * npair)),
        in_specs=[qo, kv, kv, pl.BlockSpec((1, nh, T), lambda b, p: (b, 0, 0))],
        out_specs=qo,
        out_shape=jax.ShapeDtypeStruct((B, T, w), BF16),
        compiler_params=_params("parallel", "arbitrary"),
        name="fox_prompt",
    )(q, kt, vt, ft)


def _fox_sample_kernel(q_ref, kt_ref, vt_ref, ct_ref, kn_ref, vn_ref, lf_ref, o_ref, fqt_scr, *, hg, d, s_new):
    g = pl.program_id(1)
    nh = lf_ref.shape[2]
    tri = (lax.broadcasted_iota(jnp.int32, (s_new, s_new), 0)
           >= lax.broadcasted_iota(jnp.int32, (s_new, s_new), 1))
    eye = (lax.broadcasted_iota(jnp.int32, (nh, nh), 0) == lax.broadcasted_iota(jnp.int32, (nh, nh), 1)).astype(F32)
    fq = _select_sum(lf_ref[0], tri, "l")
    fqt_scr[...] = _select_sum(fq, eye, "l", _dot_nt)
    hs = range(hg)
    sl = [slice(hh * d, (hh + 1) * d) for hh in hs]
    qh = [q_ref[0, 0, :, sl[hh]] for hh in hs]
    sc = [_dot(qh[hh], kt_ref[0, hh].astype(BF16)) + ct_ref[0, pl.ds(g * hg + hh, 1), :] for hh in hs]
    sn = [jnp.where(tri, _dot_nt(qh[hh], kn_ref[0, 0, :, sl[hh]]) - fqt_scr[pl.ds(g * hg + hh, 1), :], -jnp.inf)
          for hh in hs]
    m = [jnp.maximum(jnp.max(sc[hh], axis=-1, keepdims=True), jnp.max(sn[hh], axis=-1, keepdims=True)) for hh in hs]
    ec = [jnp.exp(sc[hh] - m[hh]) for hh in hs]
    en = [jnp.exp(sn[hh] - m[hh]) for hh in hs]
    l = [jnp.sum(ec[hh], axis=-1, keepdims=True) + jnp.sum(en[hh], axis=-1, keepdims=True) for hh in hs]
    o = [(_dot_nt(ec[hh].astype(BF16), vt_ref[0, hh].astype(BF16)) + _dot(en[hh].astype(BF16), vn_ref[0, 0, :, sl[hh]]))
         / l[hh] for hh in hs]
    o_ref[0, 0] = jnp.concatenate(o, axis=-1).astype(BF16)


def _fox_sample(q, kt, vt, ct, kn, vn, lf, *, hg, b0):
    _, ng, s_new, gw = q.shape
    nb, nh, d, lc = kt.shape
    new = pl.BlockSpec((1, 1, s_new, gw), lambda b, g: (b0 + b, g, 0, 0))
    old = pl.BlockSpec((1, hg, d, lc), lambda b, g: (b, g, 0, 0))
    return pl.pallas_call(
        functools.partial(_fox_sample_kernel, hg=hg, d=d, s_new=s_new),
        grid=(nb, ng),
        in_specs=[new, old, old, pl.BlockSpec((1, nh, lc), lambda b, g: (b0 + b, 0, 0)), new, new,
                  pl.BlockSpec((1, s_new, nh), lambda b, g: (b0 + b, 0, 0))],
        out_specs=pl.BlockSpec((1, 1, s_new, gw), lambda b, g: (b, g, 0, 0)),
        out_shape=jax.ShapeDtypeStruct((nb,) + q.shape[1:], BF16),
        scratch_shapes=[pltpu.VMEM((nh, s_new), F32)],
        compiler_params=_params("parallel", "arbitrary"),
        name="fox_sample",
    )(q, kt, vt, ct, kn, vn, lf)


def _trunk(x, state, wts, dims, *, tm, tm_in):
    B, T, D = x.shape
    n = B * T
    depth = wts["norm_g"].shape[0]
    h_a, dk_a, dv_a, h_b, d_b, h_c, d_c = dims
    a_qk, a_v, b_w, c_w = h_a * dk_a, h_a * dv_a, h_b * d_b, h_c * d_c
    xs = x.reshape(n, D)
    new_states = []
    pending = ()
    r3 = lambda t: t.reshape(B, T, t.shape[-1])
    ffn_w = wts["ffn_bf16"]

    def ffn(xs, g, key, proj, final_g=None):
        if state is not None:
            return _ffn(xs, g, *ffn_w[key], proj, final_g, tm=tm)[0]
        steps = n // tm
        nxt = (key[0], 1) if key[1] == 0 else (key[0] + 1, 0)
        jobs = []
        if nxt[0] < depth:
            d_ff = ffn_w[key][1].shape[0]
            jobs += [_slab_cast(wts["w_gu"], nxt, D, steps), _slab_cast(wts["w_down"], nxt, d_ff, steps, per=2)]
        for src in wts["hosted_src"]:
            share = src.shape[0] // (2 * depth)
            jobs.append(_slab_cast(src, (), share, steps, first=(2 * key[0] + key[1]) * steps))
        xs, *outs = _ffn(xs, g, *ffn_w[key], proj, final_g, jobs, tm=tm)
        if nxt[0] < depth:
            ffn_w[nxt], outs = outs[:2], outs[2:]
        wts["hosted_out"].append(outs)
        return xs

    for l in range(depth):
        i = l // 2
        g = wts["norm_g"][l]
        xs = ffn(xs, g[0:1], (l, 0), pending)
        if l % 2 == 0:
            prompt = state is None
            keep = min(BAND_CHUNKS * CHUNK, T)
            qa, kat, va, gate, la, lat, bq, bk, bv, *kv32 = _even_in(
                xs, g[1:2], {k: v[i] for k, v in wts["even"].items()}, tm=tm_in, t_split=(B, T) if prompt else (1, n),
                a_qk=a_qk, a_v=a_v, b_w=b_w, dk_a=dk_a, d_b=d_b, kv32=not prompt)
            if not prompt:
                kat, lat = (t.reshape(a_qk, B, T).transpose(1, 0, 2) for t in (kat, lat))
            nk, nv = _band_tail(xs, g[1:2], wts["even_kvt"][i], B=B, T=T, keep=keep) if prompt else kv32
            s0 = jnp.zeros((B, h_a, dk_a, dv_a), F32) if prompt else state[l][0]
            blk = min(CHUNK, T)
            o_a, s_fin = _gla(r3(qa), kat, r3(va), r3(gate), r3(la), lat, s0, wts["gla_g"][i],
                              bb=GLA_BATCH, L=blk, nsub=GLA_BLOCKS if T % (GLA_BLOCKS * blk) == 0 else 1)
            if prompt:
                o_b = _band_prompt(r3(bq), r3(bk), r3(bv), wts["band_bias"][i], nh=h_b, d=d_b, nsub=BAND_BLOCKS)
                nk, nv = (t.reshape(B, h_b, d_b, keep).transpose(0, 3, 1, 2) for t in (nk, nv))
            else:
                ck, cv = state[l][1], state[l][2]
                o_b = _band_sample(r3(bq), ck.transpose(0, 2, 3, 1), cv.transpose(0, 2, 3, 1), r3(bk), r3(bv),
                                   wts["band_bias"][i], nh=h_b, d=d_b)
                nk, nv = (t.reshape(B, T, h_b, d_b) for t in (nk, nv))
            new_states.append((s_fin, nk, nv))
            pending = ((o_a.reshape(n, a_v), wts["even_out_a"][i]), (o_b.reshape(n, b_w), wts["even_out_b"][i]))
        elif state is None:
            q, kt, vt, lft = _odd_in_t(xs, g[1:2], wts["odd_q"][i], wts["odd_kvt"][i], wts["odd_ft"][i],
                                       wts["b_f_col"][i], B=B, T=T, tm=tm_in, nh=h_c, d_c=d_c)
            ft = _cumsum_lanes(lft.reshape(B * h_c, T), suffix=False).reshape(B, h_c, T)
            o = _fox_prompt(r3(q), kt, vt, ft, tq=FOX_BLOCK, tk=FOX_BLOCK, npair=FOX_PAIRS)
            new_states.append((kt.transpose(0, 3, 1, 2), vt.transpose(0, 3, 1, 2), lft.transpose(0, 2, 1)))
            pending = ((o.reshape(n, c_w), wts["odd_out"][i]),)
        else:
            q, k, v, k32, v32, lf = _odd_in(xs, g[1:2], wts["odd_main"][i], wts["odd_f"][i], wts["b_f"][i],
                                             tm=tm_in, c_w=c_w, d_c=d_c, h_c=h_c)
            clf = state[l][2]
            lc = clf.shape[1]
            hg = FOX_SAMPLE_HEADS
            ct = _cumsum_lanes(clf.astype(F32).transpose(0, 2, 1).reshape(B * h_c, lc), suffix=True)
            grp = lambda t: t.reshape(B, T, h_c // hg, hg * d_c).transpose(0, 2, 1, 3)
            hosted = wts["hosted_out"]
            per = B // len(hosted)
            o = jnp.concatenate([
                _fox_sample(grp(q), parts[2 * i].reshape(per, h_c, d_c, lc), parts[2 * i + 1].reshape(per, h_c, d_c, lc),
                            ct.reshape(B, h_c, lc), grp(k), grp(v), r3(lf), hg=hg, b0=j * per)
                for j, parts in enumerate(hosted)], axis=0)
            o = o.transpose(0, 2, 1, 3).reshape(n, c_w)
            new_states.append((r3(k32).reshape(B, T, h_c, d_c), r3(v32).reshape(B, T, h_c, d_c), r3(lf)))
            pending = ((o, wts["odd_out"][i]),)
        last = l == depth - 1
        xs = ffn(xs, g[2:3], (l, 1), pending, wts["final_g"] if last else None)
        pending = ()
    return xs.reshape(B, T, D), new_states


def kernel(x_prompt, x_sample, state_gla, cache_band_k, cache_band_v, cache_fox_k, cache_fox_v, cache_fox_logf,
           norm_g, ffn_w_gu, ffn_w_down, even_w_in, gla_w_alpha_up, gla_b_alpha, gla_norm_g, band_rel_bias,
           even_w_out, odd_w_in, fox_b_f, odd_w_out, final_norm_g):
    depth = norm_g.shape[0]
    _, _, h_a, dk_a, dv_a = state_gla.shape
    h_b, d_b = cache_band_k.shape[-2:]
    h_c, d_c = cache_fox_k.shape[-2:]
    r_a = gla_w_alpha_up.shape[1]
    a_qk, a_v, b_w, c_w = h_a * dk_a, h_a * dv_a, h_b * d_b, h_c * d_c
    assert cache_band_k.shape[2] == BAND_CHUNKS * CHUNK and band_rel_bias.shape[-1] == 2 * MAX_REL + 1

    r0 = 2 * a_qk + 2 * a_v
    n_rel = -(-band_rel_bias.shape[-1] // LANES) * LANES
    rel_padded = jnp.pad(band_rel_bias, ((0, 0), (0, 0), (0, n_rel - band_rel_bias.shape[-1])))
    win = (BAND_CHUNKS + 1) * CHUNK
    odd_t = odd_w_in.transpose(0, 2, 1)
    even_t = even_w_in.transpose(0, 2, 1)
    wts = {
        "norm_g": norm_g,
        "w_gu": ffn_w_gu,
        "w_down": ffn_w_down,
        "ffn_bf16": {(0, 0): [ffn_w_gu[0, 0].astype(BF16), ffn_w_down[0, 0].astype(BF16)]},
        "even": {
            "main": jnp.concatenate([even_w_in[:, :, :r0], even_w_in[:, :, r0 + r_a:]], axis=-1).astype(BF16),
            "r": even_w_in[:, :, r0:r0 + r_a].astype(BF16),
            "up": gla_w_alpha_up.astype(BF16),
            "b_alpha": gla_b_alpha[:, None, :],
        },
        "even_kvt": even_t[:, r0 + r_a + b_w:].astype(BF16),
        "gla_g": gla_norm_g[:, None, :],
        "band_bias": [_bias_table(rel_padded[i], chunk=CHUNK, win=win, rows=BAND_BLOCK_CHUNKS * CHUNK)
                      for i in range(rel_padded.shape[0])],
        "even_out_a": even_w_out[:, :a_v].astype(BF16),
        "even_out_b": even_w_out[:, a_v:].astype(BF16),
        "odd_main": odd_w_in[:, :, :3 * c_w].astype(BF16),
        "odd_f": odd_w_in[:, :, 3 * c_w:].astype(BF16),
        "odd_q": odd_w_in[:, :, :c_w].astype(BF16),
        "odd_kvt": odd_t[:, c_w:3 * c_w].astype(BF16),
        "odd_ft": odd_t[:, 3 * c_w:].astype(BF16),
        "b_f": fox_b_f[:, None, :],
        "b_f_col": fox_b_f[:, :, None],
        "odd_out": odd_w_out.astype(BF16),
        "final_g": final_norm_g[None, :],
    }
    dims = (h_a, dk_a, dv_a, h_b, d_b, h_c, d_c)

    sample_states = []
    wts["hosted_src"], wts["hosted_out"] = [], []
    for l in range(depth):
        i = l // 2
        if l % 2 == 0:
            sample_states.append((state_gla[i], cache_band_k[i], cache_band_v[i]))
        else:
            sample_states.append((cache_fox_k[i], cache_fox_v[i], cache_fox_logf[i]))
            wts["hosted_src"] += [c[i].transpose(0, 2, 3, 1).reshape(-1, c.shape[2]) for c in (cache_fox_k, cache_fox_v)]
    assert len(wts["hosted_src"]) == 2, "the sample's forgetting attention expects one cached odd layer"

    n_sample = x_sample.shape[0] * x_sample.shape[1]
    y_prompt, ns_p = _trunk(x_prompt, None, wts, dims, tm=PROMPT_ROWS, tm_in=IN_PROJ_ROWS)
    y_sample, ns_s = _trunk(x_sample, sample_states, wts, dims, tm=n_sample, tm_in=n_sample)

    ev = range(0, depth, 2)
    od = range(1, depth, 2)
    stack = lambda ns, layers, j: jnp.stack([ns[l][j] for l in layers])
    return (y_prompt, y_sample,
            stack(ns_p, ev, 0), stack(ns_p, ev, 1), stack(ns_p, ev, 2),
            stack(ns_p, od, 0), stack(ns_p, od, 1), stack(ns_p, od, 2),
            stack(ns_s, ev, 0), stack(ns_s, ev, 1), stack(ns_s, ev, 2),
            stack(ns_s, od, 0), stack(ns_s, od, 1), stack(ns_s, od, 2))
```
